```python
import math
import jax, jax.numpy as jnp
from jax import lax
import numpy as np

D_MODEL = 1024
BATCH = 8
SEQ = 2048
DEPTH = 2

MLA_HEADS = 8
QK_NOPE_DIM = 64
QK_ROPE_DIM = 32
QK_HEAD_DIM = QK_NOPE_DIM + QK_ROPE_DIM
V_HEAD_DIM = 64
Q_LORA_RANK = 384
KV_LORA_RANK = 256
ROPE_THETA = 10000.0
ATTN_BLOCK = 128
MLA_OUT = MLA_HEADS * V_HEAD_DIM
POOL_WINDOWS = (2, 4, 8, 16)
POOL_GROUP_DIM = 128
POOL_GROUPS = len(POOL_WINDOWS)
POOL_DIM = POOL_GROUPS * POOL_GROUP_DIM
SSD_HEADS = 16
SSD_HEAD_DIM = 64
SSD_INNER = SSD_HEADS * SSD_HEAD_DIM
SSD_GROUPS = 2
SSD_STATE = 128
SSD_CONV = 4
SSD_CHUNK = 128
SSD_CONV_DIM = SSD_INNER + 2 * SSD_GROUPS * SSD_STATE
FFN_DIM = 2816
FFN_CONV = 3
N_BRANCH = 3
BRANCH_DIM = MLA_OUT + POOL_DIM + SSD_INNER
IN_SIZES = (Q_LORA_RANK, KV_LORA_RANK + QK_ROPE_DIM, POOL_DIM, SSD_INNER, SSD_CONV_DIM, SSD_HEADS, N_BRANCH * D_MODEL)
IN_DIM = sum(IN_SIZES)
EPS = 1e-6

kernel_name = 'hybrid_mla_pool_ssd_gated_block'


def _split(x, sizes):
    outs, start = [], 0
    for size in sizes:
        outs.append(x[..., start:start + size])
        start += size
    return outs


def rms_norm(x, w):
    xf = x.astype(jnp.float32)
    y = xf * lax.rsqrt(jnp.mean(xf * xf, axis=-1, keepdims=True) + EPS)
    return (y * w.astype(jnp.float32)).astype(x.dtype)


def modulate(h, shift, scale):
    return h * (1 + scale[:, None, :]) + shift[:, None, :]


def causal_dwconv(x, w, b):
    k = w.shape[0]
    y = lax.conv_general_dilated(x, w[:, None, :].astype(x.dtype), window_strides=(1,), padding=[(k - 1, 0)],
                                 dimension_numbers=('NWC', 'WIO', 'NWC'), feature_group_count=x.shape[-1])
    return y + b.astype(x.dtype)


def rope_tables(positions):
    inv_freq = ROPE_THETA ** (-jnp.arange(0, QK_ROPE_DIM, 2, dtype=jnp.float32) / QK_ROPE_DIM)
    ang = positions.astype(jnp.float32)[..., None] * inv_freq
    return jnp.cos(ang), jnp.sin(ang)


def apply_rope(x, cos, sin):
    half = x.shape[-1] // 2
    x1, x2 = x[..., :half], x[..., half:]
    return jnp.concatenate([x1 * cos - x2 * sin, x2 * cos + x1 * sin], axis=-1).astype(x.dtype)


def mla_branch(q_lat, kv_lat, cos, sin, q_a_norm, w_q_b, kv_a_norm, w_kv_b, q_norm, k_norm):
    b, s, _ = q_lat.shape
    q = (rms_norm(q_lat, q_a_norm) @ w_q_b).reshape(b, s, MLA_HEADS, QK_HEAD_DIM)
    q_nope, q_rope = q[..., :QK_NOPE_DIM], q[..., QK_NOPE_DIM:]
    c_kv, k_rope = kv_lat[..., :KV_LORA_RANK], kv_lat[..., KV_LORA_RANK:]
    kv = (rms_norm(c_kv, kv_a_norm) @ w_kv_b).reshape(b, s, MLA_HEADS, QK_NOPE_DIM + V_HEAD_DIM)
    k_nope, v = kv[..., :QK_NOPE_DIM], kv[..., QK_NOPE_DIM:]
    q_nope = rms_norm(q_nope, q_norm[:QK_NOPE_DIM])
    q_rope = apply_rope(rms_norm(q_rope, q_norm[QK_NOPE_DIM:]), cos[:, :, None], sin[:, :, None])
    k_nope = rms_norm(k_nope, k_norm[:QK_NOPE_DIM])
    k_rope = apply_rope(rms_norm(k_rope, k_norm[QK_NOPE_DIM:]), cos, sin)
    n_blk = s // ATTN_BLOCK
    qn = q_nope.reshape(b, n_blk, ATTN_BLOCK, MLA_HEADS, QK_NOPE_DIM).swapaxes(0, 1)
    qr = q_rope.reshape(b, n_blk, ATTN_BLOCK, MLA_HEADS, QK_ROPE_DIM).swapaxes(0, 1)
    key_pos = jnp.arange(s)
    scale = QK_HEAD_DIM ** -0.5

    def attend(args):
        qn_b, qr_b, blk = args
        sc = jnp.einsum('bqhd,bkhd->bhqk', qn_b, k_nope) + jnp.einsum('bqhd,bkd->bhqk', qr_b, k_rope)
        sc = sc.astype(jnp.float32) * scale
        q_pos = blk * ATTN_BLOCK + jnp.arange(ATTN_BLOCK)
        sc = jnp.where(key_pos[None, :] <= q_pos[:, None], sc, -jnp.inf)
        p = jax.nn.softmax(sc, axis=-1).astype(v.dtype)
        return jnp.einsum('bhqk,bkhd->bqhd', p, v)

    o = lax.map(attend, (qn, qr, jnp.arange(n_blk)))
    return o.swapaxes(0, 1).reshape(b, s, MLA_OUT)


def pool_branch(u, pool_w, pool_scale):
    b, s, _ = u.shape
    uf = u.astype(jnp.float32).reshape(b, s, POOL_GROUPS, POOL_GROUP_DIM)
    cs = jnp.cumsum(uf, axis=1)
    t = jnp.arange(s)
    outs = []
    for g, w in enumerate(POOL_WINDOWS):
        csg = cs[:, :, g]
        lagged = jnp.pad(csg, ((0, 0), (w, 0), (0, 0)))[:, :s]
        count = jnp.minimum(t + 1, w).astype(jnp.float32)[None, :, None]
        outs.append((csg - lagged) / count - uf[:, :, g])
    pooled = jnp.stack(outs, axis=2).astype(u.dtype)
    mixed = jnp.einsum('bsgc,gcd->bsgd', pooled, pool_w)
    return mixed.reshape(b, s, POOL_DIM) * pool_scale


def ssd_branch(z, xbc, dt_raw, conv_w, conv_b, dt_bias, a_log, d_skip, norm_w):
    b, s, _ = z.shape
    nc, lc = s // SSD_CHUNK, SSD_CHUNK
    r = SSD_HEADS // SSD_GROUPS
    xbc = jax.nn.silu(causal_dwconv(xbc, conv_w, conv_b))
    xs, bm, cm = _split(xbc, (SSD_INNER, SSD_GROUPS * SSD_STATE, SSD_GROUPS * SSD_STATE))
    dt = jax.nn.softplus(dt_raw.astype(jnp.float32) + dt_bias.astype(jnp.float32))
    a = -jnp.exp(a_log.astype(jnp.float32))
    x_h = xs.reshape(b, s, SSD_HEADS, SSD_HEAD_DIM)
    xdt = (x_h.astype(jnp.float32) * dt[..., None]).reshape(b, nc, lc, SSD_GROUPS, r, SSD_HEAD_DIM)
    bc = bm.astype(jnp.float32).reshape(b, nc, lc, SSD_GROUPS, SSD_STATE)
    cc = cm.astype(jnp.float32).reshape(b, nc, lc, SSD_GROUPS, SSD_STATE)
    da = (dt * a).reshape(b, nc, lc, SSD_GROUPS, r).transpose(0, 3, 4, 1, 2)
    a_cs = jnp.cumsum(da, axis=-1)
    causal = jnp.tril(jnp.ones((lc, lc), dtype=bool))
    seg = a_cs[..., :, None] - a_cs[..., None, :]
    decay = jnp.exp(jnp.where(causal, seg, -jnp.inf))
    cb = jnp.einsum('bclgn,bcsgn->bcgls', cc, bc)
    y_diag = jnp.einsum('bcgls,bgrcls,bcsgrp->bclgrp', cb, decay, xdt)
    decay_states = jnp.exp(a_cs[..., -1:] - a_cs)
    states = jnp.einsum('bclgn,bgrcl,bclgrp->bcgrpn', bc, decay_states, xdt)
    chunk_decay = jnp.exp(a_cs[..., -1])

    def step(h, inp):
        st, dc = inp
        return h * dc[..., None, None] + st, h

    h0 = jnp.zeros((b, SSD_GROUPS, r, SSD_HEAD_DIM, SSD_STATE), jnp.float32)
    _, prev = lax.scan(step, h0, (states.transpose(1, 0, 2, 3, 4, 5), chunk_decay.transpose(3, 0, 1, 2)))
    y_off = jnp.einsum('bclgn,cbgrpn,bgrcl->bclgrp', cc, prev, jnp.exp(a_cs))
    y = (y_diag + y_off).reshape(b, s, SSD_HEADS, SSD_HEAD_DIM) + x_h * d_skip[:, None]
    gated = y.reshape(b, s, SSD_INNER) * jax.nn.silu(z.astype(jnp.float32))
    gated = gated.reshape(b, s, SSD_GROUPS, SSD_INNER // SSD_GROUPS)
    gated = gated * lax.rsqrt(jnp.mean(gated * gated, axis=-1, keepdims=True) + EPS)
    return (gated.reshape(b, s, SSD_INNER) * norm_w.astype(jnp.float32)).astype(z.dtype)


def setup_inputs(seed: int = 0) -> dict:
    key = jax.random.key(seed)
    ks = iter(jax.random.split(key, 40))
    L, D = DEPTH, D_MODEL

    def nrm(shape, scale):
        return jax.random.normal(next(ks), shape, jnp.float32) * scale

    x = nrm((BATCH, SEQ, D), 1.0)
    c = nrm((BATCH, D), 1.0)
    offset = jax.random.randint(next(ks), (BATCH, 1), 0, 4096, dtype=jnp.int32)
    positions = offset + jnp.arange(SEQ, dtype=jnp.int32)[None, :]
    ada_w = nrm((L, D, 6 * D), D ** -0.5)
    ada_b = nrm((L, 6 * D), 0.02)
    norm1_w = 1.0 + nrm((L, D), 0.02)
    w_in = nrm((L, D, IN_DIM), D ** -0.5)
    q_a_norm = 1.0 + nrm((L, Q_LORA_RANK), 0.02)
    w_q_b = nrm((L, Q_LORA_RANK, MLA_HEADS * QK_HEAD_DIM), Q_LORA_RANK ** -0.5)
    kv_a_norm = 1.0 + nrm((L, KV_LORA_RANK), 0.02)
    w_kv_b = nrm((L, KV_LORA_RANK, MLA_HEADS * (QK_NOPE_DIM + V_HEAD_DIM)), KV_LORA_RANK ** -0.5)
    q_norm = 1.0 + nrm((L, QK_HEAD_DIM), 0.02)
    k_norm = 1.0 + nrm((L, QK_HEAD_DIM), 0.02)
    pool_w = nrm((L, POOL_GROUPS, POOL_GROUP_DIM, POOL_GROUP_DIM), POOL_GROUP_DIM ** -0.5)
    pool_scale = 1.0 + nrm((L, POOL_DIM), 0.1)
    ssd_conv_w = nrm((L, SSD_CONV, SSD_CONV_DIM), SSD_CONV ** -0.5)
    ssd_conv_b = nrm((L, SSD_CONV_DIM), 0.02)
    dt0 = jnp.exp(jax.random.uniform(next(ks), (L, SSD_HEADS), jnp.float32, math.log(1e-3), math.log(1e-1)))
    ssd_dt_bias = dt0 + jnp.log(-jnp.expm1(-dt0))
    ssd_a_log = jnp.log(jax.random.uniform(next(ks), (L, SSD_HEADS), jnp.float32, 1.0, 16.0))
    ssd_d = 1.0 + nrm((L, SSD_HEADS), 0.1)
    ssd_norm_w = 1.0 + nrm((L, SSD_INNER), 0.02)
    w_branch = nrm((L, BRANCH_DIM, D), MLA_OUT ** -0.5)
    w_out = nrm((L, D, D), D ** -0.5)
    norm2_w = 1.0 + nrm((L, D), 0.02)
    ffn_up = nrm((L, D, 2 * FFN_DIM), D ** -0.5)
    ffn_conv_w = nrm((L, FFN_CONV, 2 * FFN_DIM), FFN_CONV ** -0.5)
    ffn_conv_b = nrm((L, 2 * FFN_DIM), 0.02)
    ffn_down = nrm((L, FFN_DIM, D), FFN_DIM ** -0.5)
    return {'x': x, 'c': c, 'positions': positions, 'ada_w': ada_w, 'ada_b': ada_b, 'norm1_w': norm1_w,
            'w_in': w_in, 'q_a_norm': q_a_norm, 'w_q_b': w_q_b, 'kv_a_norm': kv_a_norm, 'w_kv_b': w_kv_b,
            'q_norm': q_norm, 'k_norm': k_norm, 'pool_w': pool_w, 'pool_scale': pool_scale,
            'ssd_conv_w': ssd_conv_w, 'ssd_conv_b': ssd_conv_b, 'ssd_dt_bias': ssd_dt_bias,
            'ssd_a_log': ssd_a_log, 'ssd_d': ssd_d, 'ssd_norm_w': ssd_norm_w, 'w_branch': w_branch,
            'w_out': w_out, 'norm2_w': norm2_w, 'ffn_up': ffn_up, 'ffn_conv_w': ffn_conv_w,
            'ffn_conv_b': ffn_conv_b, 'ffn_down': ffn_down}


def reference(x, c, positions, ada_w, ada_b, norm1_w, w_in, q_a_norm, w_q_b, kv_a_norm, w_kv_b, q_norm, k_norm,
              pool_w, pool_scale, ssd_conv_w, ssd_conv_b, ssd_dt_bias, ssd_a_log, ssd_d, ssd_norm_w, w_branch,
              w_out, norm2_w, ffn_up, ffn_conv_w, ffn_conv_b, ffn_down):
    cos, sin = rope_tables(positions)
    c_act = jax.nn.silu(c)
    for l in range(DEPTH):
        mod = c_act @ ada_w[l] + ada_b[l]
        sh1, sc1, g1, sh2, sc2, g2 = jnp.split(mod, 6, axis=-1)
        h = modulate(rms_norm(x, norm1_w[l]), sh1, sc1)
        proj = h @ w_in[l]
        q_lat, kv_lat, u_pool, z, xbc, dt_raw, gate_logits = _split(proj, IN_SIZES)
        o_a = mla_branch(q_lat, kv_lat, cos, sin, q_a_norm[l], w_q_b[l], kv_a_norm[l], w_kv_b[l],
                         q_norm[l], k_norm[l])
        o_b = pool_branch(u_pool, pool_w[l], pool_scale[l])
        o_c = ssd_branch(z, xbc, dt_raw, ssd_conv_w[l], ssd_conv_b[l], ssd_dt_bias[l], ssd_a_log[l],
                         ssd_d[l], ssd_norm_w[l])
        wb = w_branch[l]
        y_a = o_a @ wb[:MLA_OUT]
        y_b = o_b @ wb[MLA_OUT:MLA_OUT + POOL_DIM]
        y_c = o_c @ wb[MLA_OUT + POOL_DIM:]
        gate_a, gate_b, gate_c = jnp.split(jax.nn.sigmoid(gate_logits), N_BRANCH, axis=-1)
        merged = gate_a * y_a + gate_b * y_b + gate_c * y_c
        x = x + g1[:, None, :] * (merged @ w_out[l])
        h = modulate(rms_norm(x, norm2_w[l]), sh2, sc2)
        up = causal_dwconv(h @ ffn_up[l], ffn_conv_w[l], ffn_conv_b[l])
        u_gate, u_val = jnp.split(up, 2, axis=-1)
        x = x + g2[:, None, :] * ((jax.nn.silu(u_gate) * u_val) @ ffn_down[l])
    return x
```

```python
import functools
import math

import jax
import jax.numpy as jnp
from jax import lax
from jax.experimental import pallas as pl
from jax.experimental.pallas import tpu as pltpu

F32 = jnp.float32
BF16 = jnp.bfloat16

D_MODEL = 1024
MLA_HEADS = 8
QK_NOPE = 64
QK_ROPE = 32
QK_HEAD = QK_NOPE + QK_ROPE
V_HEAD = 64
Q_LORA = 384
KV_LORA = 256
ROPE_THETA = 10000.0
MLA_OUT = MLA_HEADS * V_HEAD
POOL_WINDOWS = (2, 4, 8, 16)
POOL_GROUP = 128
POOL_DIM = len(POOL_WINDOWS) * POOL_GROUP
SSD_HEADS = 16
SSD_HEAD_DIM = 64
SSD_INNER = SSD_HEADS * SSD_HEAD_DIM
SSD_GROUPS = 2
SSD_STATE = 128
SSD_CONV = 4
SSD_CHUNK = 128
FFN_DIM = 2816
FFN_CONV = 3
N_BRANCH = 3
EPS = 1e-6

LANES = 128
SUBLANES = 8
HEAD_PAD = 128
NEG_BIG = -1e30

P_Z = 0
P_XS = P_Z + SSD_INNER
P_GATE = P_XS + SSD_INNER
P_POOL = P_GATE + N_BRANCH * D_MODEL
P_BC = P_POOL + POOL_DIM
P_LAT = P_BC + 2 * SSD_GROUPS * SSD_STATE
LAT_W = Q_LORA + KV_LORA + HEAD_PAD
PROJ_W = P_LAT + LAT_W
DT_PAD = 128

FFN_CHUNK = 256
FFN_NCHUNK = FFN_DIM // FFN_CHUNK

VMEM_LIMIT = 56 * 1024 * 1024


def _cparams(sem):
    return pltpu.CompilerParams(dimension_semantics=sem, vmem_limit_bytes=VMEM_LIMIT)


def _resident(shape):
    nd = len(shape)
    return pl.BlockSpec(shape, lambda *_: (0,) * nd, pipeline_mode=pl.Buffered(1))


def _silu(v):
    return v * jax.nn.sigmoid(v)


def _norm_mod(x, w, shift, scale):
    y = x * lax.rsqrt(jnp.mean(x * x, axis=-1, keepdims=True) + EPS)
    return (y * w) * (1.0 + scale) + shift


def _split_dot(x, m):
    hi = x.astype(BF16)
    lo = (x - hi.astype(F32)).astype(BF16)
    return (jnp.dot(hi, m, preferred_element_type=F32)
            + jnp.dot(lo, m, preferred_element_type=F32))


def _mods_kernel(c_ref, w_ref, b_ref, o_ref):
    ca = _silu(c_ref[...]).astype(BF16)
    w = w_ref[0].astype(BF16)
    o_ref[0] = jnp.dot(ca, w, preferred_element_type=F32) + b_ref[0]


def _mods(c, ada_w, ada_b):
    depth, d, n = ada_w.shape
    b = c.shape[0]
    tn = 1536
    return pl.pallas_call(
        _mods_kernel,
        grid=(depth, n // tn),
        in_specs=[pl.BlockSpec((b, d), lambda l, j: (0, 0)),
                  pl.BlockSpec((1, d, tn), lambda l, j: (l, 0, j)),
                  pl.BlockSpec((1, 1, tn), lambda l, j: (l, 0, j))],
        out_specs=pl.BlockSpec((1, b, tn), lambda l, j: (l, 0, j)),
        out_shape=jax.ShapeDtypeStruct((depth, b, n), F32),
        compiler_params=_cparams(("arbitrary", "arbitrary")),
        name="adaln_mods",
    )(c, ada_w, ada_b.reshape(depth, 1, n))


def _rope_kernel(pos_ref, freq_ref, sign_ref, c_ref, s_ref):
    ang = pos_ref[...] * freq_ref[...]
    c_ref[...] = jnp.cos(ang)
    s_ref[...] = jnp.sin(ang) * sign_ref[...]


def _rope_tables(positions):
    t = positions.size
    tm = 2048
    half = QK_ROPE // 2
    inv_freq = ROPE_THETA ** (-jnp.arange(0, QK_ROPE, 2, dtype=F32) / QK_ROPE)
    zeros = jnp.zeros((QK_NOPE,), F32)
    tail = jnp.zeros((HEAD_PAD - QK_HEAD,), F32)
    freq = jnp.concatenate([zeros, inv_freq, inv_freq, tail]).reshape(1, HEAD_PAD)
    sign = jnp.concatenate([zeros, -jnp.ones((half,), F32), jnp.ones((half,), F32), tail]).reshape(1, HEAD_PAD)
    pos = positions.astype(F32).reshape(t, 1)
    row = pl.BlockSpec((1, HEAD_PAD), lambda i: (0, 0))
    tab = pl.BlockSpec((tm, HEAD_PAD), lambda i: (i, 0))
    return pl.pallas_call(
        _rope_kernel,
        grid=(t // tm,),
        in_specs=[pl.BlockSpec((tm, 1), lambda i: (i, 0)), row, row],
        out_specs=[tab, tab],
        out_shape=[jax.ShapeDtypeStruct((t, HEAD_PAD), F32)] * 2,
        compiler_params=_cparams(("arbitrary",)),
        name="rope_tables",
    )(pos, freq, sign)


def _prenorm_kernel(x_ref, mod_ref, w_ref, h_ref):
    m = mod_ref[0]
    h_ref[...] = _norm_mod(x_ref[...], w_ref[...], m[0:1], m[1:2]).astype(BF16)


def _prenorm(xf, mod_l, norm_w, seq):
    t, d = xf.shape
    tm = 1024
    per_seq = seq // tm
    return pl.pallas_call(
        _prenorm_kernel,
        grid=(t // tm,),
        in_specs=[pl.BlockSpec((tm, d), lambda i: (i, 0)),
                  pl.BlockSpec((1, 6, d), lambda i: (i // per_seq, 0, 0)),
                  pl.BlockSpec((1, d), lambda i: (0, 0))],
        out_specs=pl.BlockSpec((tm, d), lambda i: (i, 0)),
        out_shape=jax.ShapeDtypeStruct((t, d), BF16),
        compiler_params=_cparams(("arbitrary",)),
        name="prenorm",
    )(xf, mod_l, norm_w.reshape(1, d))


INPROJ_COLS = 768


def _inproj_kernel(h_ref, w_ref, wdt_ref, proj_ref, dt_ref):
    h = h_ref[...]
    for c0 in range(0, PROJ_W, INPROJ_COLS):
        proj_ref[:, c0:c0 + INPROJ_COLS] = jnp.dot(
            h, w_ref[:, c0:c0 + INPROJ_COLS], preferred_element_type=F32).astype(BF16)
    dt_ref[...] = jnp.dot(h, wdt_ref[...], preferred_element_type=F32)


def _inproj(h, w_main, w_dt):
    t, d = h.shape
    tm = 512
    return pl.pallas_call(
        _inproj_kernel,
        grid=(t // tm,),
        in_specs=[pl.BlockSpec((tm, d), lambda i: (i, 0)),
                  _resident((d, PROJ_W)),
                  _resident((d, DT_PAD))],
        out_specs=[pl.BlockSpec((tm, PROJ_W), lambda i: (i, 0)),
                   pl.BlockSpec((tm, DT_PAD), lambda i: (i, 0))],
        out_shape=[jax.ShapeDtypeStruct((t, PROJ_W), BF16),
                   jax.ShapeDtypeStruct((t, DT_PAD), F32)],
        compiler_params=_cparams(("arbitrary",)),
        name="in_projection",
    )(h, w_main, w_dt)


def _rope_apply(y, cos, sin_signed, lane):
    half = QK_ROPE // 2
    partner = jnp.where(lane < QK_NOPE + half,
                        pltpu.roll(y, HEAD_PAD - half, 1),
                        pltpu.roll(y, half, 1))
    return y * cos + partner * sin_signed


def _mla_prep_kernel(lat_ref, cos_ref, sin_ref, qan_ref, kvan_ref, wq_ref, wkv_ref,
                     gq_ref, gk_ref, qw_ref, kw_ref, krw_ref, q_ref, k_ref, v_ref):
    lat = lat_ref[...].astype(F32)
    q_lat = lat[:, :Q_LORA]
    c_kv = lat[:, Q_LORA:Q_LORA + KV_LORA]
    k_rope = lat[:, Q_LORA + KV_LORA:]
    cos = cos_ref[...]
    sin = sin_ref[...]
    lane = lax.broadcasted_iota(jnp.int32, cos.shape, 1)

    qn = q_lat * lax.rsqrt(jnp.mean(q_lat * q_lat, axis=-1, keepdims=True) + EPS) * qan_ref[...]
    cn = c_kv * lax.rsqrt(jnp.mean(c_kv * c_kv, axis=-1, keepdims=True) + EPS) * kvan_ref[...]
    q = jnp.dot(qn.astype(BF16), wq_ref[...], preferred_element_type=F32)
    kv = jnp.dot(cn.astype(BF16), wkv_ref[...], preferred_element_type=F32)
    kw = MLA_HEADS * HEAD_PAD
    v_ref[...] = kv[:, kw:].astype(BF16)

    kr_ms = jnp.sum(k_rope * k_rope, axis=-1, keepdims=True) * (1.0 / QK_ROPE)
    kr = _rope_apply(k_rope * lax.rsqrt(kr_ms + EPS) * krw_ref[...], cos, sin, lane)

    pair = 2 * HEAD_PAD
    for c0 in range(0, kw, pair):
        xq = q[:, c0:c0 + pair]
        ms = jnp.dot((xq * xq).astype(BF16), gq_ref[...], preferred_element_type=F32)
        yq = xq * lax.rsqrt(ms + EPS) * qw_ref[:, c0:c0 + pair]
        xk = kv[:, c0:c0 + pair]
        ms = jnp.dot((xk * xk).astype(BF16), gk_ref[...], preferred_element_type=F32)
        yk = xk * lax.rsqrt(ms + EPS) * kw_ref[:, c0:c0 + pair]
        for h0 in range(0, pair, HEAD_PAD):
            q_ref[:, c0 + h0:c0 + h0 + HEAD_PAD] = _rope_apply(
                yq[:, h0:h0 + HEAD_PAD], cos, sin, lane).astype(BF16)
            k_ref[:, c0 + h0:c0 + h0 + HEAD_PAD] = (yk[:, h0:h0 + HEAD_PAD] + kr).astype(BF16)


def _mla_prep(proj, cos_t, sin_t, qan, kvan, wq, wkv, gq, gk, qw, kw, krw):
    t = proj.shape[0]
    tm = 512
    hw = MLA_HEADS * HEAD_PAD
    row = lambda n: pl.BlockSpec((1, n), lambda i: (0, 0))
    return pl.pallas_call(
        _mla_prep_kernel,
        grid=(t // tm,),
        in_specs=[pl.BlockSpec((tm, LAT_W), lambda i: (i, P_LAT // LAT_W)),
                  pl.BlockSpec((tm, HEAD_PAD), lambda i: (i, 0)),
                  pl.BlockSpec((tm, HEAD_PAD), lambda i: (i, 0)),
                  row(Q_LORA), row(KV_LORA),
                  _resident(wq.shape), _resident(wkv.shape),
                  _resident(gq.shape), _resident(gk.shape),
                  row(hw), row(hw), row(HEAD_PAD)],
        out_specs=[pl.BlockSpec((tm, hw), lambda i: (i, 0)),
                   pl.BlockSpec((tm, hw), lambda i: (i, 0)),
                   pl.BlockSpec((tm, MLA_OUT), lambda i: (i, 0))],
        out_shape=[jax.ShapeDtypeStruct((t, hw), BF16),
                   jax.ShapeDtypeStruct((t, hw), BF16),
                   jax.ShapeDtypeStruct((t, MLA_OUT), BF16)],
        compiler_params=_cparams(("arbitrary",)),
        name="mla_prep",
    )(proj, cos_t, sin_t, qan, kvan, wq, wkv, gq, gk, qw, kw, krw)


ATT_TILE = 512


def _attn_kernel(q_ref, k_ref, v_ref, o_ref):
    seq = q_ref.shape[0]
    tq = ATT_TILE
    lane = lax.broadcasted_iota(jnp.int32, (tq, 2 * V_HEAD), 1)
    rows = lax.broadcasted_iota(jnp.int32, (tq, tq), 0)
    cols = lax.broadcasted_iota(jnp.int32, (tq, tq), 1)
    causal = cols <= rows
    nt = (((1,), (1,)), ((), ()))

    for qi in range(seq // tq):
        q0 = qi * tq
        outs = []
        for hh in range(2):
            hs = slice(hh * HEAD_PAD, (hh + 1) * HEAD_PAD)
            q = q_ref[q0:q0 + tq, hs]

            def step(k0, carry, masked):
                m, l, acc = carry
                kb = k_ref[pl.ds(k0, tq), hs]
                s = lax.dot_general(q, kb, nt, preferred_element_type=F32)
                if masked:
                    s = jnp.where(causal, s, NEG_BIG)
                m_new = jnp.maximum(m, jnp.max(s, axis=-1, keepdims=True))
                alpha = jnp.exp(m - m_new)
                p = jnp.exp(s - m_new)
                l = alpha * l + jnp.sum(p, axis=-1, keepdims=True)
                acc = alpha * acc + jnp.dot(p.astype(BF16), v_ref[pl.ds(k0, tq), :],
                                            preferred_element_type=F32)
                return m_new, l, acc

            carry = (jnp.full((tq, 1), NEG_BIG, F32), jnp.zeros((tq, 1), F32),
                     jnp.zeros((tq, 2 * V_HEAD), F32))
            if qi > 0:
                carry = lax.fori_loop(
                    0, qi, lambda j, cr: step(pl.multiple_of(j * tq, tq), cr, False), carry)
            m, l, acc = step(q0, carry, True)
            outs.append(acc / l)
        o_ref[q0:q0 + tq, :] = jnp.where(lane < V_HEAD, outs[0], outs[1]).astype(BF16)


def _attention(q, k, v, batch, seq):
    t = q.shape[0]
    pairs = MLA_HEADS // 2
    return pl.pallas_call(
        _attn_kernel,
        grid=(batch, pairs),
        in_specs=[pl.BlockSpec((seq, 2 * HEAD_PAD), lambda b, p: (b, p)),
                  pl.BlockSpec((seq, 2 * HEAD_PAD), lambda b, p: (b, p)),
                  pl.BlockSpec((seq, 2 * V_HEAD), lambda b, p: (b, p))],
        out_specs=pl.BlockSpec((seq, 2 * V_HEAD), lambda b, p: (b, p)),
        out_shape=jax.ShapeDtypeStruct((t, MLA_OUT), BF16),
        compiler_params=_cparams(("arbitrary", "arbitrary")),
        name="mla_attention",
    )(q, k, v)


POOL_HALO = 16


def _pool_kernel(u_ref, pw_ref, ps_ref, o_ref, ext_ref):
    ts = u_ref.shape[0]
    j = pl.program_id(1)

    @pl.when(j == 0)
    def _():
        ext_ref[0:POOL_HALO, :] = jnp.zeros((POOL_HALO, POOL_DIM), F32)

    u = u_ref[...].astype(F32)
    ext_ref[POOL_HALO:POOL_HALO + ts, :] = u
    t_pos = j * ts + lax.broadcasted_iota(jnp.int32, (ts, 1), 0)
    for g, w in enumerate(POOL_WINDOWS):
        gs = slice(g * POOL_GROUP, (g + 1) * POOL_GROUP)
        ug = u[:, gs]
        acc = ug
        for dlt in range(1, w):
            acc = acc + ext_ref[POOL_HALO - dlt:POOL_HALO - dlt + ts, gs]
        count = jnp.minimum(t_pos + 1, w).astype(F32)
        pooled = acc / count - ug
        mixed = jnp.dot(pooled.astype(BF16), pw_ref[g], preferred_element_type=F32)
        o_ref[:, gs] = (mixed * ps_ref[:, gs]).astype(BF16)
    ext_ref[0:POOL_HALO, :] = ext_ref[ts:ts + POOL_HALO, :]


def _pool(proj, pool_w, pool_scale, batch, seq):
    t = proj.shape[0]
    ts = 512
    per_seq = seq // ts
    return pl.pallas_call(
        _pool_kernel,
        grid=(batch, per_seq),
        in_specs=[pl.BlockSpec((ts, POOL_DIM), lambda b, j: (b * per_seq + j, P_POOL // POOL_DIM)),
                  _resident(pool_w.shape),
                  pl.BlockSpec((1, POOL_DIM), lambda b, j: (0, 0))],
        out_specs=pl.BlockSpec((ts, POOL_DIM), lambda b, j: (b * per_seq + j, 0)),
        out_shape=jax.ShapeDtypeStruct((t, POOL_DIM), BF16),
        scratch_shapes=[pltpu.VMEM((ts + POOL_HALO, POOL_DIM), F32)],
        compiler_params=_cparams(("arbitrary", "arbitrary")),
        name="pool_branch",
    )(proj, pool_w, pool_scale)


CONV_HALO = 8
BC_W = 2 * SSD_GROUPS * SSD_STATE
GROUP_W = SSD_INNER // SSD_GROUPS


def _causal_conv(ext_ref, x, w_ref, b_ref, taps):
    n = x.shape[0]
    out = w_ref[taps - 1:taps, :] * x + b_ref[...]
    for k in range(taps - 1):
        off = CONV_HALO - (taps - 1) + k
        out = out + w_ref[k:k + 1, :] * ext_ref[off:off + n, :]
    return out


def _ssd_kernel(z_ref, xs_ref, bc_ref, dt_ref, cwx_ref, cbx_ref, cwb_ref, cbb_ref,
                dtb_ref, alog_ref, dful_ref, nw_ref, exp_ref, tri_ref, o_ref,
                extx_ref, extb_ref, state_ref, gated_ref):
    lc = SSD_CHUNK
    j = pl.program_id(1)

    @pl.when(j == 0)
    def _():
        extx_ref[0:CONV_HALO, :] = jnp.zeros((CONV_HALO, SSD_INNER), F32)
        extb_ref[0:CONV_HALO, :] = jnp.zeros((CONV_HALO, BC_W), F32)
        state_ref[...] = jnp.zeros(state_ref.shape, F32)

    x_raw = xs_ref[...].astype(F32)
    b_raw = bc_ref[...].astype(F32)
    extx_ref[CONV_HALO:CONV_HALO + lc, :] = x_raw
    extb_ref[CONV_HALO:CONV_HALO + lc, :] = b_raw
    xs = _silu(_causal_conv(extx_ref, x_raw, cwx_ref, cbx_ref, SSD_CONV))
    bc = _silu(_causal_conv(extb_ref, b_raw, cwb_ref, cbb_ref, SSD_CONV))
    extx_ref[0:CONV_HALO, :] = extx_ref[lc:lc + CONV_HALO, :]
    extb_ref[0:CONV_HALO, :] = extb_ref[lc:lc + CONV_HALO, :]
    bm = bc[:, :SSD_GROUPS * SSD_STATE].astype(BF16)
    cm = bc[:, SSD_GROUPS * SSD_STATE:].astype(BF16)

    v = dt_ref[...] + dtb_ref[...]
    dt = jnp.maximum(v, 0.0) + jnp.log1p(jnp.exp(-jnp.abs(v)))
    da = dt * (-jnp.exp(alog_ref[...]))
    a_cs = _split_dot_left(tri_ref[...], da)
    a_last = a_cs[lc - 1:lc, :]
    stacked = jnp.concatenate([dt, jnp.exp(a_cs), jnp.exp(a_last - a_cs)], axis=0)
    full = _split_dot(stacked, exp_ref[...])
    dt_full = full[0:lc]
    ea_full = full[lc:2 * lc]
    ds_full = full[2 * lc:3 * lc]
    chunk_decay = ea_full[lc - 1:lc, :]
    a_cs_t = a_cs.T

    xdt = xs * dt_full
    xdt_b = xdt.astype(BF16)
    xds_b = (xdt * ds_full).astype(BF16)

    rows = lax.broadcasted_iota(jnp.int32, (lc, lc), 0)
    cols = lax.broadcasted_iota(jnp.int32, (lc, lc), 1)
    causal = cols <= rows
    lane = lax.broadcasted_iota(jnp.int32, (lc, 2 * SSD_HEAD_DIM), 1)
    nt = (((1,), (1,)), ((), ()))
    tn = (((0,), (0,)), ((), ()))
    heads_per_group = SSD_HEADS // SSD_GROUPS

    for g in range(SSD_GROUPS):
        gs = slice(g * GROUP_W, (g + 1) * GROUP_W)
        bg = bm[:, g * SSD_STATE:(g + 1) * SSD_STATE]
        cg = cm[:, g * SSD_STATE:(g + 1) * SSD_STATE]
        cb = lax.dot_general(cg, bg, nt, preferred_element_type=F32)
        state = state_ref[g]
        y_off = jnp.dot(cg, state.astype(BF16), preferred_element_type=F32) * ea_full[:, gs]
        s_new = lax.dot_general(bg, xds_b[:, gs], tn, preferred_element_type=F32)
        state_ref[g] = state * chunk_decay[:, gs] + s_new
        for pp in range(heads_per_group // 2):
            h0 = g * heads_per_group + 2 * pp
            ps = slice(h0 * SSD_HEAD_DIM, (h0 + 2) * SSD_HEAD_DIM)
            ys = []
            for hd in (h0, h0 + 1):
                seg = a_cs[:, hd:hd + 1] - a_cs_t[hd:hd + 1, :]
                decay = jnp.exp(jnp.where(causal, seg, NEG_BIG))
                ys.append(jnp.dot((cb * decay).astype(BF16), xdt_b[:, ps],
                                  preferred_element_type=F32))
            y_pair = jnp.where(lane < SSD_HEAD_DIM, ys[0], ys[1])
            ls = slice(2 * pp * SSD_HEAD_DIM, (2 * pp + 2) * SSD_HEAD_DIM)
            y = y_pair + y_off[:, ls] + xs[:, ps] * dful_ref[:, ps]
            gated_ref[:, ps] = y * _silu(z_ref[:, ps].astype(F32))

    for g in range(SSD_GROUPS):
        gs = slice(g * GROUP_W, (g + 1) * GROUP_W)
        gt = gated_ref[:, gs]
        ms = jnp.mean(gt * gt, axis=-1, keepdims=True)
        o_ref[:, gs] = (gt * lax.rsqrt(ms + EPS) * nw_ref[:, gs]).astype(o_ref.dtype)


def _split_dot_left(m, x):
    hi = x.astype(BF16)
    lo = (x - hi.astype(F32)).astype(BF16)
    return (jnp.dot(m, hi, preferred_element_type=F32)
            + jnp.dot(m, lo, preferred_element_type=F32))


def _ssd(proj, dt, cwx, cbx, cwb, cbb, dtb, alog, dful, nw, expand, tri, batch, seq):
    t = proj.shape[0]
    lc = SSD_CHUNK
    nc = seq // lc
    row = lambda n: pl.BlockSpec((1, n), lambda b, j: (0, 0))
    taps = lambda n: pl.BlockSpec((SSD_CONV, n), lambda b, j: (0, 0))
    return pl.pallas_call(
        _ssd_kernel,
        grid=(batch, nc),
        in_specs=[pl.BlockSpec((lc, SSD_INNER), lambda b, j: (b * nc + j, P_Z // SSD_INNER)),
                  pl.BlockSpec((lc, SSD_INNER), lambda b, j: (b * nc + j, P_XS // SSD_INNER)),
                  pl.BlockSpec((lc, BC_W), lambda b, j: (b * nc + j, P_BC // BC_W)),
                  pl.BlockSpec((lc, DT_PAD), lambda b, j: (b * nc + j, 0)),
                  taps(SSD_INNER), row(SSD_INNER), taps(BC_W), row(BC_W),
                  row(DT_PAD), row(DT_PAD), row(SSD_INNER), row(SSD_INNER),
                  _resident(expand.shape), _resident(tri.shape)],
        out_specs=pl.BlockSpec((lc, SSD_INNER), lambda b, j: (b * nc + j, 0)),
        out_shape=jax.ShapeDtypeStruct((t, SSD_INNER), BF16),
        scratch_shapes=[pltpu.VMEM((lc + CONV_HALO, SSD_INNER), F32),
                        pltpu.VMEM((lc + CONV_HALO, BC_W), F32),
                        pltpu.VMEM((SSD_GROUPS, SSD_STATE, GROUP_W), F32),
                        pltpu.VMEM((lc, SSD_INNER), F32)],
        compiler_params=_cparams(("arbitrary", "arbitrary")),
        name="ssd_branch",
    )(proj, proj, proj, dt, cwx, cbx, cwb, cbb, dtb, alog, dful, nw, expand, tri)


def _merge_kernel(x_ref, ga_ref, gb_ref, gc_ref, oa_ref, ob_ref, oc_ref, mod_ref,
                  wb_ref, wo_ref, nw_ref, x_out, h_out):
    m = mod_ref[0]
    ya = jnp.dot(oa_ref[...], wb_ref[0:MLA_OUT, :], preferred_element_type=F32)
    yb = jnp.dot(ob_ref[...], wb_ref[MLA_OUT:MLA_OUT + POOL_DIM, :], preferred_element_type=F32)
    yc = jnp.dot(oc_ref[...], wb_ref[MLA_OUT + POOL_DIM:, :], preferred_element_type=F32)
    merged = (jax.nn.sigmoid(ga_ref[...].astype(F32)) * ya
              + jax.nn.sigmoid(gb_ref[...].astype(F32)) * yb
              + jax.nn.sigmoid(gc_ref[...].astype(F32)) * yc)
    y = jnp.dot(merged.astype(BF16), wo_ref[...], preferred_element_type=F32)
    x1 = x_ref[...] + m[2:3] * y
    x_out[...] = x1
    h_out[...] = _norm_mod(x1, nw_ref[...], m[3:4], m[4:5]).astype(BF16)


def _merge(xf, proj, o_a, o_b, o_c, mod_l, wb, wo, norm_w, seq):
    t, d = xf.shape
    tm = 512
    per_seq = seq // tm
    gate = lambda k: pl.BlockSpec((tm, d), lambda i: (i, P_GATE // d + k))
    tile = lambda n: pl.BlockSpec((tm, n), lambda i: (i, 0))
    return pl.pallas_call(
        _merge_kernel,
        grid=(t // tm,),
        in_specs=[tile(d), gate(0), gate(1), gate(2), tile(MLA_OUT), tile(POOL_DIM), tile(SSD_INNER),
                  pl.BlockSpec((1, 6, d), lambda i: (i // per_seq, 0, 0)),
                  _resident(wb.shape), _resident(wo.shape),
                  pl.BlockSpec((1, d), lambda i: (0, 0))],
        out_specs=[tile(d), tile(d)],
        out_shape=[jax.ShapeDtypeStruct((t, d), F32), jax.ShapeDtypeStruct((t, d), BF16)],
        compiler_params=_cparams(("arbitrary",)),
        name="merge_out",
    )(xf, proj, proj, proj, o_a, o_b, o_c, mod_l, wb, wo, norm_w.reshape(1, d))


def _ffn_kernel(with_next, h_ref, x_ref, mod_ref, *rest):
    if with_next:
        modn_ref, nwn_ref, upw_ref, cw_ref, cb_ref, dw_ref, x_out, h_out, ext_ref, carry_ref, acc_ref = rest
    else:
        upw_ref, cw_ref, cb_ref, dw_ref, x_out, ext_ref, carry_ref, acc_ref = rest
    ts = h_ref.shape[0]
    j = pl.program_id(1)

    @pl.when(j == 0)
    def _():
        carry_ref[...] = jnp.zeros(carry_ref.shape, F32)

    h = h_ref[...]
    acc_ref[...] = jnp.zeros(acc_ref.shape, F32)

    def chunk(c, _):
        u = jnp.dot(h, upw_ref[c], preferred_element_type=F32)
        ext_ref[0:CONV_HALO, :] = carry_ref[c]
        ext_ref[CONV_HALO:CONV_HALO + ts, :] = u
        carry_ref[c] = u[ts - CONV_HALO:, :]
        cw = cw_ref[c]
        conv = cw[FFN_CONV - 1:FFN_CONV, :] * u + cb_ref[c]
        for k in range(FFN_CONV - 1):
            off = CONV_HALO - (FFN_CONV - 1) + k
            conv = conv + cw[k:k + 1, :] * ext_ref[off:off + ts, :]
        act = _silu(conv[:, :FFN_CHUNK]) * conv[:, FFN_CHUNK:]
        acc_ref[...] += jnp.dot(act.astype(BF16), dw_ref[c], preferred_element_type=F32)
        return 0

    lax.fori_loop(0, FFN_NCHUNK, chunk, 0)
    m = mod_ref[0]
    x2 = x_ref[...] + m[5:6] * acc_ref[...]
    x_out[...] = x2
    if with_next:
        mn = modn_ref[0]
        h_out[...] = _norm_mod(x2, nwn_ref[...], mn[0:1], mn[1:2]).astype(BF16)


def _ffn(h2, x1, mod_l, upw, cw, cb, dw, seq, mod_next=None, norm_next=None):
    t, d = x1.shape
    ts = 512
    per_seq = seq // ts
    batch = t // seq
    with_next = mod_next is not None
    tile = pl.BlockSpec((ts, d), lambda b, j: (b * per_seq + j, 0))
    modspec = pl.BlockSpec((1, 6, d), lambda b, j: (b, 0, 0))
    in_specs = [tile, tile, modspec]
    args = [h2, x1, mod_l]
    out_specs = [tile]
    out_shape = [jax.ShapeDtypeStruct((t, d), F32)]
    if with_next:
        in_specs += [modspec, pl.BlockSpec((1, d), lambda b, j: (0, 0))]
        args += [mod_next, norm_next.reshape(1, d)]
        out_specs.append(tile)
        out_shape.append(jax.ShapeDtypeStruct((t, d), BF16))
    in_specs += [_resident(upw.shape), _resident(cw.shape), _resident(cb.shape), _resident(dw.shape)]
    args += [upw, cw, cb, dw]
    outs = pl.pallas_call(
        functools.partial(_ffn_kernel, with_next),
        grid=(batch, per_seq),
        in_specs=in_specs,
        out_specs=out_specs,
        out_shape=out_shape,
        scratch_shapes=[pltpu.VMEM((ts + CONV_HALO, 2 * FFN_CHUNK), F32),
                        pltpu.VMEM((FFN_NCHUNK, CONV_HALO, 2 * FFN_CHUNK), F32),
                        pltpu.VMEM((ts, d), F32)],
        compiler_params=_cparams(("arbitrary", "arbitrary")),
        name="conv_mlp",
    )(*args)
    return (outs[0], outs[1]) if with_next else (outs[0], None)


def _head_blocks(w, per_head, keep):
    kdim = w.shape[0]
    w = w.reshape(kdim, MLA_HEADS, per_head)[:, :, :keep]
    w = jnp.pad(w, ((0, 0), (0, 0), (0, HEAD_PAD - keep)))
    return w.reshape(kdim, MLA_HEADS * HEAD_PAD)


def _group_mean_matrix(sizes):
    blk = jnp.zeros((HEAD_PAD, HEAD_PAD), F32)
    start = 0
    for n in sizes:
        blk = blk.at[start:start + n, start:start + n].set(1.0 / n)
        start += n
    z = jnp.zeros_like(blk)
    return jnp.concatenate([jnp.concatenate([blk, z], 1), jnp.concatenate([z, blk], 1)], 0).astype(BF16)


def _pad_lanes(v, n):
    return jnp.pad(v, (0, n - v.shape[0])).reshape(1, n)


def kernel(x, c, positions, ada_w, ada_b, norm1_w, w_in, q_a_norm, w_q_b, kv_a_norm, w_kv_b, q_norm, k_norm, pool_w, pool_scale, ssd_conv_w, ssd_conv_b, ssd_dt_bias, ssd_a_log, ssd_d, ssd_norm_w, w_branch, w_out, norm2_w, ffn_up, ffn_conv_w, ffn_conv_b, ffn_down):
    batch, seq, d = x.shape
    depth = ada_w.shape[0]
    t = batch * seq
    xf = x.reshape(t, d)

    mods = _mods(c, ada_w, ada_b).reshape(depth, batch, 6, d)
    cos_t, sin_t = _rope_tables(positions)

    gq = _group_mean_matrix((QK_NOPE, QK_ROPE))
    gk = _group_mean_matrix((QK_NOPE,))
    head_of_col = jnp.arange(SSD_INNER) // SSD_HEAD_DIM
    expand = (jnp.arange(DT_PAD)[:, None] == head_of_col[None, :]).astype(BF16)
    tri = (jnp.arange(SSD_CHUNK)[None, :] <= jnp.arange(SSD_CHUNK)[:, None]).astype(BF16)
    o = 0
    offs = {}
    for name, size in (("q_lat", Q_LORA), ("c_kv", KV_LORA), ("k_rope", QK_ROPE), ("pool", POOL_DIM),
                       ("z", SSD_INNER), ("xs", SSD_INNER), ("bc", 2 * SSD_GROUPS * SSD_STATE),
                       ("dt", SSD_HEADS), ("gate", N_BRANCH * D_MODEL)):
        offs[name] = (o, o + size)
        o += size

    h = _prenorm(xf, mods[0], norm1_w[0], seq)
    for l in range(depth):
        w = w_in[l]
        col = lambda name: w[:, offs[name][0]:offs[name][1]]
        w_main = jnp.concatenate(
            [col("z"), col("xs"), col("gate"), col("pool"), col("bc"), col("q_lat"), col("c_kv"),
             jnp.zeros((d, QK_NOPE), F32), col("k_rope"), jnp.zeros((d, HEAD_PAD - QK_HEAD), F32)],
            axis=1).astype(BF16)
        w_dt = jnp.pad(col("dt"), ((0, 0), (0, DT_PAD - SSD_HEADS))).astype(BF16)
        proj, dt = _inproj(h, w_main, w_dt)

        wq = _head_blocks(w_q_b[l], QK_HEAD, QK_HEAD).astype(BF16)
        wkv3 = w_kv_b[l].reshape(KV_LORA, MLA_HEADS, QK_NOPE + V_HEAD)
        wk = _head_blocks(w_kv_b[l], QK_NOPE + V_HEAD, QK_NOPE)
        wv = wkv3[:, :, QK_NOPE:].reshape(KV_LORA, MLA_OUT)
        wkv = jnp.concatenate([wk, wv], axis=1).astype(BF16)
        scale = QK_HEAD ** -0.5
        qw = jnp.tile(_pad_lanes(q_norm[l], HEAD_PAD), (1, MLA_HEADS)) * scale
        kw = jnp.tile(_pad_lanes(k_norm[l][:QK_NOPE], HEAD_PAD), (1, MLA_HEADS))
        krw = jnp.concatenate([jnp.zeros((QK_NOPE,), F32), k_norm[l][QK_NOPE:],
                               jnp.zeros((HEAD_PAD - QK_HEAD,), F32)]).reshape(1, HEAD_PAD)
        q, k, v = _mla_prep(proj, cos_t, sin_t, q_a_norm[l].reshape(1, Q_LORA),
                            kv_a_norm[l].reshape(1, KV_LORA), wq, wkv, gq, gk, qw, kw, krw)
        o_a = _attention(q, k, v, batch, seq)

        o_b = _pool(proj, pool_w[l].astype(BF16), pool_scale[l].reshape(1, POOL_DIM), batch, seq)

        cwl = ssd_conv_w[l]
        cbl = ssd_conv_b[l].reshape(1, -1)
        o_c = _ssd(proj, dt, cwl[:, :SSD_INNER], cbl[:, :SSD_INNER], cwl[:, SSD_INNER:], cbl[:, SSD_INNER:],
                   _pad_lanes(ssd_dt_bias[l], DT_PAD), _pad_lanes(ssd_a_log[l], DT_PAD),
                   jnp.repeat(ssd_d[l], SSD_HEAD_DIM).reshape(1, SSD_INNER),
                   ssd_norm_w[l].reshape(1, SSD_INNER), expand, tri, batch, seq)

        x1, h2 = _merge(xf, proj, o_a, o_b, o_c, mods[l], w_branch[l].astype(BF16),
                        w_out[l].astype(BF16), norm2_w[l], seq)

        chunked = lambda a: jnp.concatenate(
            [a[..., :FFN_DIM].reshape(a.shape[0], FFN_NCHUNK, FFN_CHUNK),
             a[..., FFN_DIM:].reshape(a.shape[0], FFN_NCHUNK, FFN_CHUNK)], axis=-1).transpose(1, 0, 2)
        upw = chunked(ffn_up[l]).astype(BF16)
        cw = chunked(ffn_conv_w[l])
        cb = chunked(ffn_conv_b[l].reshape(1, -1))
        dw = ffn_down[l].reshape(FFN_NCHUNK, FFN_CHUNK, d).astype(BF16)
        if l + 1 < depth:
            xf, h = _ffn(h2, x1, mods[l], upw, cw, cb, dw, seq, mods[l + 1], norm1_w[l + 1])
        else:
            xf, h = _ffn(h2, x1, mods[l], upw, cw, cb, dw, seq)
    return xf.reshape(batch, seq, d)
```

```python
import functools
import math

import jax
import jax.numpy as jnp
from jax import lax
from jax.experimental import pallas as pl
from jax.experimental.pallas import tpu as pltpu

F32 = jnp.float32
BF16 = jnp.bfloat16

D_MODEL = 1024
MLA_HEADS = 8
QK_NOPE = 64
QK_ROPE = 32
QK_HEAD = QK_NOPE + QK_ROPE
V_HEAD = 64
Q_LORA = 384
KV_LORA = 256
ROPE_THETA = 10000.0
MLA_OUT = MLA_HEADS * V_HEAD
POOL_WINDOWS = (2, 4, 8, 16)
POOL_GROUP = 128
POOL_DIM = len(POOL_WINDOWS) * POOL_GROUP
SSD_HEADS = 16
SSD_HEAD_DIM = 64
SSD_INNER = SSD_HEADS * SSD_HEAD_DIM
SSD_GROUPS = 2
SSD_STATE = 128
SSD_CONV = 4
SSD_CHUNK = 128
FFN_DIM = 2816
FFN_CONV = 3
N_BRANCH = 3
EPS = 1e-6

LANES = 128
SUBLANES = 8
HEAD_PAD = 128
NEG_BIG = -1e30
CONV_HALO = SUBLANES

P_Z = 0
P_XS = P_Z + SSD_INNER
P_GATE = P_XS + SSD_INNER
P_POOL = P_GATE + N_BRANCH * D_MODEL
P_BC = P_POOL + POOL_DIM
P_LAT = P_BC + 2 * SSD_GROUPS * SSD_STATE
LAT_W = Q_LORA + KV_LORA + HEAD_PAD
PROJ_W = P_LAT + LAT_W
DT_PAD = 128

FFN_CHUNK = 256
FFN_NCHUNK = FFN_DIM // FFN_CHUNK

VMEM_LIMIT = 56 * 1024 * 1024


def _cparams(sem):
    return pltpu.CompilerParams(dimension_semantics=sem, vmem_limit_bytes=VMEM_LIMIT)


def _resident(shape):
    nd = len(shape)
    return pl.BlockSpec(shape, lambda *_: (0,) * nd, pipeline_mode=pl.Buffered(1))


def _silu(v):
    return v * jax.nn.sigmoid(v)


def _norm_mod(x, w, shift, scale):
    y = x * lax.rsqrt(jnp.mean(x * x, axis=-1, keepdims=True) + EPS)
    return (y * w) * (1.0 + scale) + shift


def _shift_rows(u, prev, k):
    full = pltpu.roll(u, k, 0)
    row = lax.broadcasted_iota(jnp.int32, prev.shape, 0)
    first = jnp.where(row < k, pltpu.roll(prev, k, 0), full[:SUBLANES])
    return jnp.concatenate([first, full[SUBLANES:]], axis=0)


def _causal_conv(u, prev, w, b, taps):
    out = w[taps - 1:taps, :] * u + b
    for k in range(taps - 1):
        out = out + w[k:k + 1, :] * _shift_rows(u, prev, taps - 1 - k)
    return out


def _split_dot(x, m):
    hi = x.astype(BF16)
    lo = (x - hi.astype(F32)).astype(BF16)
    return (jnp.dot(hi, m, preferred_element_type=F32)
            + jnp.dot(lo, m, preferred_element_type=F32))


def _mods_kernel(c_ref, w_ref, b_ref, o_ref):
    ca = _silu(c_ref[...]).astype(BF16)
    w = w_ref[0].astype(BF16)
    o_ref[0] = jnp.dot(ca, w, preferred_element_type=F32) + b_ref[0]


def _mods(c, ada_w, ada_b):
    depth, d, n = ada_w.shape
    b = c.shape[0]
    tn = 1536
    return pl.pallas_call(
        _mods_kernel,
        grid=(depth, n // tn),
        in_specs=[pl.BlockSpec((b, d), lambda l, j: (0, 0)),
                  pl.BlockSpec((1, d, tn), lambda l, j: (l, 0, j)),
                  pl.BlockSpec((1, 1, tn), lambda l, j: (l, 0, j))],
        out_specs=pl.BlockSpec((1, b, tn), lambda l, j: (l, 0, j)),
        out_shape=jax.ShapeDtypeStruct((depth, b, n), F32),
        compiler_params=_cparams(("arbitrary", "arbitrary")),
        name="adaln_mods",
    )(c, ada_w, ada_b.reshape(depth, 1, n))


def _rope_kernel(pos_ref, freq_ref, sign_ref, c_ref, s_ref):
    ang = pos_ref[...] * freq_ref[...]
    c_ref[...] = jnp.cos(ang)
    s_ref[...] = jnp.sin(ang) * sign_ref[...]


def _rope_tables(positions):
    t = positions.size
    tm = 2048
    half = QK_ROPE // 2
    inv_freq = ROPE_THETA ** (-jnp.arange(0, QK_ROPE, 2, dtype=F32) / QK_ROPE)
    zeros = jnp.zeros((QK_NOPE,), F32)
    tail = jnp.zeros((HEAD_PAD - QK_HEAD,), F32)
    freq = jnp.concatenate([zeros, inv_freq, inv_freq, tail]).reshape(1, HEAD_PAD)
    sign = jnp.concatenate([zeros, -jnp.ones((half,), F32), jnp.ones((half,), F32), tail]).reshape(1, HEAD_PAD)
    pos = positions.astype(F32).reshape(t, 1)
    row = pl.BlockSpec((1, HEAD_PAD), lambda i: (0, 0))
    tab = pl.BlockSpec((tm, HEAD_PAD), lambda i: (i, 0))
    return pl.pallas_call(
        _rope_kernel,
        grid=(t // tm,),
        in_specs=[pl.BlockSpec((tm, 1), lambda i: (i, 0)), row, row],
        out_specs=[tab, tab],
        out_shape=[jax.ShapeDtypeStruct((t, HEAD_PAD), F32)] * 2,
        compiler_params=_cparams(("arbitrary",)),
        name="rope_tables",
    )(pos, freq, sign)


def _prenorm_kernel(x_ref, mod_ref, w_ref, h_ref):
    m = mod_ref[0]
    h_ref[...] = _norm_mod(x_ref[...], w_ref[...], m[0:1], m[1:2]).astype(BF16)


def _prenorm(xf, mod_l, norm_w, seq):
    t, d = xf.shape
    tm = 1024
    per_seq = seq // tm
    return pl.pallas_call(
        _prenorm_kernel,
        grid=(t // tm,),
        in_specs=[pl.BlockSpec((tm, d), lambda i: (i, 0)),
                  pl.BlockSpec((1, 6, d), lambda i: (i // per_seq, 0, 0)),
                  pl.BlockSpec((1, d), lambda i: (0, 0))],
        out_specs=pl.BlockSpec((tm, d), lambda i: (i, 0)),
        out_shape=jax.ShapeDtypeStruct((t, d), BF16),
        compiler_params=_cparams(("arbitrary",)),
        name="prenorm",
    )(xf, mod_l, norm_w.reshape(1, d))


INPROJ_COLS = 512
BC_W = 2 * SSD_GROUPS * SSD_STATE


def _inproj_kernel(h_ref, w_ref, wdt_ref, cwx_ref, cbx_ref, cwb_ref, cbb_ref,
                   proj_ref, dt_ref, carx_ref, carb_ref, stage_ref):
    tm = h_ref.shape[0]
    wc = INPROJ_COLS

    @pl.when(pl.program_id(1) == 0)
    def _():
        carx_ref[...] = jnp.zeros(carx_ref.shape, F32)
        carb_ref[...] = jnp.zeros(carb_ref.shape, F32)

    h = h_ref[...]

    def mm(c0, width):
        return jnp.dot(h, w_ref[:, c0:c0 + width], preferred_element_type=F32)

    def conv_silu(u, rel, car_ref, cw_ref, cb_ref):
        cs = slice(rel, rel + wc)
        prev = car_ref[:, cs]
        car_ref[:, cs] = u[tm - CONV_HALO:, :]
        return _silu(_causal_conv(u, prev, cw_ref[:, cs], cb_ref[:, cs], SSD_CONV))

    heavy = [(P_XS + r, functools.partial(conv_silu, rel=r, car_ref=carx_ref, cw_ref=cwx_ref, cb_ref=cbx_ref))
             for r in range(0, SSD_INNER, wc)]
    heavy += [(P_BC + r, functools.partial(conv_silu, rel=r, car_ref=carb_ref, cw_ref=cwb_ref, cb_ref=cbb_ref))
              for r in range(0, BC_W, wc)]
    heavy += [(P_Z + r, _silu) for r in range(0, SSD_INNER, wc)]
    light = list(range(P_GATE, P_BC, wc))
    slot = pl.program_id(1) % 2
    for k, (c0, _) in enumerate(heavy):
        stage_ref[slot, :, k * wc:(k + 1) * wc] = mm(c0, wc)
    for k in range(max(len(heavy), len(light))):
        if k < len(light):
            proj_ref[:, light[k]:light[k] + wc] = mm(light[k], wc).astype(BF16)
        if k < len(heavy):
            c0, fn = heavy[k]
            proj_ref[:, c0:c0 + wc] = fn(stage_ref[slot, :, k * wc:(k + 1) * wc]).astype(BF16)
    proj_ref[:, P_LAT:] = mm(P_LAT, LAT_W).astype(BF16)
    dt_ref[...] = jnp.dot(h, wdt_ref[...], preferred_element_type=F32)


def _inproj(h, w_main, w_dt, cwx, cbx, cwb, cbb, batch, seq):
    t, d = h.shape
    tm = 512
    per_seq = seq // tm
    rows = lambda n: pl.BlockSpec((tm, n), lambda b, j: (b * per_seq + j, 0))
    whole = lambda a: pl.BlockSpec(a.shape, lambda b, j: (0, 0))
    return pl.pallas_call(
        _inproj_kernel,
        grid=(batch, per_seq),
        in_specs=[rows(d), _resident((d, PROJ_W)), _resident((d, DT_PAD)),
                  whole(cwx), whole(cbx), whole(cwb), whole(cbb)],
        out_specs=[rows(PROJ_W), rows(DT_PAD)],
        out_shape=[jax.ShapeDtypeStruct((t, PROJ_W), BF16),
                   jax.ShapeDtypeStruct((t, DT_PAD), F32)],
        scratch_shapes=[pltpu.VMEM((CONV_HALO, SSD_INNER), F32),
                        pltpu.VMEM((CONV_HALO, BC_W), F32),
                        pltpu.VMEM((2, tm, 2 * SSD_INNER + BC_W), F32)],
        compiler_params=_cparams(("arbitrary", "arbitrary")),
        name="in_projection",
    )(h, w_main, w_dt, cwx, cbx, cwb, cbb)


def _rope_apply(y, cos, sin_signed, lane):
    half = QK_ROPE // 2
    partner = jnp.where(lane < QK_NOPE + half,
                        pltpu.roll(y, HEAD_PAD - half, 1),
                        pltpu.roll(y, half, 1))
    return y * cos + partner * sin_signed


def _mla_prep_kernel(lat_ref, cos_ref, sin_ref, qan_ref, kvan_ref, wq_ref, wkv_ref,
                     gq_ref, gk_ref, qw_ref, kw_ref, krw_ref, q_ref, k_ref, v_ref):
    lat = lat_ref[...].astype(F32)
    q_lat = lat[:, :Q_LORA]
    c_kv = lat[:, Q_LORA:Q_LORA + KV_LORA]
    k_rope = lat[:, Q_LORA + KV_LORA:]
    cos = cos_ref[...]
    sin = sin_ref[...]
    lane = lax.broadcasted_iota(jnp.int32, cos.shape, 1)

    qn = q_lat * lax.rsqrt(jnp.mean(q_lat * q_lat, axis=-1, keepdims=True) + EPS) * qan_ref[...]
    cn = c_kv * lax.rsqrt(jnp.mean(c_kv * c_kv, axis=-1, keepdims=True) + EPS) * kvan_ref[...]
    q = jnp.dot(qn.astype(BF16), wq_ref[...], preferred_element_type=F32)
    kv = jnp.dot(cn.astype(BF16), wkv_ref[...], preferred_element_type=F32)
    kw = MLA_HEADS * HEAD_PAD
    vlane = lax.broadcasted_iota(jnp.int32, (1, kw), 1)
    v_ref[...] = jnp.where(vlane % HEAD_PAD < V_HEAD, kv[:, kw:], 1.0).astype(BF16)

    kr_ms = jnp.sum(k_rope * k_rope, axis=-1, keepdims=True) * (1.0 / QK_ROPE)
    kr = _rope_apply(k_rope * lax.rsqrt(kr_ms + EPS) * krw_ref[...], cos, sin, lane)

    pair = 2 * HEAD_PAD
    for c0 in range(0, kw, pair):
        xq = q[:, c0:c0 + pair]
        ms = jnp.dot((xq * xq).astype(BF16), gq_ref[...], preferred_element_type=F32)
        yq = xq * lax.rsqrt(ms + EPS) * qw_ref[:, c0:c0 + pair]
        xk = kv[:, c0:c0 + pair]
        ms = jnp.dot((xk * xk).astype(BF16), gk_ref[...], preferred_element_type=F32)
        yk = xk * lax.rsqrt(ms + EPS) * kw_ref[:, c0:c0 + pair]
        for h0 in range(0, pair, HEAD_PAD):
            q_ref[:, c0 + h0:c0 + h0 + HEAD_PAD] = _rope_apply(
                yq[:, h0:h0 + HEAD_PAD], cos, sin, lane).astype(BF16)
            k_ref[:, c0 + h0:c0 + h0 + HEAD_PAD] = (yk[:, h0:h0 + HEAD_PAD] + kr).astype(BF16)


def _mla_prep(proj, cos_t, sin_t, qan, kvan, wq, wkv, gq, gk, qw, kw, krw):
    t = proj.shape[0]
    tm = 512
    hw = MLA_HEADS * HEAD_PAD
    row = lambda n: pl.BlockSpec((1, n), lambda i: (0, 0))
    return pl.pallas_call(
        _mla_prep_kernel,
        grid=(t // tm,),
        in_specs=[pl.BlockSpec((tm, LAT_W), lambda i: (i, P_LAT // LAT_W)),
                  pl.BlockSpec((tm, HEAD_PAD), lambda i: (i, 0)),
                  pl.BlockSpec((tm, HEAD_PAD), lambda i: (i, 0)),
                  row(Q_LORA), row(KV_LORA),
                  _resident(wq.shape), _resident(wkv.shape),
                  _resident(gq.shape), _resident(gk.shape),
                  row(hw), row(hw), row(HEAD_PAD)],
        out_specs=[pl.BlockSpec((tm, hw), lambda i: (i, 0)),
                   pl.BlockSpec((tm, hw), lambda i: (i, 0)),
                   pl.BlockSpec((tm, hw), lambda i: (i, 0))],
        out_shape=[jax.ShapeDtypeStruct((t, hw), BF16)] * 3,
        compiler_params=_cparams(("arbitrary",)),
        name="mla_prep",
    )(proj, cos_t, sin_t, qan, kvan, wq, wkv, gq, gk, qw, kw, krw)


ATT_TILE = 512


def _attn_kernel(q_ref, k_ref, v_ref, o_ref):
    seq = q_ref.shape[0]
    tq = ATT_TILE
    lane = lax.broadcasted_iota(jnp.int32, (tq, HEAD_PAD), 1)
    rows = lax.broadcasted_iota(jnp.int32, (tq, tq), 0)
    cols = lax.broadcasted_iota(jnp.int32, (tq, tq), 1)
    causal = cols <= rows
    nt = (((1,), (1,)), ((), ()))
    heads = (slice(0, HEAD_PAD), slice(HEAD_PAD, 2 * HEAD_PAD))

    for qi in range(seq // tq):
        q0 = qi * tq
        qs = [q_ref[q0:q0 + tq, hs] for hs in heads]

        def step(k0, carry, masked):
            new = []
            for hs, q, (m, acc) in zip(heads, qs, carry):
                s = lax.dot_general(q, k_ref[pl.ds(k0, tq), hs], nt, preferred_element_type=F32)
                if masked:
                    s = jnp.where(causal, s, NEG_BIG)
                m_new = jnp.maximum(m, jnp.max(s, axis=-1, keepdims=True))
                p = jnp.exp(s - m_new).astype(BF16)
                acc = jnp.exp(m - m_new) * acc + jnp.dot(p, v_ref[pl.ds(k0, tq), hs],
                                                         preferred_element_type=F32)
                new.append((m_new, acc))
            return tuple(new)

        init = (jnp.full((tq, 1), NEG_BIG, F32), jnp.zeros((tq, HEAD_PAD), F32))
        carry = (init, init)
        for j in range(qi):
            carry = step(j * tq, carry, False)
        carry = step(q0, carry, True)
        outs = []
        for _, acc in carry:
            denom = jnp.where(lane < V_HEAD, pltpu.roll(acc, V_HEAD, 1), 1.0)
            outs.append(acc / denom)
        o_ref[q0:q0 + tq, :] = jnp.where(lane < V_HEAD, outs[0],
                                         pltpu.roll(outs[1], V_HEAD, 1)).astype(BF16)


def _attention(q, k, v, batch, seq):
    t = q.shape[0]
    pairs = MLA_HEADS // 2
    return pl.pallas_call(
        _attn_kernel,
        grid=(batch, pairs),
        in_specs=[pl.BlockSpec((seq, 2 * HEAD_PAD), lambda b, p: (b, p)),
                  pl.BlockSpec((seq, 2 * HEAD_PAD), lambda b, p: (b, p)),
                  pl.BlockSpec((seq, 2 * HEAD_PAD), lambda b, p: (b, p))],
        out_specs=pl.BlockSpec((seq, 2 * V_HEAD), lambda b, p: (b, p)),
        out_shape=jax.ShapeDtypeStruct((t, MLA_OUT), BF16),
        compiler_params=_cparams(("arbitrary", "arbitrary")),
        name="mla_attention",
    )(q, k, v)


POOL_HALO = 16


def _pool_kernel(u_ref, pw_ref, ps_ref, o_ref, ext_ref):
    ts = u_ref.shape[0]
    j = pl.program_id(1)

    @pl.when(j == 0)
    def _():
        ext_ref[0:POOL_HALO, :] = jnp.zeros((POOL_HALO, POOL_DIM), F32)

    u = u_ref[...].astype(F32)
    ext_ref[POOL_HALO:POOL_HALO + ts, :] = u
    t_pos = j * ts + lax.broadcasted_iota(jnp.int32, (ts, 1), 0)
    for g, w in enumerate(POOL_WINDOWS):
        gs = slice(g * POOL_GROUP, (g + 1) * POOL_GROUP)
        ug = u[:, gs]
        acc = ug
        for dlt in range(1, w):
            acc = acc + ext_ref[POOL_HALO - dlt:POOL_HALO - dlt + ts, gs]
        count = jnp.minimum(t_pos + 1, w).astype(F32)
        pooled = acc / count - ug
        mixed = jnp.dot(pooled.astype(BF16), pw_ref[g], preferred_element_type=F32)
        o_ref[:, gs] = (mixed * ps_ref[:, gs]).astype(BF16)
    ext_ref[0:POOL_HALO, :] = ext_ref[ts:ts + POOL_HALO, :]


def _pool(proj, pool_w, pool_scale, batch, seq):
    t = proj.shape[0]
    ts = 512
    per_seq = seq // ts
    return pl.pallas_call(
        _pool_kernel,
        grid=(batch, per_seq),
        in_specs=[pl.BlockSpec((ts, POOL_DIM), lambda b, j: (b * per_seq + j, P_POOL // POOL_DIM)),
                  _resident(pool_w.shape),
                  pl.BlockSpec((1, POOL_DIM), lambda b, j: (0, 0))],
        out_specs=pl.BlockSpec((ts, POOL_DIM), lambda b, j: (b * per_seq + j, 0)),
        out_shape=jax.ShapeDtypeStruct((t, POOL_DIM), BF16),
        scratch_shapes=[pltpu.VMEM((ts + POOL_HALO, POOL_DIM), F32)],
        compiler_params=_cparams(("arbitrary", "arbitrary")),
        name="pool_branch",
    )(proj, pool_w, pool_scale)


GROUP_W = SSD_INNER // SSD_GROUPS
SSD_STEP_CHUNKS = 4


def _ssd_kernel(zs_ref, xs_ref, bc_ref, dt_ref, dtb_ref, alog_ref, dful_ref, nw_ref,
                exp_ref, tri_ref, o_ref, state_ref, gated_ref):
    lc = SSD_CHUNK

    @pl.when(pl.program_id(1) == 0)
    def _():
        state_ref[...] = jnp.zeros(state_ref.shape, F32)

    rows = lax.broadcasted_iota(jnp.int32, (lc, lc), 0)
    cols = lax.broadcasted_iota(jnp.int32, (lc, lc), 1)
    causal = cols <= rows
    lane = lax.broadcasted_iota(jnp.int32, (lc, 2 * SSD_HEAD_DIM), 1)
    nt = (((1,), (1,)), ((), ()))
    tn = (((0,), (0,)), ((), ()))
    heads_per_group = SSD_HEADS // SSD_GROUPS
    neg_a = -jnp.exp(alog_ref[...])

    for r0 in range(0, xs_ref.shape[0], lc):
        rs = slice(r0, r0 + lc)
        xs = xs_ref[rs, :].astype(F32)
        bm = bc_ref[rs, :SSD_GROUPS * SSD_STATE]
        cm = bc_ref[rs, SSD_GROUPS * SSD_STATE:]

        v = dt_ref[rs, :] + dtb_ref[...]
        dt = jnp.maximum(v, 0.0) + jnp.log1p(jnp.exp(-jnp.abs(v)))
        a_cs = _split_dot_left(tri_ref[...], dt * neg_a)
        a_last = a_cs[lc - 1:lc, :]
        stacked = jnp.concatenate([dt, jnp.exp(a_cs), jnp.exp(a_last - a_cs)], axis=0)
        full = _split_dot(stacked, exp_ref[...])
        dt_full = full[0:lc]
        ea_full = full[lc:2 * lc]
        ds_full = full[2 * lc:3 * lc]
        chunk_decay = ea_full[lc - 1:lc, :]
        a_cs_t = a_cs.T

        xdt = xs * dt_full
        xdt_b = xdt.astype(BF16)
        xds_b = (xdt * ds_full).astype(BF16)

        for g in range(SSD_GROUPS):
            gs = slice(g * GROUP_W, (g + 1) * GROUP_W)
            bg = bm[:, g * SSD_STATE:(g + 1) * SSD_STATE]
            cg = cm[:, g * SSD_STATE:(g + 1) * SSD_STATE]
            cb = lax.dot_general(cg, bg, nt, preferred_element_type=F32)
            state = state_ref[g]
            y_off = jnp.dot(cg, state.astype(BF16), preferred_element_type=F32) * ea_full[:, gs]
            s_new = lax.dot_general(bg, xds_b[:, gs], tn, preferred_element_type=F32)
            state_ref[g] = state * chunk_decay[:, gs] + s_new
            for pp in range(heads_per_group // 2):
                h0 = g * heads_per_group + 2 * pp
                ps = slice(h0 * SSD_HEAD_DIM, (h0 + 2) * SSD_HEAD_DIM)
                ys = []
                for hd in (h0, h0 + 1):
                    seg = a_cs[:, hd:hd + 1] - a_cs_t[hd:hd + 1, :]
                    decay = jnp.exp(jnp.where(causal, seg, NEG_BIG))
                    ys.append(jnp.dot((cb * decay).astype(BF16), xdt_b[:, ps],
                                      preferred_element_type=F32))
                y_pair = jnp.where(lane < SSD_HEAD_DIM, ys[0], ys[1])
                ls = slice(2 * pp * SSD_HEAD_DIM, (2 * pp + 2) * SSD_HEAD_DIM)
                y = y_pair + y_off[:, ls] + xs[:, ps] * dful_ref[:, ps]
                gated_ref[rs, ps] = y * zs_ref[rs, ps].astype(F32)

        for g in range(SSD_GROUPS):
            gs = slice(g * GROUP_W, (g + 1) * GROUP_W)
            gt = gated_ref[rs, gs]
            ms = jnp.mean(gt * gt, axis=-1, keepdims=True)
            o_ref[rs, gs] = (gt * lax.rsqrt(ms + EPS) * nw_ref[:, gs]).astype(o_ref.dtype)


def _split_dot_left(m, x):
    hi = x.astype(BF16)
    lo = (x - hi.astype(F32)).astype(BF16)
    return (jnp.dot(m, hi, preferred_element_type=F32)
            + jnp.dot(m, lo, preferred_element_type=F32))


def _ssd(proj, dt, dtb, alog, dful, nw, expand, tri, batch, seq):
    t = proj.shape[0]
    lc = SSD_STEP_CHUNKS * SSD_CHUNK
    nc = seq // lc
    row = lambda n: pl.BlockSpec((1, n), lambda b, j: (0, 0))
    return pl.pallas_call(
        _ssd_kernel,
        grid=(batch, nc),
        in_specs=[pl.BlockSpec((lc, SSD_INNER), lambda b, j: (b * nc + j, P_Z // SSD_INNER)),
                  pl.BlockSpec((lc, SSD_INNER), lambda b, j: (b * nc + j, P_XS // SSD_INNER)),
                  pl.BlockSpec((lc, BC_W), lambda b, j: (b * nc + j, P_BC // BC_W)),
                  pl.BlockSpec((lc, DT_PAD), lambda b, j: (b * nc + j, 0)),
                  row(DT_PAD), row(DT_PAD), row(SSD_INNER), row(SSD_INNER),
                  _resident(expand.shape), _resident(tri.shape)],
        out_specs=pl.BlockSpec((lc, SSD_INNER), lambda b, j: (b * nc + j, 0)),
        out_shape=jax.ShapeDtypeStruct((t, SSD_INNER), BF16),
        scratch_shapes=[pltpu.VMEM((SSD_GROUPS, SSD_STATE, GROUP_W), F32),
                        pltpu.VMEM((lc, SSD_INNER), F32)],
        compiler_params=_cparams(("arbitrary", "arbitrary")),
        name="ssd_branch",
    )(proj, proj, proj, dt, dtb, alog, dful, nw, expand, tri)


def _merge_kernel(x_ref, ga_ref, gb_ref, gc_ref, oa_ref, ob_ref, oc_ref, mod_ref,
                  wb_ref, wo_ref, nw_ref, x_out, h_out):
    m = mod_ref[0]
    ya = jnp.dot(oa_ref[...], wb_ref[0:MLA_OUT, :], preferred_element_type=F32)
    yb = jnp.dot(ob_ref[...], wb_ref[MLA_OUT:MLA_OUT + POOL_DIM, :], preferred_element_type=F32)
    yc = jnp.dot(oc_ref[...], wb_ref[MLA_OUT + POOL_DIM:, :], preferred_element_type=F32)
    merged = (jax.nn.sigmoid(ga_ref[...].astype(F32)) * ya
              + jax.nn.sigmoid(gb_ref[...].astype(F32)) * yb
              + jax.nn.sigmoid(gc_ref[...].astype(F32)) * yc)
    y = jnp.dot(merged.astype(BF16), wo_ref[...], preferred_element_type=F32)
    x1 = x_ref[...] + m[2:3] * y
    x_out[...] = x1
    h_out[...] = _norm_mod(x1, nw_ref[...], m[3:4], m[4:5]).astype(BF16)


def _merge(xf, proj, o_a, o_b, o_c, mod_l, wb, wo, norm_w, seq):
    t, d = xf.shape
    tm = 512
    per_seq = seq // tm
    gate = lambda k: pl.BlockSpec((tm, d), lambda i: (i, P_GATE // d + k))
    tile = lambda n: pl.BlockSpec((tm, n), lambda i: (i, 0))
    return pl.pallas_call(
        _merge_kernel,
        grid=(t // tm,),
        in_specs=[tile(d), gate(0), gate(1), gate(2), tile(MLA_OUT), tile(POOL_DIM), tile(SSD_INNER),
                  pl.BlockSpec((1, 6, d), lambda i: (i // per_seq, 0, 0)),
                  _resident(wb.shape), _resident(wo.shape),
                  pl.BlockSpec((1, d), lambda i: (0, 0))],
        out_specs=[tile(d), tile(d)],
        out_shape=[jax.ShapeDtypeStruct((t, d), F32), jax.ShapeDtypeStruct((t, d), BF16)],
        compiler_params=_cparams(("arbitrary",)),
        name="merge_out",
    )(xf, proj, proj, proj, o_a, o_b, o_c, mod_l, wb, wo, norm_w.reshape(1, d))


def _ffn_kernel(with_next, h_ref, x_ref, mod_ref, *rest):
    if with_next:
        modn_ref, nwn_ref, upw_ref, cw_ref, cb_ref, dw_ref, x_out, h_out, carry_ref, act_ref = rest
    else:
        upw_ref, cw_ref, cb_ref, dw_ref, x_out, carry_ref, act_ref = rest
    ts = h_ref.shape[0]
    j = pl.program_id(1)

    @pl.when(j == 0)
    def _():
        carry_ref[...] = jnp.zeros(carry_ref.shape, F32)

    h = h_ref[...]

    def chunk(c, _):
        u = jnp.dot(h, upw_ref[c], preferred_element_type=F32)
        prev = carry_ref[c]
        carry_ref[c] = u[ts - CONV_HALO:, :]
        conv = _causal_conv(u, prev, cw_ref[c], cb_ref[c], FFN_CONV)
        act = _silu(conv[:, :FFN_CHUNK]) * conv[:, FFN_CHUNK:]
        act_ref[c] = act.astype(BF16)
        return 0

    for c in range(FFN_NCHUNK):
        chunk(c, 0)
    y = jnp.dot(act_ref[0], dw_ref[0], preferred_element_type=F32)
    for c in range(1, FFN_NCHUNK):
        y = y + jnp.dot(act_ref[c], dw_ref[c], preferred_element_type=F32)
    m = mod_ref[0]
    x2 = x_ref[...] + m[5:6] * y
    x_out[...] = x2
    if with_next:
        mn = modn_ref[0]
        h_out[...] = _norm_mod(x2, nwn_ref[...], mn[0:1], mn[1:2]).astype(BF16)


def _ffn(h2, x1, mod_l, upw, cw, cb, dw, seq, mod_next=None, norm_next=None):
    t, d = x1.shape
    ts = 512
    per_seq = seq // ts
    batch = t // seq
    with_next = mod_next is not None
    tile = pl.BlockSpec((ts, d), lambda b, j: (b * per_seq + j, 0))
    modspec = pl.BlockSpec((1, 6, d), lambda b, j: (b, 0, 0))
    in_specs = [tile, tile, modspec]
    args = [h2, x1, mod_l]
    out_specs = [tile]
    out_shape = [jax.ShapeDtypeStruct((t, d), F32)]
    if with_next:
        in_specs += [modspec, pl.BlockSpec((1, d), lambda b, j: (0, 0))]
        args += [mod_next, norm_next.reshape(1, d)]
        out_specs.append(tile)
        out_shape.append(jax.ShapeDtypeStruct((t, d), BF16))
    in_specs += [_resident(upw.shape), _resident(cw.shape), _resident(cb.shape), _resident(dw.shape)]
    args += [upw, cw, cb, dw]
    outs = pl.pallas_call(
        functools.partial(_ffn_kernel, with_next),
        grid=(batch, per_seq),
        in_specs=in_specs,
        out_specs=out_specs,
        out_shape=out_shape,
        scratch_shapes=[pltpu.VMEM((FFN_NCHUNK, CONV_HALO, 2 * FFN_CHUNK), F32),
                        pltpu.VMEM((FFN_NCHUNK, ts, FFN_CHUNK), BF16)],
        compiler_params=_cparams(("arbitrary", "arbitrary")),
        name="conv_mlp",
    )(*args)
    return (outs[0], outs[1]) if with_next else (outs[0], None)


def _head_blocks(w, per_head, keep):
    kdim = w.shape[0]
    w = w.reshape(kdim, MLA_HEADS, per_head)[:, :, :keep]
    w = jnp.pad(w, ((0, 0), (0, 0), (0, HEAD_PAD - keep)))
    return w.reshape(kdim, MLA_HEADS * HEAD_PAD)


def _group_mean_matrix(sizes):
    blk = jnp.zeros((HEAD_PAD, HEAD_PAD), F32)
    start = 0
    for n in sizes:
        blk = blk.at[start:start + n, start:start + n].set(1.0 / n)
        start += n
    z = jnp.zeros_like(blk)
    return jnp.concatenate([jnp.concatenate([blk, z], 1), jnp.concatenate([z, blk], 1)], 0).astype(BF16)


def _pad_lanes(v, n):
    return jnp.pad(v, (0, n - v.shape[0])).reshape(1, n)


def kernel(x, c, positions, ada_w, ada_b, norm1_w, w_in, q_a_norm, w_q_b, kv_a_norm, w_kv_b, q_norm, k_norm, pool_w, pool_scale, ssd_conv_w, ssd_conv_b, ssd_dt_bias, ssd_a_log, ssd_d, ssd_norm_w, w_branch, w_out, norm2_w, ffn_up, ffn_conv_w, ffn_conv_b, ffn_down):
    batch, seq, d = x.shape
    depth = ada_w.shape[0]
    t = batch * seq
    xf = x.reshape(t, d)

    mods = _mods(c, ada_w, ada_b).reshape(depth, batch, 6, d)
    cos_t, sin_t = _rope_tables(positions)

    gq = _group_mean_matrix((QK_NOPE, QK_ROPE))
    gk = _group_mean_matrix((QK_NOPE,))
    head_of_col = jnp.arange(SSD_INNER) // SSD_HEAD_DIM
    expand = (jnp.arange(DT_PAD)[:, None] == head_of_col[None, :]).astype(BF16)
    tri = (jnp.arange(SSD_CHUNK)[None, :] <= jnp.arange(SSD_CHUNK)[:, None]).astype(BF16)
    o = 0
    offs = {}
    for name, size in (("q_lat", Q_LORA), ("c_kv", KV_LORA), ("k_rope", QK_ROPE), ("pool", POOL_DIM),
                       ("z", SSD_INNER), ("xs", SSD_INNER), ("bc", 2 * SSD_GROUPS * SSD_STATE),
                       ("dt", SSD_HEADS), ("gate", N_BRANCH * D_MODEL)):
        offs[name] = (o, o + size)
        o += size

    h = _prenorm(xf, mods[0], norm1_w[0], seq)
    for l in range(depth):
        w = w_in[l]
        col = lambda name: w[:, offs[name][0]:offs[name][1]]
        w_main = jnp.concatenate(
            [col("z"), col("xs"), col("gate"), col("pool"), col("bc"), col("q_lat"), col("c_kv"),
             jnp.zeros((d, QK_NOPE), F32), col("k_rope"), jnp.zeros((d, HEAD_PAD - QK_HEAD), F32)],
            axis=1).astype(BF16)
        w_dt = jnp.pad(col("dt"), ((0, 0), (0, DT_PAD - SSD_HEADS))).astype(BF16)
        cwl = ssd_conv_w[l]
        cbl = ssd_conv_b[l].reshape(1, -1)
        proj, dt = _inproj(h, w_main, w_dt, cwl[:, :SSD_INNER], cbl[:, :SSD_INNER],
                           cwl[:, SSD_INNER:], cbl[:, SSD_INNER:], batch, seq)

        wq = _head_blocks(w_q_b[l], QK_HEAD, QK_HEAD).astype(BF16)
        wkv3 = w_kv_b[l].reshape(KV_LORA, MLA_HEADS, QK_NOPE + V_HEAD)
        wk = _head_blocks(w_kv_b[l], QK_NOPE + V_HEAD, QK_NOPE)
        wv = jnp.pad(wkv3[:, :, QK_NOPE:], ((0, 0), (0, 0), (0, HEAD_PAD - V_HEAD))).reshape(
            KV_LORA, MLA_HEADS * HEAD_PAD)
        wkv = jnp.concatenate([wk, wv], axis=1).astype(BF16)
        scale = QK_HEAD ** -0.5
        qw = jnp.tile(_pad_lanes(q_norm[l], HEAD_PAD), (1, MLA_HEADS)) * scale
        kw = jnp.tile(_pad_lanes(k_norm[l][:QK_NOPE], HEAD_PAD), (1, MLA_HEADS))
        krw = jnp.concatenate([jnp.zeros((QK_NOPE,), F32), k_norm[l][QK_NOPE:],
                               jnp.zeros((HEAD_PAD - QK_HEAD,), F32)]).reshape(1, HEAD_PAD)
        q, k, v = _mla_prep(proj, cos_t, sin_t, q_a_norm[l].reshape(1, Q_LORA),
                            kv_a_norm[l].reshape(1, KV_LORA), wq, wkv, gq, gk, qw, kw, krw)
        o_a = _attention(q, k, v, batch, seq)

        o_b = _pool(proj, pool_w[l].astype(BF16), pool_scale[l].reshape(1, POOL_DIM), batch, seq)

        o_c = _ssd(proj, dt, _pad_lanes(ssd_dt_bias[l], DT_PAD), _pad_lanes(ssd_a_log[l], DT_PAD),
                   jnp.repeat(ssd_d[l], SSD_HEAD_DIM).reshape(1, SSD_INNER),
                   ssd_norm_w[l].reshape(1, SSD_INNER), expand, tri, batch, seq)

        x1, h2 = _merge(xf, proj, o_a, o_b, o_c, mods[l], w_branch[l].astype(BF16),
                        w_out[l].astype(BF16), norm2_w[l], seq)

        chunked = lambda a: jnp.concatenate(
            [a[..., :FFN_DIM].reshape(a.shape[0], FFN_NCHUNK, FFN_CHUNK),
             a[..., FFN_DIM:].reshape(a.shape[0], FFN_NCHUNK, FFN_CHUNK)], axis=-1).transpose(1, 0, 2)
        upw = chunked(ffn_up[l]).astype(BF16)
        cw = chunked(ffn_conv_w[l])
        cb = chunked(ffn_conv_b[l].reshape(1, -1))
        dw = ffn_down[l].reshape(FFN_NCHUNK, FFN_CHUNK, d).astype(BF16)
        if l + 1 < depth:
            xf, h = _ffn(h2, x1, mods[l], upw, cw, cb, dw, seq, mods[l + 1], norm1_w[l + 1])
        else:
            xf, h = _ffn(h2, x1, mods[l], upw, cw, cb, dw, seq)
    return xf.reshape(batch, seq, d)
```

```python
import functools

import jax
import jax.numpy as jnp
from jax import lax
from jax.experimental import pallas as pl
from jax.experimental.pallas import tpu as pltpu

F32 = jnp.float32
BF16 = jnp.bfloat16

D_MODEL = 1024
MLA_HEADS = 8
QK_NOPE = 64
QK_ROPE = 32
QK_HEAD = QK_NOPE + QK_ROPE
V_HEAD = 64
Q_LORA = 384
KV_LORA = 256
ROPE_THETA = 10000.0
MLA_OUT = MLA_HEADS * V_HEAD
POOL_WINDOWS = (2, 4, 8, 16)
POOL_GROUP = 128
POOL_DIM = len(POOL_WINDOWS) * POOL_GROUP
SSD_HEADS = 16
SSD_HEAD_DIM = 64
SSD_INNER = SSD_HEADS * SSD_HEAD_DIM
SSD_GROUPS = 2
SSD_STATE = 128
SSD_CONV = 4
SSD_CHUNK = 128
FFN_DIM = 2816
FFN_CONV = 3
N_BRANCH = 3
EPS = 1e-6

SUBLANES = 8
HEAD_PAD = 128
NEG_BIG = -1e30
CONV_HALO = SUBLANES

BC_W = 2 * SSD_GROUPS * SSD_STATE
GROUP_W = SSD_INNER // SSD_GROUPS
P_Z = 0
P_XS = P_Z + SSD_INNER
P_GATE = P_XS + SSD_INNER
P_POOL = P_GATE + N_BRANCH * D_MODEL
P_BC = P_POOL + POOL_DIM
P_LAT = P_BC + BC_W
LAT_W = Q_LORA + KV_LORA + HEAD_PAD
PROJ_W = P_LAT + LAT_W
DT_PAD = HEAD_PAD

FFN_CHUNK = 256
FFN_NCHUNK = FFN_DIM // FFN_CHUNK

VMEM_LIMIT = 56 * 1024 * 1024


def _cparams(sem):
    return pltpu.CompilerParams(dimension_semantics=sem, vmem_limit_bytes=VMEM_LIMIT)


def _const_block(a):
    nd = a.ndim
    return pl.BlockSpec(a.shape, lambda *_: (0,) * nd, pipeline_mode=pl.Buffered(1))


def _layer_block(a, layer):
    nd = a.ndim
    return pl.BlockSpec((1,) + a.shape[1:], lambda *_: (layer,) + (0,) * (nd - 1),
                        pipeline_mode=pl.Buffered(1))


def _silu(v):
    return v * jax.nn.sigmoid(v)


def _norm_mod(x, w, shift, scale):
    y = x * lax.rsqrt(jnp.mean(x * x, axis=-1, keepdims=True) + EPS)
    return (y * w) * (1.0 + scale) + shift


def _shift_rows(u, prev, k):
    full = pltpu.roll(u, k, 0)
    row = lax.broadcasted_iota(jnp.int32, prev.shape, 0)
    first = jnp.where(row < k, pltpu.roll(prev, k, 0), full[:SUBLANES])
    return jnp.concatenate([first, full[SUBLANES:]], axis=0)


def _causal_conv(u, prev, w, b, taps):
    out = w[taps - 1:taps, :] * u + b
    for k in range(taps - 1):
        out = out + w[k:k + 1, :] * _shift_rows(u, prev, taps - 1 - k)
    return out


def _split(x):
    hi = x.astype(BF16)
    return hi, (x - hi.astype(F32)).astype(BF16)


def _split_dot(x, m):
    hi, lo = _split(x)
    return jnp.dot(hi, m, preferred_element_type=F32) + jnp.dot(lo, m, preferred_element_type=F32)


def _split_dot_left(m, x):
    hi, lo = _split(x)
    return jnp.dot(m, hi, preferred_element_type=F32) + jnp.dot(m, lo, preferred_element_type=F32)


def _mods_kernel(c_ref, w_ref, b_ref, o_ref):
    ca = _silu(c_ref[...]).astype(BF16)
    w = w_ref[0].astype(BF16)
    o_ref[0] = jnp.dot(ca, w, preferred_element_type=F32) + b_ref[0]


def _mods(c, ada_w, ada_b):
    depth, d, n = ada_w.shape
    b = c.shape[0]
    tn = 1536
    return pl.pallas_call(
        _mods_kernel,
        grid=(depth, n // tn),
        in_specs=[pl.BlockSpec((b, d), lambda l, j: (0, 0)),
                  pl.BlockSpec((1, d, tn), lambda l, j: (l, 0, j)),
                  pl.BlockSpec((1, 1, tn), lambda l, j: (l, 0, j))],
        out_specs=pl.BlockSpec((1, b, tn), lambda l, j: (l, 0, j)),
        out_shape=jax.ShapeDtypeStruct((depth, b, n), F32),
        compiler_params=_cparams(("arbitrary", "arbitrary")),
        name="adaln_mods",
    )(c, ada_w, ada_b.reshape(depth, 1, n))


def _rope_kernel(pos_ref, freq_ref, sign_ref, c_ref, s_ref):
    ang = pos_ref[...] * freq_ref[...]
    c_ref[...] = jnp.cos(ang)
    s_ref[...] = jnp.sin(ang) * sign_ref[...]


def _rope_tables(positions):
    t = positions.size
    tm = 2048
    half = QK_ROPE // 2
    inv_freq = ROPE_THETA ** (-jnp.arange(0, QK_ROPE, 2, dtype=F32) / QK_ROPE)
    zeros = jnp.zeros((QK_NOPE,), F32)
    tail = jnp.zeros((HEAD_PAD - QK_HEAD,), F32)
    freq = jnp.concatenate([zeros, inv_freq, inv_freq, tail]).reshape(1, HEAD_PAD)
    sign = jnp.concatenate([zeros, -jnp.ones((half,), F32), jnp.ones((half,), F32), tail]).reshape(1, HEAD_PAD)
    pos = positions.astype(F32).reshape(t, 1)
    row = pl.BlockSpec((1, HEAD_PAD), lambda i: (0, 0))
    tab = pl.BlockSpec((tm, HEAD_PAD), lambda i: (i, 0))
    return pl.pallas_call(
        _rope_kernel,
        grid=(t // tm,),
        in_specs=[pl.BlockSpec((tm, 1), lambda i: (i, 0)), row, row],
        out_specs=[tab, tab],
        out_shape=[jax.ShapeDtypeStruct((t, HEAD_PAD), F32)] * 2,
        compiler_params=_cparams(("arbitrary",)),
        name="rope_tables",
    )(pos, freq, sign)


def _prenorm_kernel(x_ref, mod_ref, w_ref, h_ref):
    m = mod_ref[0, 0]
    h_ref[...] = _norm_mod(x_ref[...], w_ref[0], m[0:1], m[1:2]).astype(BF16)


def _prenorm(xf, mods, norm_w, layer, seq):
    t, d = xf.shape
    tm = 1024
    per_seq = seq // tm
    return pl.pallas_call(
        _prenorm_kernel,
        grid=(t // tm,),
        in_specs=[pl.BlockSpec((tm, d), lambda i: (i, 0)),
                  pl.BlockSpec((1, 1, 6, d), lambda i: (layer, i // per_seq, 0, 0)),
                  _layer_block(norm_w, layer)],
        out_specs=pl.BlockSpec((tm, d), lambda i: (i, 0)),
        out_shape=jax.ShapeDtypeStruct((t, d), BF16),
        compiler_params=_cparams(("arbitrary",)),
        name="prenorm",
    )(xf, mods, norm_w)


INPROJ_COLS = 512


def _inproj_kernel(h_ref, w_ref, cw_ref, cb_ref, proj_ref, dt_ref, car_ref):
    tm = h_ref.shape[0]
    wc = INPROJ_COLS

    @pl.when(pl.program_id(1) == 0)
    def _():
        car_ref[...] = jnp.zeros(car_ref.shape, F32)

    h = h_ref[...]

    def mm(c0, width):
        return jnp.dot(h, w_ref[0, :, c0:c0 + width], preferred_element_type=F32)

    for rel in range(0, SSD_INNER, wc):
        proj_ref[:, P_Z + rel:P_Z + rel + wc] = _silu(mm(P_Z + rel, wc)).astype(BF16)
    for rel in range(0, SSD_INNER + BC_W, wc):
        c0 = P_XS + rel if rel < SSD_INNER else P_BC + rel - SSD_INNER
        cs = slice(rel, rel + wc)
        u = mm(c0, wc)
        prev = car_ref[:, cs]
        car_ref[:, cs] = u[tm - CONV_HALO:, :]
        conv = _causal_conv(u, prev, cw_ref[0, :, cs], cb_ref[0, :, cs], SSD_CONV)
        proj_ref[:, c0:c0 + wc] = _silu(conv).astype(BF16)
    for c0 in range(P_GATE, P_BC, wc):
        proj_ref[:, c0:c0 + wc] = mm(c0, wc).astype(BF16)
    lat = mm(P_LAT, LAT_W)
    proj_ref[:, P_LAT:] = lat.astype(BF16)
    dt_ref[...] = lat[:, LAT_W - DT_PAD:]


def _inproj(h, w_main, conv_w, conv_b, layer, batch, seq):
    t, d = h.shape
    tm = 512
    per_seq = seq // tm
    rows = lambda n: pl.BlockSpec((tm, n), lambda b, j: (b * per_seq + j, 0))
    return pl.pallas_call(
        _inproj_kernel,
        grid=(batch, per_seq),
        in_specs=[rows(d), _layer_block(w_main, layer), _layer_block(conv_w, layer),
                  _layer_block(conv_b, layer)],
        out_specs=[rows(PROJ_W), rows(DT_PAD)],
        out_shape=[jax.ShapeDtypeStruct((t, PROJ_W), BF16),
                   jax.ShapeDtypeStruct((t, DT_PAD), F32)],
        scratch_shapes=[pltpu.VMEM((CONV_HALO, SSD_INNER + BC_W), F32)],
        compiler_params=_cparams(("arbitrary", "arbitrary")),
        name="in_projection",
    )(h, w_main, conv_w, conv_b)


def _rope_apply(y, cos, sin_signed, lane):
    half = QK_ROPE // 2
    partner = jnp.where(lane < QK_NOPE + half,
                        pltpu.roll(y, HEAD_PAD - half, 1),
                        pltpu.roll(y, half, 1))
    return y * cos + partner * sin_signed


def _mla_prep_kernel(lat_ref, cos_ref, sin_ref, qan_ref, kvan_ref, wq_ref, wkv_ref,
                     gq_ref, gk_ref, qw_ref, kw_ref, krw_ref, q_ref, k_ref, v_ref):
    lat = lat_ref[...].astype(F32)
    q_lat = lat[:, :Q_LORA]
    c_kv = lat[:, Q_LORA:Q_LORA + KV_LORA]
    cos = cos_ref[...]
    sin = sin_ref[...]
    lane = lax.broadcasted_iota(jnp.int32, cos.shape, 1)
    k_rope = jnp.where((lane >= QK_NOPE) & (lane < QK_HEAD), lat[:, Q_LORA + KV_LORA:], 0.0)

    qn = q_lat * lax.rsqrt(jnp.mean(q_lat * q_lat, axis=-1, keepdims=True) + EPS) * qan_ref[0]
    cn = c_kv * lax.rsqrt(jnp.mean(c_kv * c_kv, axis=-1, keepdims=True) + EPS) * kvan_ref[0]
    q = jnp.dot(qn.astype(BF16), wq_ref[0], preferred_element_type=F32)
    kv = jnp.dot(cn.astype(BF16), wkv_ref[0], preferred_element_type=F32)
    kw = MLA_HEADS * HEAD_PAD
    vlane = lax.broadcasted_iota(jnp.int32, (1, kw), 1)
    v_ref[...] = jnp.where(vlane % HEAD_PAD < V_HEAD, kv[:, kw:], 1.0).astype(BF16)

    kr_ms = jnp.sum(k_rope * k_rope, axis=-1, keepdims=True) * (1.0 / QK_ROPE)
    kr = _rope_apply(k_rope * lax.rsqrt(kr_ms + EPS) * krw_ref[0], cos, sin, lane)

    qw = qw_ref[0]
    kwt = kw_ref[0]
    pair = 2 * HEAD_PAD
    for c0 in range(0, kw, pair):
        xq = q[:, c0:c0 + pair]
        ms = jnp.dot((xq * xq).astype(BF16), gq_ref[...], preferred_element_type=F32)
        yq = xq * lax.rsqrt(ms + EPS) * qw[:, c0:c0 + pair]
        xk = kv[:, c0:c0 + pair]
        ms = jnp.dot((xk * xk).astype(BF16), gk_ref[...], preferred_element_type=F32)
        yk = xk * lax.rsqrt(ms + EPS) * kwt[:, c0:c0 + pair]
        for h0 in range(0, pair, HEAD_PAD):
            q_ref[:, c0 + h0:c0 + h0 + HEAD_PAD] = _rope_apply(
                yq[:, h0:h0 + HEAD_PAD], cos, sin, lane).astype(BF16)
            k_ref[:, c0 + h0:c0 + h0 + HEAD_PAD] = (yk[:, h0:h0 + HEAD_PAD] + kr).astype(BF16)


def _mla_prep(proj, cos_t, sin_t, qan, kvan, wq, wkv, gq, gk, qw, kw, krw, layer):
    t = proj.shape[0]
    tm = 512
    hw = MLA_HEADS * HEAD_PAD
    lb = lambda a: _layer_block(a, layer)
    return pl.pallas_call(
        _mla_prep_kernel,
        grid=(t // tm,),
        in_specs=[pl.BlockSpec((tm, LAT_W), lambda i: (i, P_LAT // LAT_W)),
                  pl.BlockSpec((tm, HEAD_PAD), lambda i: (i, 0)),
                  pl.BlockSpec((tm, HEAD_PAD), lambda i: (i, 0)),
                  lb(qan), lb(kvan), lb(wq), lb(wkv),
                  _const_block(gq), _const_block(gk),
                  lb(qw), lb(kw), lb(krw)],
        out_specs=[pl.BlockSpec((tm, hw), lambda i: (i, 0))] * 3,
        out_shape=[jax.ShapeDtypeStruct((t, hw), BF16)] * 3,
        compiler_params=_cparams(("arbitrary",)),
        name="mla_prep",
    )(proj, cos_t, sin_t, qan, kvan, wq, wkv, gq, gk, qw, kw, krw)


ATT_TILE = 512


def _attn_kernel(q_ref, k_ref, v_ref, o_ref):
    seq = q_ref.shape[0]
    tq = ATT_TILE
    hq = tq // 2
    lane = lax.broadcasted_iota(jnp.int32, (tq, HEAD_PAD), 1)
    mask_a = (lax.broadcasted_iota(jnp.int32, (tq, hq), 1)
              <= lax.broadcasted_iota(jnp.int32, (tq, hq), 0))
    mask_b = (lax.broadcasted_iota(jnp.int32, (hq, hq), 1)
              <= lax.broadcasted_iota(jnp.int32, (hq, hq), 0))
    nt = (((1,), (1,)), ((), ()))
    heads = (slice(0, HEAD_PAD), slice(HEAD_PAD, 2 * HEAD_PAD))

    def block(state, q, k0, width, hs, mask):
        m, acc = state
        s = lax.dot_general(q, k_ref[k0:k0 + width, hs], nt, preferred_element_type=F32)
        if mask is not None:
            s = jnp.where(mask, s, NEG_BIG)
        m_new = jnp.maximum(m, jnp.max(s, axis=-1, keepdims=True))
        p = jnp.exp(s - m_new).astype(BF16)
        acc = jnp.exp(m - m_new) * acc + jnp.dot(p, v_ref[k0:k0 + width, hs],
                                                 preferred_element_type=F32)
        return m_new, acc

    for qi in range(seq // tq):
        q0 = qi * tq
        qs = [q_ref[q0:q0 + tq, hs] for hs in heads]
        init = (jnp.full((tq, 1), NEG_BIG, F32), jnp.zeros((tq, HEAD_PAD), F32))
        states = [init, init]
        for j in range(qi):
            states = [block(st, q, j * tq, tq, hs, None) for st, q, hs in zip(states, qs, heads)]
        states = [block(st, q, q0, hq, hs, mask_a) for st, q, hs in zip(states, qs, heads)]
        outs = []
        for (m, acc), q, hs in zip(states, qs, heads):
            m_lo, acc_lo = block((m[hq:], acc[hq:]), q[hq:], q0 + hq, hq, hs, mask_b)
            acc = jnp.concatenate([acc[:hq], acc_lo], axis=0)
            denom = jnp.where(lane < V_HEAD, pltpu.roll(acc, V_HEAD, 1), 1.0)
            outs.append(acc / denom)
        o_ref[q0:q0 + tq, :] = jnp.where(lane < V_HEAD, outs[0],
                                         pltpu.roll(outs[1], V_HEAD, 1)).astype(BF16)


def _attention(q, k, v, batch, seq):
    t = q.shape[0]
    pairs = MLA_HEADS // 2
    blk = pl.BlockSpec((seq, 2 * HEAD_PAD), lambda b, p: (b, p))
    return pl.pallas_call(
        _attn_kernel,
        grid=(batch, pairs),
        in_specs=[blk, blk, blk],
        out_specs=pl.BlockSpec((seq, 2 * V_HEAD), lambda b, p: (b, p)),
        out_shape=jax.ShapeDtypeStruct((t, MLA_OUT), BF16),
        compiler_params=_cparams(("arbitrary", "arbitrary")),
        name="mla_attention",
    )(q, k, v)


POOL_HALO = 16


def _pool_kernel(u_ref, pw_ref, ps_ref, o_ref, ext_ref):
    ts = u_ref.shape[0]
    j = pl.program_id(1)

    @pl.when(j == 0)
    def _():
        ext_ref[0:POOL_HALO, :] = jnp.zeros((POOL_HALO, POOL_DIM), F32)

    u = u_ref[...].astype(F32)
    ext_ref[POOL_HALO:POOL_HALO + ts, :] = u
    t_pos = j * ts + lax.broadcasted_iota(jnp.int32, (ts, 1), 0)
    for g, w in enumerate(POOL_WINDOWS):
        gs = slice(g * POOL_GROUP, (g + 1) * POOL_GROUP)
        ug = u[:, gs]
        acc = ug
        for dlt in range(1, w):
            acc = acc + ext_ref[POOL_HALO - dlt:POOL_HALO - dlt + ts, gs]
        count = jnp.minimum(t_pos + 1, w).astype(F32)
        pooled = acc / count - ug
        mixed = jnp.dot(pooled.astype(BF16), pw_ref[0, g], preferred_element_type=F32)
        o_ref[:, gs] = (mixed * ps_ref[0, :, gs]).astype(BF16)
    ext_ref[0:POOL_HALO, :] = ext_ref[ts:ts + POOL_HALO, :]


def _pool(proj, pool_w, pool_scale, layer, batch, seq):
    t = proj.shape[0]
    ts = 512
    per_seq = seq // ts
    return pl.pallas_call(
        _pool_kernel,
        grid=(batch, per_seq),
        in_specs=[pl.BlockSpec((ts, POOL_DIM), lambda b, j: (b * per_seq + j, P_POOL // POOL_DIM)),
                  _layer_block(pool_w, layer), _layer_block(pool_scale, layer)],
        out_specs=pl.BlockSpec((ts, POOL_DIM), lambda b, j: (b * per_seq + j, 0)),
        out_shape=jax.ShapeDtypeStruct((t, POOL_DIM), BF16),
        scratch_shapes=[pltpu.VMEM((ts + POOL_HALO, POOL_DIM), F32)],
        compiler_params=_cparams(("arbitrary", "arbitrary")),
        name="pool_branch",
    )(proj, pool_w, pool_scale)


SSD_STEP_CHUNKS = 4


def _ssd_kernel(zs_ref, xs_ref, bc_ref, dt_ref, dtb_ref, alog_ref, dful_ref, nw_ref,
                exp_ref, tri_ref, o_ref, state_ref, gated_ref):
    lc = SSD_CHUNK

    @pl.when(pl.program_id(1) == 0)
    def _():
        state_ref[...] = jnp.zeros(state_ref.shape, F32)

    rows = lax.broadcasted_iota(jnp.int32, (lc, lc), 0)
    cols = lax.broadcasted_iota(jnp.int32, (lc, lc), 1)
    causal = cols <= rows
    lane = lax.broadcasted_iota(jnp.int32, (lc, 2 * SSD_HEAD_DIM), 1)
    nt = (((1,), (1,)), ((), ()))
    tn = (((0,), (0,)), ((), ()))
    heads_per_group = SSD_HEADS // SSD_GROUPS
    neg_a = -jnp.exp(alog_ref[0])
    dtb = dtb_ref[0]
    dful = dful_ref[0]
    nw = nw_ref[0]

    for r0 in range(0, xs_ref.shape[0], lc):
        rs = slice(r0, r0 + lc)
        xs = xs_ref[rs, :].astype(F32)
        bm = bc_ref[rs, :SSD_GROUPS * SSD_STATE]
        cm = bc_ref[rs, SSD_GROUPS * SSD_STATE:]

        v = dt_ref[rs, :] + dtb
        dt = jnp.maximum(v, 0.0) + jnp.log1p(jnp.exp(-jnp.abs(v)))
        a_cs = _split_dot_left(tri_ref[...], dt * neg_a)
        a_last = a_cs[lc - 1:lc, :]
        stacked = jnp.concatenate([dt, jnp.exp(a_cs), jnp.exp(a_last - a_cs)], axis=0)
        full = _split_dot(stacked, exp_ref[...])
        dt_full = full[0:lc]
        ea_full = full[lc:2 * lc]
        ds_full = full[2 * lc:3 * lc]
        chunk_decay = ea_full[lc - 1:lc, :]
        a_cs_t = a_cs.T

        xdt = xs * dt_full
        xdt_b = xdt.astype(BF16)
        xds_b = (xdt * ds_full).astype(BF16)

        for g in range(SSD_GROUPS):
            gs = slice(g * GROUP_W, (g + 1) * GROUP_W)
            bg = bm[:, g * SSD_STATE:(g + 1) * SSD_STATE]
            cg = cm[:, g * SSD_STATE:(g + 1) * SSD_STATE]
            cb = lax.dot_general(cg, bg, nt, preferred_element_type=F32)
            state = state_ref[g]
            y_off = jnp.dot(cg, state.astype(BF16), preferred_element_type=F32) * ea_full[:, gs]
            s_new = lax.dot_general(bg, xds_b[:, gs], tn, preferred_element_type=F32)
            state_ref[g] = state * chunk_decay[:, gs] + s_new
            for pp in range(heads_per_group // 2):
                h0 = g * heads_per_group + 2 * pp
                ps = slice(h0 * SSD_HEAD_DIM, (h0 + 2) * SSD_HEAD_DIM)
                ys = []
                for hd in (h0, h0 + 1):
                    seg = a_cs[:, hd:hd + 1] - a_cs_t[hd:hd + 1, :]
                    decay = jnp.exp(jnp.where(causal, seg, NEG_BIG))
                    ys.append(jnp.dot((cb * decay).astype(BF16), xdt_b[:, ps],
                                      preferred_element_type=F32))
                y_pair = jnp.where(lane < SSD_HEAD_DIM, ys[0], ys[1])
                ls = slice(2 * pp * SSD_HEAD_DIM, (2 * pp + 2) * SSD_HEAD_DIM)
                y = y_pair + y_off[:, ls] + xs[:, ps] * dful[:, ps]
                gated_ref[rs, ps] = y * zs_ref[rs, ps].astype(F32)

        for g in range(SSD_GROUPS):
            gs = slice(g * GROUP_W, (g + 1) * GROUP_W)
            gt = gated_ref[rs, gs]
            ms = jnp.mean(gt * gt, axis=-1, keepdims=True)
            o_ref[rs, gs] = (gt * lax.rsqrt(ms + EPS) * nw[:, gs]).astype(o_ref.dtype)


def _ssd(proj, dt, dtb, alog, dful, nw, expand, tri, layer, batch, seq):
    t = proj.shape[0]
    lc = SSD_STEP_CHUNKS * SSD_CHUNK
    nc = seq // lc
    lb = lambda a: _layer_block(a, layer)
    return pl.pallas_call(
        _ssd_kernel,
        grid=(batch, nc),
        in_specs=[pl.BlockSpec((lc, SSD_INNER), lambda b, j: (b * nc + j, P_Z // SSD_INNER)),
                  pl.BlockSpec((lc, SSD_INNER), lambda b, j: (b * nc + j, P_XS // SSD_INNER)),
                  pl.BlockSpec((lc, BC_W), lambda b, j: (b * nc + j, P_BC // BC_W)),
                  pl.BlockSpec((lc, DT_PAD), lambda b, j: (b * nc + j, 0)),
                  lb(dtb), lb(alog), lb(dful), lb(nw),
                  _const_block(expand), _const_block(tri)],
        out_specs=pl.BlockSpec((lc, SSD_INNER), lambda b, j: (b * nc + j, 0)),
        out_shape=jax.ShapeDtypeStruct((t, SSD_INNER), BF16),
        scratch_shapes=[pltpu.VMEM((SSD_GROUPS, SSD_STATE, GROUP_W), F32),
                        pltpu.VMEM((lc, SSD_INNER), F32)],
        compiler_params=_cparams(("arbitrary", "arbitrary")),
        name="ssd_branch",
    )(proj, proj, proj, dt, dtb, alog, dful, nw, expand, tri)


def _merge_kernel(x_ref, ga_ref, gb_ref, gc_ref, oa_ref, ob_ref, oc_ref, mod_ref,
                  wb_ref, wo_ref, nw_ref, x_out, h_out):
    m = mod_ref[0, 0]
    ya = jnp.dot(oa_ref[...], wb_ref[0, 0:MLA_OUT, :], preferred_element_type=F32)
    yb = jnp.dot(ob_ref[...], wb_ref[0, MLA_OUT:MLA_OUT + POOL_DIM, :], preferred_element_type=F32)
    yc = jnp.dot(oc_ref[...], wb_ref[0, MLA_OUT + POOL_DIM:, :], preferred_element_type=F32)
    merged = (jax.nn.sigmoid(ga_ref[...].astype(F32)) * ya
              + jax.nn.sigmoid(gb_ref[...].astype(F32)) * yb
              + jax.nn.sigmoid(gc_ref[...].astype(F32)) * yc)
    y = jnp.dot(merged.astype(BF16), wo_ref[0], preferred_element_type=F32)
    x1 = x_ref[...] + m[2:3] * y
    x_out[...] = x1
    h_out[...] = _norm_mod(x1, nw_ref[0], m[3:4], m[4:5]).astype(BF16)


def _merge(xf, proj, o_a, o_b, o_c, mods, wb, wo, norm_w, layer, seq):
    t, d = xf.shape
    tm = 512
    per_seq = seq // tm
    gate = lambda k: pl.BlockSpec((tm, d), lambda i: (i, P_GATE // d + k))
    tile = lambda n: pl.BlockSpec((tm, n), lambda i: (i, 0))
    return pl.pallas_call(
        _merge_kernel,
        grid=(t // tm,),
        in_specs=[tile(d), gate(0), gate(1), gate(2), tile(MLA_OUT), tile(POOL_DIM), tile(SSD_INNER),
                  pl.BlockSpec((1, 1, 6, d), lambda i: (layer, i // per_seq, 0, 0)),
                  _layer_block(wb, layer), _layer_block(wo, layer), _layer_block(norm_w, layer)],
        out_specs=[tile(d), tile(d)],
        out_shape=[jax.ShapeDtypeStruct((t, d), F32), jax.ShapeDtypeStruct((t, d), BF16)],
        compiler_params=_cparams(("arbitrary",)),
        name="merge_out",
    )(xf, proj, proj, proj, o_a, o_b, o_c, mods, wb, wo, norm_w)


def _ffn_kernel(with_next, h_ref, x_ref, mod_ref, *rest):
    if with_next:
        modn_ref, nwn_ref, upw_ref, cw_ref, cb_ref, dw_ref, x_out, h_out, carry_ref, act_ref = rest
    else:
        upw_ref, cw_ref, cb_ref, dw_ref, x_out, carry_ref, act_ref = rest
    ts = h_ref.shape[0]

    @pl.when(pl.program_id(1) == 0)
    def _():
        carry_ref[...] = jnp.zeros(carry_ref.shape, F32)

    h = h_ref[...]

    def conv_half(cs):
        u = jnp.dot(h, upw_ref[0, :, cs], preferred_element_type=F32)
        prev = carry_ref[:, cs]
        carry_ref[:, cs] = u[ts - CONV_HALO:, :]
        return _causal_conv(u, prev, cw_ref[0, :, cs], cb_ref[0, :, cs], FFN_CONV)

    for c in range(FFN_NCHUNK):
        gate = conv_half(slice(c * FFN_CHUNK, (c + 1) * FFN_CHUNK))
        val = conv_half(slice(FFN_DIM + c * FFN_CHUNK, FFN_DIM + (c + 1) * FFN_CHUNK))
        act_ref[c] = (_silu(gate) * val).astype(BF16)
    y = jnp.dot(act_ref[0], dw_ref[0, 0:FFN_CHUNK, :], preferred_element_type=F32)
    for c in range(1, FFN_NCHUNK):
        y = y + jnp.dot(act_ref[c], dw_ref[0, c * FFN_CHUNK:(c + 1) * FFN_CHUNK, :],
                        preferred_element_type=F32)
    m = mod_ref[0, 0]
    x2 = x_ref[...] + m[5:6] * y
    x_out[...] = x2
    if with_next:
        mn = modn_ref[0, 0]
        h_out[...] = _norm_mod(x2, nwn_ref[0], mn[0:1], mn[1:2]).astype(BF16)


def _ffn(h2, x1, mods, upw, cw, cb, dw, norm1_w, layer, seq):
    t, d = x1.shape
    ts = 512
    per_seq = seq // ts
    batch = t // seq
    with_next = layer + 1 < mods.shape[0]
    tile = pl.BlockSpec((ts, d), lambda b, j: (b * per_seq + j, 0))
    modspec = lambda l: pl.BlockSpec((1, 1, 6, d), lambda b, j: (l, b, 0, 0))
    in_specs = [tile, tile, modspec(layer)]
    args = [h2, x1, mods]
    out_specs = [tile]
    out_shape = [jax.ShapeDtypeStruct((t, d), F32)]
    if with_next:
        in_specs += [modspec(layer + 1), _layer_block(norm1_w, layer + 1)]
        args += [mods, norm1_w]
        out_specs.append(tile)
        out_shape.append(jax.ShapeDtypeStruct((t, d), BF16))
    in_specs += [_layer_block(a, layer) for a in (upw, cw, cb, dw)]
    args += [upw, cw, cb, dw]
    outs = pl.pallas_call(
        functools.partial(_ffn_kernel, with_next),
        grid=(batch, per_seq),
        in_specs=in_specs,
        out_specs=out_specs,
        out_shape=out_shape,
        scratch_shapes=[pltpu.VMEM((CONV_HALO, 2 * FFN_DIM), F32),
                        pltpu.VMEM((FFN_NCHUNK, ts, FFN_CHUNK), BF16)],
        compiler_params=_cparams(("arbitrary", "arbitrary")),
        name="conv_mlp",
    )(*args)
    return (outs[0], outs[1]) if with_next else (outs[0], None)


def _head_blocks(w, per_head, lo, hi):
    depth, kdim, _ = w.shape
    w = w.reshape(depth, kdim, MLA_HEADS, per_head)[..., lo:hi]
    w = jnp.pad(w, ((0, 0), (0, 0), (0, 0), (0, HEAD_PAD - (hi - lo))))
    return w.reshape(depth, kdim, MLA_HEADS * HEAD_PAD)


def _group_mean_matrix(sizes):
    blk = jnp.zeros((HEAD_PAD, HEAD_PAD), F32)
    start = 0
    for n in sizes:
        blk = blk.at[start:start + n, start:start + n].set(1.0 / n)
        start += n
    z = jnp.zeros_like(blk)
    return jnp.concatenate([jnp.concatenate([blk, z], 1), jnp.concatenate([z, blk], 1)], 0).astype(BF16)


def _rows(v, pad_to=None):
    if pad_to is not None:
        v = jnp.pad(v, ((0, 0), (0, pad_to - v.shape[1])))
    return v[:, None, :]


def kernel(x, c, positions, ada_w, ada_b, norm1_w, w_in, q_a_norm, w_q_b, kv_a_norm, w_kv_b, q_norm, k_norm, pool_w, pool_scale, ssd_conv_w, ssd_conv_b, ssd_dt_bias, ssd_a_log, ssd_d, ssd_norm_w, w_branch, w_out, norm2_w, ffn_up, ffn_conv_w, ffn_conv_b, ffn_down):
    batch, seq, d = x.shape
    depth = ada_w.shape[0]
    t = batch * seq
    xf = x.reshape(t, d)

    mods = _mods(c, ada_w, ada_b).reshape(depth, batch, 6, d)
    cos_t, sin_t = _rope_tables(positions)

    gq = _group_mean_matrix((QK_NOPE, QK_ROPE))
    gk = _group_mean_matrix((QK_NOPE,))
    head_of_col = jnp.arange(SSD_INNER) // SSD_HEAD_DIM
    expand = (jnp.arange(DT_PAD)[:, None] == head_of_col[None, :]).astype(BF16)
    tri = (jnp.arange(SSD_CHUNK)[None, :] <= jnp.arange(SSD_CHUNK)[:, None]).astype(BF16)

    o = 0
    offs = {}
    for name, size in (("q_lat", Q_LORA), ("c_kv", KV_LORA), ("k_rope", QK_ROPE), ("pool", POOL_DIM),
                       ("z", SSD_INNER), ("xs", SSD_INNER), ("bc", BC_W),
                       ("dt", SSD_HEADS), ("gate", N_BRANCH * D_MODEL)):
        offs[name] = (o, o + size)
        o += size
    col = lambda name: w_in[:, :, offs[name][0]:offs[name][1]]
    zcols = lambda n: jnp.zeros((depth, d, n), F32)
    w_main = jnp.concatenate(
        [col("z"), col("xs"), col("gate"), col("pool"), col("bc"), col("q_lat"), col("c_kv"),
         col("dt"), zcols(QK_NOPE - SSD_HEADS), col("k_rope"), zcols(HEAD_PAD - QK_HEAD)],
        axis=2).astype(BF16)

    wq = _head_blocks(w_q_b, QK_HEAD, 0, QK_HEAD).astype(BF16)
    wkv = jnp.concatenate([_head_blocks(w_kv_b, QK_NOPE + V_HEAD, 0, QK_NOPE),
                           _head_blocks(w_kv_b, QK_NOPE + V_HEAD, QK_NOPE, QK_NOPE + V_HEAD)],
                          axis=2).astype(BF16)
    hw = MLA_HEADS * HEAD_PAD
    qw = jnp.tile(_rows(q_norm, HEAD_PAD), (1, 1, MLA_HEADS)) * (QK_HEAD ** -0.5)
    kw = jnp.tile(_rows(k_norm[:, :QK_NOPE], HEAD_PAD), (1, 1, MLA_HEADS))
    krw = _rows(jnp.pad(k_norm[:, QK_NOPE:], ((0, 0), (QK_NOPE, 0))), HEAD_PAD)
    qan = _rows(q_a_norm)
    kvan = _rows(kv_a_norm)

    pool_wb = pool_w.astype(BF16)
    pool_sc = _rows(pool_scale)
    conv_b = _rows(ssd_conv_b)
    dtb = _rows(ssd_dt_bias, DT_PAD)
    alog = _rows(ssd_a_log, DT_PAD)
    dful = _rows(jnp.repeat(ssd_d, SSD_HEAD_DIM, axis=1))
    ssd_nw = _rows(ssd_norm_w)
    wb = w_branch.astype(BF16)
    wo = w_out.astype(BF16)
    n1 = _rows(norm1_w)
    n2 = _rows(norm2_w)
    upw = ffn_up.astype(BF16)
    dw = ffn_down.astype(BF16)
    fcb = _rows(ffn_conv_b)

    h = _prenorm(xf, mods, n1, 0, seq)
    for l in range(depth):
        proj, dt = _inproj(h, w_main, ssd_conv_w, conv_b, l, batch, seq)
        q, k, v = _mla_prep(proj, cos_t, sin_t, qan, kvan, wq, wkv, gq, gk, qw, kw, krw, l)
        o_a = _attention(q, k, v, batch, seq)
        o_b = _pool(proj, pool_wb, pool_sc, l, batch, seq)
        o_c = _ssd(proj, dt, dtb, alog, dful, ssd_nw, expand, tri, l, batch, seq)
        x1, h2 = _merge(xf, proj, o_a, o_b, o_c, mods, wb, wo, n2, l, seq)
        xf, h = _ffn(h2, x1, mods, upw, ffn_conv_w, fcb, dw, n1, l, seq)
    return xf.reshape(batch, seq, d)
```

```python
import functools

import jax
import jax.numpy as jnp
from jax import lax
from jax.experimental import pallas as pl
from jax.experimental.pallas import tpu as pltpu

F32 = jnp.float32
BF16 = jnp.bfloat16

D_MODEL = 1024
MLA_HEADS = 8
QK_NOPE = 64
QK_ROPE = 32
QK_HEAD = QK_NOPE + QK_ROPE
V_HEAD = 64
Q_LORA = 384
KV_LORA = 256
ROPE_THETA = 10000.0
MLA_OUT = MLA_HEADS * V_HEAD
POOL_WINDOWS = (2, 4, 8, 16)
POOL_GROUP = 128
POOL_DIM = len(POOL_WINDOWS) * POOL_GROUP
SSD_HEADS = 16
SSD_HEAD_DIM = 64
SSD_INNER = SSD_HEADS * SSD_HEAD_DIM
SSD_GROUPS = 2
SSD_STATE = 128
SSD_CONV = 4
SSD_CHUNK = 128
FFN_DIM = 2816
FFN_CONV = 3
N_BRANCH = 3
EPS = 1e-6

SUBLANES = 8
HEAD_PAD = 128
NEG_BIG = -1e30
CONV_HALO = SUBLANES

BC_W = 2 * SSD_GROUPS * SSD_STATE
GROUP_W = SSD_INNER // SSD_GROUPS
P_Z = 0
P_XS = P_Z + SSD_INNER
P_GATE = P_XS + SSD_INNER
P_POOL = P_GATE + N_BRANCH * D_MODEL
P_BC = P_POOL + POOL_DIM
P_LAT = P_BC + BC_W
LAT_W = Q_LORA + KV_LORA + HEAD_PAD
PROJ_W = P_LAT + LAT_W
DT_PAD = HEAD_PAD

FFN_CHUNK = 256
FFN_NCHUNK = FFN_DIM // FFN_CHUNK

VMEM_LIMIT = 56 * 1024 * 1024


def _cparams(sem):
    return pltpu.CompilerParams(dimension_semantics=sem, vmem_limit_bytes=VMEM_LIMIT)


def _const_block(a):
    nd = a.ndim
    return pl.BlockSpec(a.shape, lambda *_: (0,) * nd, pipeline_mode=pl.Buffered(1))


def _layer_block(a, layer):
    nd = a.ndim
    return pl.BlockSpec((1,) + a.shape[1:], lambda *_: (layer,) + (0,) * (nd - 1),
                        pipeline_mode=pl.Buffered(1))


def _silu(v):
    return v * jax.nn.sigmoid(v)


def _norm_mod(x, w, shift, scale):
    y = x * lax.rsqrt(jnp.mean(x * x, axis=-1, keepdims=True) + EPS)
    return (y * w) * (1.0 + scale) + shift


def _shift_rows(u, prev, k):
    full = pltpu.roll(u, k, 0)
    row = lax.broadcasted_iota(jnp.int32, prev.shape, 0)
    first = jnp.where(row < k, pltpu.roll(prev, k, 0), full[:SUBLANES])
    return jnp.concatenate([first, full[SUBLANES:]], axis=0)


def _causal_conv(u, prev, w, b, taps):
    out = w[taps - 1:taps, :] * u + b
    for k in range(taps - 1):
        out = out + w[k:k + 1, :] * _shift_rows(u, prev, taps - 1 - k)
    return out


def _split(x):
    hi = x.astype(BF16)
    return hi, (x - hi.astype(F32)).astype(BF16)


def _split_dot(x, m):
    hi, lo = _split(x)
    return jnp.dot(hi, m, preferred_element_type=F32) + jnp.dot(lo, m, preferred_element_type=F32)


def _split_dot_left(m, x):
    hi, lo = _split(x)
    return jnp.dot(m, hi, preferred_element_type=F32) + jnp.dot(m, lo, preferred_element_type=F32)


def _mods_kernel(c_ref, w_ref, b_ref, o_ref):
    ca = _silu(c_ref[...]).astype(BF16)
    w = w_ref[0].astype(BF16)
    o_ref[0] = jnp.dot(ca, w, preferred_element_type=F32) + b_ref[0]


def _mods(c, ada_w, ada_b):
    depth, d, n = ada_w.shape
    b = c.shape[0]
    tn = 1536
    return pl.pallas_call(
        _mods_kernel,
        grid=(depth, n // tn),
        in_specs=[pl.BlockSpec((b, d), lambda l, j: (0, 0)),
                  pl.BlockSpec((1, d, tn), lambda l, j: (l, 0, j)),
                  pl.BlockSpec((1, 1, tn), lambda l, j: (l, 0, j))],
        out_specs=pl.BlockSpec((1, b, tn), lambda l, j: (l, 0, j)),
        out_shape=jax.ShapeDtypeStruct((depth, b, n), F32),
        compiler_params=_cparams(("arbitrary", "arbitrary")),
        name="adaln_mods",
    )(c, ada_w, ada_b.reshape(depth, 1, n))


def _rope_kernel(pos_ref, freq_ref, sign_ref, c_ref, s_ref):
    ang = pos_ref[...] * freq_ref[...]
    c_ref[...] = jnp.cos(ang)
    s_ref[...] = jnp.sin(ang) * sign_ref[...]


def _rope_tables(positions):
    t = positions.size
    tm = 2048
    half = QK_ROPE // 2
    inv_freq = ROPE_THETA ** (-jnp.arange(0, QK_ROPE, 2, dtype=F32) / QK_ROPE)
    zeros = jnp.zeros((QK_NOPE,), F32)
    tail = jnp.zeros((HEAD_PAD - QK_HEAD,), F32)
    freq = jnp.concatenate([zeros, inv_freq, inv_freq, tail]).reshape(1, HEAD_PAD)
    sign = jnp.concatenate([zeros, -jnp.ones((half,), F32), jnp.ones((half,), F32), tail]).reshape(1, HEAD_PAD)
    pos = positions.astype(F32).reshape(t, 1)
    row = pl.BlockSpec((1, HEAD_PAD), lambda i: (0, 0))
    tab = pl.BlockSpec((tm, HEAD_PAD), lambda i: (i, 0))
    return pl.pallas_call(
        _rope_kernel,
        grid=(t // tm,),
        in_specs=[pl.BlockSpec((tm, 1), lambda i: (i, 0)), row, row],
        out_specs=[tab, tab],
        out_shape=[jax.ShapeDtypeStruct((t, HEAD_PAD), F32)] * 2,
        compiler_params=_cparams(("arbitrary",)),
        name="rope_tables",
    )(pos, freq, sign)


def _prenorm_kernel(x_ref, mod_ref, w_ref, h_ref):
    m = mod_ref[0, 0]
    h_ref[...] = _norm_mod(x_ref[...], w_ref[0], m[0:1], m[1:2]).astype(BF16)


def _prenorm(xf, mods, norm_w, layer, seq):
    t, d = xf.shape
    tm = 1024
    per_seq = seq // tm
    return pl.pallas_call(
        _prenorm_kernel,
        grid=(t // tm,),
        in_specs=[pl.BlockSpec((tm, d), lambda i: (i, 0)),
                  pl.BlockSpec((1, 1, 6, d), lambda i: (layer, i // per_seq, 0, 0)),
                  _layer_block(norm_w, layer)],
        out_specs=pl.BlockSpec((tm, d), lambda i: (i, 0)),
        out_shape=jax.ShapeDtypeStruct((t, d), BF16),
        compiler_params=_cparams(("arbitrary",)),
        name="prenorm",
    )(xf, mods, norm_w)


INPROJ_COLS = 512


def _inproj_kernel(h_ref, w_ref, cw_ref, cb_ref, proj_ref, dt_ref, car_ref):
    tm = h_ref.shape[0]
    wc = INPROJ_COLS

    @pl.when(pl.program_id(1) == 0)
    def _():
        car_ref[...] = jnp.zeros(car_ref.shape, F32)

    h = h_ref[...]

    def mm(c0, width):
        return jnp.dot(h, w_ref[0, :, c0:c0 + width], preferred_element_type=F32)

    for rel in range(0, SSD_INNER, wc):
        proj_ref[:, P_Z + rel:P_Z + rel + wc] = _silu(mm(P_Z + rel, wc)).astype(BF16)
    for rel in range(0, SSD_INNER + BC_W, wc):
        c0 = P_XS + rel if rel < SSD_INNER else P_BC + rel - SSD_INNER
        cs = slice(rel, rel + wc)
        u = mm(c0, wc)
        prev = car_ref[:, cs]
        car_ref[:, cs] = u[tm - CONV_HALO:, :]
        conv = _causal_conv(u, prev, cw_ref[0, :, cs], cb_ref[0, :, cs], SSD_CONV)
        proj_ref[:, c0:c0 + wc] = conv.astype(BF16)
    for c0 in range(P_GATE, P_BC, wc):
        proj_ref[:, c0:c0 + wc] = mm(c0, wc).astype(BF16)
    lat = mm(P_LAT, LAT_W)
    proj_ref[:, P_LAT:] = lat.astype(BF16)
    dt_ref[...] = lat[:, LAT_W - DT_PAD:]


def _inproj(h, w_main, conv_w, conv_b, layer, batch, seq):
    t, d = h.shape
    tm = 512
    per_seq = seq // tm
    rows = lambda n: pl.BlockSpec((tm, n), lambda b, j: (b * per_seq + j, 0))
    return pl.pallas_call(
        _inproj_kernel,
        grid=(batch, per_seq),
        in_specs=[rows(d), _layer_block(w_main, layer), _layer_block(conv_w, layer),
                  _layer_block(conv_b, layer)],
        out_specs=[rows(PROJ_W), rows(DT_PAD)],
        out_shape=[jax.ShapeDtypeStruct((t, PROJ_W), BF16),
                   jax.ShapeDtypeStruct((t, DT_PAD), F32)],
        scratch_shapes=[pltpu.VMEM((CONV_HALO, SSD_INNER + BC_W), F32)],
        compiler_params=_cparams(("arbitrary", "arbitrary")),
        name="in_projection",
    )(h, w_main, conv_w, conv_b)


def _rope_apply(y, cos, sin_signed, lane):
    half = QK_ROPE // 2
    partner = jnp.where(lane < QK_NOPE + half,
                        pltpu.roll(y, HEAD_PAD - half, 1),
                        pltpu.roll(y, half, 1))
    return y * cos + partner * sin_signed


def _mla_prep_kernel(lat_ref, cos_ref, sin_ref, qan_ref, kvan_ref, wq_ref, wkv_ref,
                     gq_ref, gk_ref, qw_ref, kw_ref, krw_ref, q_ref, k_ref, v_ref):
    lat = lat_ref[...].astype(F32)
    q_lat = lat[:, :Q_LORA]
    c_kv = lat[:, Q_LORA:Q_LORA + KV_LORA]
    cos = cos_ref[...]
    sin = sin_ref[...]
    lane = lax.broadcasted_iota(jnp.int32, cos.shape, 1)
    k_rope = jnp.where((lane >= QK_NOPE) & (lane < QK_HEAD), lat[:, Q_LORA + KV_LORA:], 0.0)

    qn = q_lat * lax.rsqrt(jnp.mean(q_lat * q_lat, axis=-1, keepdims=True) + EPS) * qan_ref[0]
    cn = c_kv * lax.rsqrt(jnp.mean(c_kv * c_kv, axis=-1, keepdims=True) + EPS) * kvan_ref[0]
    q = jnp.dot(qn.astype(BF16), wq_ref[0], preferred_element_type=F32)
    kv = jnp.dot(cn.astype(BF16), wkv_ref[0], preferred_element_type=F32)
    kw = MLA_HEADS * HEAD_PAD
    vlane = lax.broadcasted_iota(jnp.int32, (1, kw), 1)
    v_ref[...] = jnp.where(vlane % HEAD_PAD < V_HEAD, kv[:, kw:], 1.0).astype(BF16)

    kr_ms = jnp.sum(k_rope * k_rope, axis=-1, keepdims=True) * (1.0 / QK_ROPE)
    kr = _rope_apply(k_rope * lax.rsqrt(kr_ms + EPS) * krw_ref[0], cos, sin, lane)

    qw = qw_ref[0]
    kwt = kw_ref[0]
    pair = 2 * HEAD_PAD
    for c0 in range(0, kw, pair):
        xq = q[:, c0:c0 + pair]
        ms = jnp.dot((xq * xq).astype(BF16), gq_ref[...], preferred_element_type=F32)
        yq = xq * lax.rsqrt(ms + EPS) * qw[:, c0:c0 + pair]
        xk = kv[:, c0:c0 + pair]
        ms = jnp.dot((xk * xk).astype(BF16), gk_ref[...], preferred_element_type=F32)
        yk = xk * lax.rsqrt(ms + EPS) * kwt[:, c0:c0 + pair]
        for h0 in range(0, pair, HEAD_PAD):
            q_ref[:, c0 + h0:c0 + h0 + HEAD_PAD] = _rope_apply(
                yq[:, h0:h0 + HEAD_PAD], cos, sin, lane).astype(BF16)
            k_ref[:, c0 + h0:c0 + h0 + HEAD_PAD] = (yk[:, h0:h0 + HEAD_PAD] + kr).astype(BF16)


def _mla_prep(proj, cos_t, sin_t, qan, kvan, wq, wkv, gq, gk, qw, kw, krw, layer):
    t = proj.shape[0]
    tm = 512
    hw = MLA_HEADS * HEAD_PAD
    lb = lambda a: _layer_block(a, layer)
    return pl.pallas_call(
        _mla_prep_kernel,
        grid=(t // tm,),
        in_specs=[pl.BlockSpec((tm, LAT_W), lambda i: (i, P_LAT // LAT_W)),
                  pl.BlockSpec((tm, HEAD_PAD), lambda i: (i, 0)),
                  pl.BlockSpec((tm, HEAD_PAD), lambda i: (i, 0)),
                  lb(qan), lb(kvan), lb(wq), lb(wkv),
                  _const_block(gq), _const_block(gk),
                  lb(qw), lb(kw), lb(krw)],
        out_specs=[pl.BlockSpec((tm, hw), lambda i: (i, 0))] * 3,
        out_shape=[jax.ShapeDtypeStruct((t, hw), BF16)] * 3,
        compiler_params=_cparams(("arbitrary",)),
        name="mla_prep",
    )(proj, cos_t, sin_t, qan, kvan, wq, wkv, gq, gk, qw, kw, krw)


ATT_TILE = 512


def _attn_kernel(q_ref, k_ref, v_ref, o_ref):
    seq = q_ref.shape[0]
    tq = ATT_TILE
    hq = tq // 2
    lane = lax.broadcasted_iota(jnp.int32, (tq, HEAD_PAD), 1)
    mask_a = (lax.broadcasted_iota(jnp.int32, (tq, hq), 1)
              <= lax.broadcasted_iota(jnp.int32, (tq, hq), 0))
    mask_b = (lax.broadcasted_iota(jnp.int32, (hq, hq), 1)
              <= lax.broadcasted_iota(jnp.int32, (hq, hq), 0))
    nt = (((1,), (1,)), ((), ()))
    heads = (slice(0, HEAD_PAD), slice(HEAD_PAD, 2 * HEAD_PAD))

    def block(state, q, k0, width, hs, mask):
        m, acc = state
        s = lax.dot_general(q, k_ref[k0:k0 + width, hs], nt, preferred_element_type=F32)
        if mask is not None:
            s = jnp.where(mask, s, NEG_BIG)
        m_new = jnp.maximum(m, jnp.max(s, axis=-1, keepdims=True))
        p = jnp.exp((s - m_new).astype(BF16))
        acc = jnp.exp(m - m_new) * acc + jnp.dot(p, v_ref[k0:k0 + width, hs],
                                                 preferred_element_type=F32)
        return m_new, acc

    for qi in range(seq // tq):
        q0 = qi * tq
        qs = [q_ref[q0:q0 + tq, hs] for hs in heads]
        init = (jnp.full((tq, 1), NEG_BIG, F32), jnp.zeros((tq, HEAD_PAD), F32))
        states = [init, init]
        for j in range(qi):
            states = [block(st, q, j * tq, tq, hs, None) for st, q, hs in zip(states, qs, heads)]
        states = [block(st, q, q0, hq, hs, mask_a) for st, q, hs in zip(states, qs, heads)]
        outs = []
        for (m, acc), q, hs in zip(states, qs, heads):
            m_lo, acc_lo = block((m[hq:], acc[hq:]), q[hq:], q0 + hq, hq, hs, mask_b)
            acc = jnp.concatenate([acc[:hq], acc_lo], axis=0)
            denom = jnp.where(lane < V_HEAD, pltpu.roll(acc, V_HEAD, 1), 1.0)
            outs.append(acc / denom)
        o_ref[q0:q0 + tq, :] = jnp.where(lane < V_HEAD, outs[0],
                                         pltpu.roll(outs[1], V_HEAD, 1)).astype(BF16)


def _attention(q, k, v, batch, seq):
    t = q.shape[0]
    pairs = MLA_HEADS // 2
    blk = pl.BlockSpec((seq, 2 * HEAD_PAD), lambda b, p: (b, p))
    return pl.pallas_call(
        _attn_kernel,
        grid=(batch, pairs),
        in_specs=[blk, blk, blk],
        out_specs=pl.BlockSpec((seq, 2 * V_HEAD), lambda b, p: (b, p)),
        out_shape=jax.ShapeDtypeStruct((t, MLA_OUT), BF16),
        compiler_params=_cparams(("arbitrary", "arbitrary")),
        name="mla_attention",
    )(q, k, v)


POOL_LEVELS = 4


def _pool_kernel(u_ref, pw_ref, ps_ref, o_ref, car_ref):
    ts = u_ref.shape[0]
    j = pl.program_id(1)

    @pl.when(j == 0)
    def _():
        car_ref[...] = jnp.zeros(car_ref.shape, F32)

    t_pos = j * ts + lax.broadcasted_iota(jnp.int32, (ts, 1), 0)
    for g, w in enumerate(POOL_WINDOWS):
        gs = slice(g * POOL_GROUP, (g + 1) * POOL_GROUP)
        ug = u_ref[:, gs].astype(F32)
        acc = ug
        span = 1
        level = 0
        while span < w:
            prev = car_ref[level, :, gs]
            car_ref[level, :, gs] = acc[ts - SUBLANES:, :]
            acc = acc + _shift_rows(acc, prev, span)
            span *= 2
            level += 1
        count = jnp.minimum(t_pos + 1, w).astype(F32)
        pooled = acc / count - ug
        mixed = jnp.dot(pooled.astype(BF16), pw_ref[0, g], preferred_element_type=F32)
        o_ref[:, gs] = (mixed * ps_ref[0, :, gs]).astype(BF16)


def _pool(proj, pool_w, pool_scale, layer, batch, seq):
    t = proj.shape[0]
    ts = 512
    per_seq = seq // ts
    return pl.pallas_call(
        _pool_kernel,
        grid=(batch, per_seq),
        in_specs=[pl.BlockSpec((ts, POOL_DIM), lambda b, j: (b * per_seq + j, P_POOL // POOL_DIM)),
                  _layer_block(pool_w, layer), _layer_block(pool_scale, layer)],
        out_specs=pl.BlockSpec((ts, POOL_DIM), lambda b, j: (b * per_seq + j, 0)),
        out_shape=jax.ShapeDtypeStruct((t, POOL_DIM), BF16),
        scratch_shapes=[pltpu.VMEM((POOL_LEVELS, SUBLANES, POOL_DIM), F32)],
        compiler_params=_cparams(("arbitrary", "arbitrary")),
        name="pool_branch",
    )(proj, pool_w, pool_scale)


SSD_STEP_CHUNKS = 4


def _ssd_kernel(zs_ref, xs_ref, bc_ref, dt_ref, dtb_ref, alog_ref, dful_ref, nw_ref,
                exp_ref, tri_ref, o_ref, state_ref, gated_ref):
    lc = SSD_CHUNK

    @pl.when(pl.program_id(1) == 0)
    def _():
        state_ref[...] = jnp.zeros(state_ref.shape, F32)

    rows = lax.broadcasted_iota(jnp.int32, (lc, lc), 0)
    cols = lax.broadcasted_iota(jnp.int32, (lc, lc), 1)
    causal = cols <= rows
    lane = lax.broadcasted_iota(jnp.int32, (lc, 2 * SSD_HEAD_DIM), 1)
    nt = (((1,), (1,)), ((), ()))
    tn = (((0,), (0,)), ((), ()))
    heads_per_group = SSD_HEADS // SSD_GROUPS
    neg_a = -jnp.exp(alog_ref[0])
    dtb = dtb_ref[0]
    dful = dful_ref[0]
    nw = nw_ref[0]

    for r0 in range(0, xs_ref.shape[0], lc):
        rs = slice(r0, r0 + lc)
        xs = _silu(xs_ref[rs, :].astype(F32))
        bc = _silu(bc_ref[rs, :].astype(F32)).astype(BF16)
        bm = bc[:, :SSD_GROUPS * SSD_STATE]
        cm = bc[:, SSD_GROUPS * SSD_STATE:]

        v = dt_ref[rs, :] + dtb
        dt = jnp.maximum(v, 0.0) + jnp.log1p(jnp.exp(-jnp.abs(v)))
        a_cs = _split_dot_left(tri_ref[...], dt * neg_a)
        a_last = a_cs[lc - 1:lc, :]
        stacked = jnp.concatenate([dt, jnp.exp(a_cs), jnp.exp(a_last - a_cs)], axis=0)
        full = _split_dot(stacked, exp_ref[...])
        dt_full = full[0:lc]
        ea_full = full[lc:2 * lc]
        ds_full = full[2 * lc:3 * lc]
        chunk_decay = ea_full[lc - 1:lc, :]
        a_cs_t = a_cs.T

        xdt = xs * dt_full
        xdt_b = xdt.astype(BF16)
        xds_b = (xdt * ds_full).astype(BF16)

        for g in range(SSD_GROUPS):
            gs = slice(g * GROUP_W, (g + 1) * GROUP_W)
            bg = bm[:, g * SSD_STATE:(g + 1) * SSD_STATE]
            cg = cm[:, g * SSD_STATE:(g + 1) * SSD_STATE]
            cb = lax.dot_general(cg, bg, nt, preferred_element_type=F32)
            state = state_ref[g]
            y_off = jnp.dot(cg, state.astype(BF16), preferred_element_type=F32) * ea_full[:, gs]
            s_new = lax.dot_general(bg, xds_b[:, gs], tn, preferred_element_type=F32)
            state_ref[g] = state * chunk_decay[:, gs] + s_new
            for pp in range(heads_per_group // 2):
                h0 = g * heads_per_group + 2 * pp
                ps = slice(h0 * SSD_HEAD_DIM, (h0 + 2) * SSD_HEAD_DIM)
                ys = []
                for hd in (h0, h0 + 1):
                    seg = a_cs[:, hd:hd + 1] - a_cs_t[hd:hd + 1, :]
                    decay = jnp.exp(jnp.where(causal, seg, NEG_BIG))
                    ys.append(jnp.dot((cb * decay).astype(BF16), xdt_b[:, ps],
                                      preferred_element_type=F32))
                y_pair = jnp.where(lane < SSD_HEAD_DIM, ys[0], ys[1])
                ls = slice(2 * pp * SSD_HEAD_DIM, (2 * pp + 2) * SSD_HEAD_DIM)
                y = y_pair + y_off[:, ls] + xs[:, ps] * dful[:, ps]
                gated_ref[rs, ps] = y * zs_ref[rs, ps].astype(F32)

        for g in range(SSD_GROUPS):
            gs = slice(g * GROUP_W, (g + 1) * GROUP_W)
            gt = gated_ref[rs, gs]
            ms = jnp.mean(gt * gt, axis=-1, keepdims=True)
            o_ref[rs, gs] = (gt * lax.rsqrt(ms + EPS) * nw[:, gs]).astype(o_ref.dtype)


def _ssd(proj, dt, dtb, alog, dful, nw, expand, tri, layer, batch, seq):
    t = proj.shape[0]
    lc = SSD_STEP_CHUNKS * SSD_CHUNK
    nc = seq // lc
    lb = lambda a: _layer_block(a, layer)
    return pl.pallas_call(
        _ssd_kernel,
        grid=(batch, nc),
        in_specs=[pl.BlockSpec((lc, SSD_INNER), lambda b, j: (b * nc + j, P_Z // SSD_INNER)),
                  pl.BlockSpec((lc, SSD_INNER), lambda b, j: (b * nc + j, P_XS // SSD_INNER)),
                  pl.BlockSpec((lc, BC_W), lambda b, j: (b * nc + j, P_BC // BC_W)),
                  pl.BlockSpec((lc, DT_PAD), lambda b, j: (b * nc + j, 0)),
                  lb(dtb), lb(alog), lb(dful), lb(nw),
                  _const_block(expand), _const_block(tri)],
        out_specs=pl.BlockSpec((lc, SSD_INNER), lambda b, j: (b * nc + j, 0)),
        out_shape=jax.ShapeDtypeStruct((t, SSD_INNER), BF16),
        scratch_shapes=[pltpu.VMEM((SSD_GROUPS, SSD_STATE, GROUP_W), F32),
                        pltpu.VMEM((lc, SSD_INNER), F32)],
        compiler_params=_cparams(("arbitrary", "arbitrary")),
        name="ssd_branch",
    )(proj, proj, proj, dt, dtb, alog, dful, nw, expand, tri)


def _merge_kernel(x_ref, ga_ref, gb_ref, gc_ref, oa_ref, ob_ref, oc_ref, mod_ref,
                  wb_ref, wo_ref, nw_ref, x_out, h_out):
    m = mod_ref[0, 0]
    ya = jnp.dot(oa_ref[...], wb_ref[0, 0:MLA_OUT, :], preferred_element_type=F32)
    yb = jnp.dot(ob_ref[...], wb_ref[0, MLA_OUT:MLA_OUT + POOL_DIM, :], preferred_element_type=F32)
    yc = jnp.dot(oc_ref[...], wb_ref[0, MLA_OUT + POOL_DIM:, :], preferred_element_type=F32)
    merged = (jax.nn.sigmoid(ga_ref[...].astype(F32)) * ya
              + jax.nn.sigmoid(gb_ref[...].astype(F32)) * yb
              + jax.nn.sigmoid(gc_ref[...].astype(F32)) * yc)
    y = jnp.dot(merged.astype(BF16), wo_ref[0], preferred_element_type=F32)
    x1 = x_ref[...] + m[2:3] * y
    x_out[...] = x1
    h_out[...] = _norm_mod(x1, nw_ref[0], m[3:4], m[4:5]).astype(BF16)


def _merge(xf, proj, o_a, o_b, o_c, mods, wb, wo, norm_w, layer, seq):
    t, d = xf.shape
    tm = 512
    per_seq = seq // tm
    gate = lambda k: pl.BlockSpec((tm, d), lambda i: (i, P_GATE // d + k))
    tile = lambda n: pl.BlockSpec((tm, n), lambda i: (i, 0))
    return pl.pallas_call(
        _merge_kernel,
        grid=(t // tm,),
        in_specs=[tile(d), gate(0), gate(1), gate(2), tile(MLA_OUT), tile(POOL_DIM), tile(SSD_INNER),
                  pl.BlockSpec((1, 1, 6, d), lambda i: (layer, i // per_seq, 0, 0)),
                  _layer_block(wb, layer), _layer_block(wo, layer), _layer_block(norm_w, layer)],
        out_specs=[tile(d), tile(d)],
        out_shape=[jax.ShapeDtypeStruct((t, d), F32), jax.ShapeDtypeStruct((t, d), BF16)],
        compiler_params=_cparams(("arbitrary",)),
        name="merge_out",
    )(xf, proj, proj, proj, o_a, o_b, o_c, mods, wb, wo, norm_w)


def _ffn_kernel(with_next, h_ref, x_ref, mod_ref, *rest):
    if with_next:
        modn_ref, nwn_ref, upw_ref, cw_ref, cb_ref, dw_ref, x_out, h_out, carry_ref, act_ref = rest
    else:
        upw_ref, cw_ref, cb_ref, dw_ref, x_out, carry_ref, act_ref = rest
    ts = h_ref.shape[0]

    @pl.when(pl.program_id(1) == 0)
    def _():
        carry_ref[...] = jnp.zeros(carry_ref.shape, F32)

    h = h_ref[...]

    def conv_half(cs):
        u = jnp.dot(h, upw_ref[0, :, cs], preferred_element_type=F32)
        prev = carry_ref[:, cs]
        carry_ref[:, cs] = u[ts - CONV_HALO:, :]
        return _causal_conv(u, prev, cw_ref[0, :, cs], cb_ref[0, :, cs], FFN_CONV)

    for c in range(FFN_NCHUNK):
        gate = conv_half(slice(c * FFN_CHUNK, (c + 1) * FFN_CHUNK))
        val = conv_half(slice(FFN_DIM + c * FFN_CHUNK, FFN_DIM + (c + 1) * FFN_CHUNK))
        act_ref[c] = (_silu(gate) * val).astype(BF16)
    y = jnp.dot(act_ref[0], dw_ref[0, 0:FFN_CHUNK, :], preferred_element_type=F32)
    for c in range(1, FFN_NCHUNK):
        y = y + jnp.dot(act_ref[c], dw_ref[0, c * FFN_CHUNK:(c + 1) * FFN_CHUNK, :],
                        preferred_element_type=F32)
    m = mod_ref[0, 0]
    x2 = x_ref[...] + m[5:6] * y
    x_out[...] = x2
    if with_next:
        mn = modn_ref[0, 0]
        h_out[...] = _norm_mod(x2, nwn_ref[0], mn[0:1], mn[1:2]).astype(BF16)


def _ffn(h2, x1, mods, upw, cw, cb, dw, norm1_w, layer, seq):
    t, d = x1.shape
    ts = 512
    per_seq = seq // ts
    batch = t // seq
    with_next = layer + 1 < mods.shape[0]
    tile = pl.BlockSpec((ts, d), lambda b, j: (b * per_seq + j, 0))
    modspec = lambda l: pl.BlockSpec((1, 1, 6, d), lambda b, j: (l, b, 0, 0))
    in_specs = [tile, tile, modspec(layer)]
    args = [h2, x1, mods]
    out_specs = [tile]
    out_shape = [jax.ShapeDtypeStruct((t, d), F32)]
    if with_next:
        in_specs += [modspec(layer + 1), _layer_block(norm1_w, layer + 1)]
        args += [mods, norm1_w]
        out_specs.append(tile)
        out_shape.append(jax.ShapeDtypeStruct((t, d), BF16))
    in_specs += [_layer_block(a, layer) for a in (upw, cw, cb, dw)]
    args += [upw, cw, cb, dw]
    outs = pl.pallas_call(
        functools.partial(_ffn_kernel, with_next),
        grid=(batch, per_seq),
        in_specs=in_specs,
        out_specs=out_specs,
        out_shape=out_shape,
        scratch_shapes=[pltpu.VMEM((CONV_HALO, 2 * FFN_DIM), F32),
                        pltpu.VMEM((FFN_NCHUNK, ts, FFN_CHUNK), BF16)],
        compiler_params=_cparams(("arbitrary", "arbitrary")),
        name="conv_mlp",
    )(*args)
    return (outs[0], outs[1]) if with_next else (outs[0], None)


def _relayout_kernel(pieces, w_ref, tail_ref, o_ref):
    w = w_ref[0]
    for src0, width, dst0 in pieces:
        o_ref[0, :, dst0:dst0 + width] = w[:, src0:src0 + width].astype(BF16)
    o_ref[0, :, PROJ_W - HEAD_PAD:] = tail_ref[0].astype(BF16)


def _relayout_w_in(w_in, tail, pieces):
    depth, d, n_in = w_in.shape
    tk = 128
    return pl.pallas_call(
        functools.partial(_relayout_kernel, pieces),
        grid=(depth, d // tk),
        in_specs=[pl.BlockSpec((1, tk, n_in), lambda l, i: (l, i, 0)),
                  pl.BlockSpec((1, tk, HEAD_PAD), lambda l, i: (l, i, 0))],
        out_specs=pl.BlockSpec((1, tk, PROJ_W), lambda l, i: (l, i, 0)),
        out_shape=jax.ShapeDtypeStruct((depth, d, PROJ_W), BF16),
        compiler_params=_cparams(("arbitrary", "arbitrary")),
        name="w_in_relayout",
    )(w_in, tail)


def _head_blocks(w, per_head, lo, hi):
    depth, kdim, _ = w.shape
    w = w.reshape(depth, kdim, MLA_HEADS, per_head)[..., lo:hi]
    w = jnp.pad(w, ((0, 0), (0, 0), (0, 0), (0, HEAD_PAD - (hi - lo))))
    return w.reshape(depth, kdim, MLA_HEADS * HEAD_PAD)


def _group_mean_matrix(sizes):
    blk = jnp.zeros((HEAD_PAD, HEAD_PAD), F32)
    start = 0
    for n in sizes:
        blk = blk.at[start:start + n, start:start + n].set(1.0 / n)
        start += n
    z = jnp.zeros_like(blk)
    return jnp.concatenate([jnp.concatenate([blk, z], 1), jnp.concatenate([z, blk], 1)], 0).astype(BF16)


def _rows(v, pad_to=None):
    if pad_to is not None:
        v = jnp.pad(v, ((0, 0), (0, pad_to - v.shape[1])))
    return v[:, None, :]


def kernel(x, c, positions, ada_w, ada_b, norm1_w, w_in, q_a_norm, w_q_b, kv_a_norm, w_kv_b, q_norm, k_norm, pool_w, pool_scale, ssd_conv_w, ssd_conv_b, ssd_dt_bias, ssd_a_log, ssd_d, ssd_norm_w, w_branch, w_out, norm2_w, ffn_up, ffn_conv_w, ffn_conv_b, ffn_down):
    batch, seq, d = x.shape
    depth = ada_w.shape[0]
    t = batch * seq
    xf = x.reshape(t, d)

    mods = _mods(c, ada_w, ada_b).reshape(depth, batch, 6, d)
    cos_t, sin_t = _rope_tables(positions)

    gq = _group_mean_matrix((QK_NOPE, QK_ROPE))
    gk = _group_mean_matrix((QK_NOPE,))
    head_of_col = jnp.arange(SSD_INNER) // SSD_HEAD_DIM
    expand = (jnp.arange(DT_PAD)[:, None] == head_of_col[None, :]).astype(BF16)
    tri = (jnp.arange(SSD_CHUNK)[None, :] <= jnp.arange(SSD_CHUNK)[:, None]).astype(BF16)

    o = 0
    offs = {}
    for name, size in (("q_lat", Q_LORA), ("c_kv", KV_LORA), ("k_rope", QK_ROPE), ("pool", POOL_DIM),
                       ("z", SSD_INNER), ("xs", SSD_INNER), ("bc", BC_W),
                       ("dt", SSD_HEADS), ("gate", N_BRANCH * D_MODEL)):
        offs[name] = (o, o + size)
        o += size
    col = lambda name: w_in[:, :, offs[name][0]:offs[name][1]]
    zcols = lambda n: jnp.zeros((depth, d, n), F32)
    tail = jnp.concatenate([col("dt"), zcols(QK_NOPE - SSD_HEADS), col("k_rope"),
                            zcols(HEAD_PAD - QK_HEAD)], axis=2)
    pieces = []
    for name, dst0 in (("z", P_Z), ("xs", P_XS), ("gate", P_GATE), ("pool", P_POOL), ("bc", P_BC),
                       ("q_lat", P_LAT), ("c_kv", P_LAT + Q_LORA)):
        pieces.append((offs[name][0], offs[name][1] - offs[name][0], dst0))
    w_main = _relayout_w_in(w_in, tail, tuple(pieces))

    wq = _head_blocks(w_q_b, QK_HEAD, 0, QK_HEAD).astype(BF16)
    wkv = jnp.concatenate([_head_blocks(w_kv_b, QK_NOPE + V_HEAD, 0, QK_NOPE),
                           _head_blocks(w_kv_b, QK_NOPE + V_HEAD, QK_NOPE, QK_NOPE + V_HEAD)],
                          axis=2).astype(BF16)
    hw = MLA_HEADS * HEAD_PAD
    qw = jnp.tile(_rows(q_norm, HEAD_PAD), (1, 1, MLA_HEADS)) * (QK_HEAD ** -0.5)
    kw = jnp.tile(_rows(k_norm[:, :QK_NOPE], HEAD_PAD), (1, 1, MLA_HEADS))
    krw = _rows(jnp.pad(k_norm[:, QK_NOPE:], ((0, 0), (QK_NOPE, 0))), HEAD_PAD)
    qan = _rows(q_a_norm)
    kvan = _rows(kv_a_norm)

    pool_wb = pool_w.astype(BF16)
    pool_sc = _rows(pool_scale)
    conv_b = _rows(ssd_conv_b)
    dtb = _rows(ssd_dt_bias, DT_PAD)
    alog = _rows(ssd_a_log, DT_PAD)
    dful = _rows(jnp.repeat(ssd_d, SSD_HEAD_DIM, axis=1))
    ssd_nw = _rows(ssd_norm_w)
    wb = w_branch.astype(BF16)
    wo = w_out.astype(BF16)
    n1 = _rows(norm1_w)
    n2 = _rows(norm2_w)
    upw = ffn_up.astype(BF16)
    dw = ffn_down.astype(BF16)
    fcb = _rows(ffn_conv_b)

    h = _prenorm(xf, mods, n1, 0, seq)
    for l in range(depth):
        proj, dt = _inproj(h, w_main, ssd_conv_w, conv_b, l, batch, seq)
        q, k, v = _mla_prep(proj, cos_t, sin_t, qan, kvan, wq, wkv, gq, gk, qw, kw, krw, l)
        o_a = _attention(q, k, v, batch, seq)
        o_b = _pool(proj, pool_wb, pool_sc, l, batch, seq)
        o_c = _ssd(proj, dt, dtb, alog, dful, ssd_nw, expand, tri, l, batch, seq)
        x1, h2 = _merge(xf, proj, o_a, o_b, o_c, mods, wb, wo, n2, l, seq)
        xf, h = _ffn(h2, x1, mods, upw, ffn_conv_w, fcb, dw, n1, l, seq)
    return xf.reshape(batch, seq, d)
```

```python
import functools

import jax
import jax.numpy as jnp
from jax import lax
from jax.experimental import pallas as pl
from jax.experimental.pallas import tpu as pltpu

F32 = jnp.float32
BF16 = jnp.bfloat16

D_MODEL = 1024
MLA_HEADS = 8
QK_NOPE = 64
QK_ROPE = 32
QK_HEAD = QK_NOPE + QK_ROPE
V_HEAD = 64
Q_LORA = 384
KV_LORA = 256
ROPE_THETA = 10000.0
MLA_OUT = MLA_HEADS * V_HEAD
POOL_WINDOWS = (2, 4, 8, 16)
POOL_GROUP = 128
POOL_DIM = len(POOL_WINDOWS) * POOL_GROUP
SSD_HEADS = 16
SSD_HEAD_DIM = 64
SSD_INNER = SSD_HEADS * SSD_HEAD_DIM
SSD_GROUPS = 2
SSD_STATE = 128
SSD_CONV = 4
SSD_CHUNK = 128
FFN_DIM = 2816
FFN_CONV = 3
N_BRANCH = 3
EPS = 1e-6

SUBLANES = 8
HEAD_PAD = 128
NEG_BIG = -1e30
CONV_HALO = SUBLANES

BC_W = 2 * SSD_GROUPS * SSD_STATE
GROUP_W = SSD_INNER // SSD_GROUPS
P_Z = 0
P_XS = P_Z + SSD_INNER
P_GATE = P_XS + SSD_INNER
P_POOL = P_GATE + N_BRANCH * D_MODEL
P_BC = P_POOL + POOL_DIM
P_LAT = P_BC + BC_W
LAT_W = Q_LORA + KV_LORA + HEAD_PAD
PROJ_W = P_LAT + LAT_W
DT_PAD = HEAD_PAD

FFN_CHUNK = 256
FFN_NCHUNK = FFN_DIM // FFN_CHUNK

VMEM_LIMIT = 56 * 1024 * 1024


def _cparams(sem):
    return pltpu.CompilerParams(dimension_semantics=sem, vmem_limit_bytes=VMEM_LIMIT)


def _const_block(a):
    nd = a.ndim
    return pl.BlockSpec(a.shape, lambda *_: (0,) * nd, pipeline_mode=pl.Buffered(1))


def _layer_block(a, layer):
    nd = a.ndim
    return pl.BlockSpec((1,) + a.shape[1:], lambda *_: (layer,) + (0,) * (nd - 1),
                        pipeline_mode=pl.Buffered(1))


def _sigmoid(v):
    return 0.5 * jnp.tanh(0.5 * v) + 0.5


def _silu(v):
    half = 0.5 * v
    return half * jnp.tanh(half) + half


def _norm_mod(x, w, shift, scale):
    y = x * lax.rsqrt(jnp.mean(x * x, axis=-1, keepdims=True) + EPS)
    return (y * w) * (1.0 + scale) + shift


def _shift_rows(u, prev, k):
    full = pltpu.roll(u, k, 0)
    row = lax.broadcasted_iota(jnp.int32, prev.shape, 0)
    first = jnp.where(row < k, pltpu.roll(prev, k, 0), full[:SUBLANES])
    return jnp.concatenate([first, full[SUBLANES:]], axis=0)


def _causal_conv(u, prev, w, b, taps):
    out = w[taps - 1:taps, :] * u + b
    for k in range(taps - 1):
        out = out + w[k:k + 1, :] * _shift_rows(u, prev, taps - 1 - k)
    return out


def _split(x):
    hi = x.astype(BF16)
    return hi, (x - hi.astype(F32)).astype(BF16)


def _split_dot(x, m):
    hi, lo = _split(x)
    return jnp.dot(hi, m, preferred_element_type=F32) + jnp.dot(lo, m, preferred_element_type=F32)


def _split_dot_left(m, x):
    hi, lo = _split(x)
    return jnp.dot(m, hi, preferred_element_type=F32) + jnp.dot(m, lo, preferred_element_type=F32)


def _mods_kernel(c_ref, w_ref, b_ref, o_ref):
    ca = _silu(c_ref[...]).astype(BF16)
    w = w_ref[0].astype(BF16)
    o_ref[0] = jnp.dot(ca, w, preferred_element_type=F32) + b_ref[0]


def _mods(c, ada_w, ada_b):
    depth, d, n = ada_w.shape
    b = c.shape[0]
    tn = 1536
    return pl.pallas_call(
        _mods_kernel,
        grid=(depth, n // tn),
        in_specs=[pl.BlockSpec((b, d), lambda l, j: (0, 0)),
                  pl.BlockSpec((1, d, tn), lambda l, j: (l, 0, j)),
                  pl.BlockSpec((1, 1, tn), lambda l, j: (l, 0, j))],
        out_specs=pl.BlockSpec((1, b, tn), lambda l, j: (l, 0, j)),
        out_shape=jax.ShapeDtypeStruct((depth, b, n), F32),
        compiler_params=_cparams(("arbitrary", "arbitrary")),
        name="adaln_mods",
    )(c, ada_w, ada_b.reshape(depth, 1, n))


def _rope_kernel(pos_ref, freq_ref, sign_ref, c_ref, s_ref):
    ang = pos_ref[...] * freq_ref[...]
    c_ref[...] = jnp.cos(ang)
    s_ref[...] = jnp.sin(ang) * sign_ref[...]


def _rope_tables(positions):
    t = positions.size
    tm = 2048
    half = QK_ROPE // 2
    inv_freq = ROPE_THETA ** (-jnp.arange(0, QK_ROPE, 2, dtype=F32) / QK_ROPE)
    zeros = jnp.zeros((QK_NOPE,), F32)
    tail = jnp.zeros((HEAD_PAD - QK_HEAD,), F32)
    freq = jnp.concatenate([zeros, inv_freq, inv_freq, tail]).reshape(1, HEAD_PAD)
    sign = jnp.concatenate([zeros, -jnp.ones((half,), F32), jnp.ones((half,), F32), tail]).reshape(1, HEAD_PAD)
    pos = positions.astype(F32).reshape(t, 1)
    row = pl.BlockSpec((1, HEAD_PAD), lambda i: (0, 0))
    tab = pl.BlockSpec((tm, HEAD_PAD), lambda i: (i, 0))
    return pl.pallas_call(
        _rope_kernel,
        grid=(t // tm,),
        in_specs=[pl.BlockSpec((tm, 1), lambda i: (i, 0)), row, row],
        out_specs=[tab, tab],
        out_shape=[jax.ShapeDtypeStruct((t, HEAD_PAD), F32)] * 2,
        compiler_params=_cparams(("arbitrary",)),
        name="rope_tables",
    )(pos, freq, sign)


def _prenorm_kernel(x_ref, mod_ref, w_ref, h_ref):
    m = mod_ref[0, 0]
    h_ref[...] = _norm_mod(x_ref[...], w_ref[0], m[0:1], m[1:2]).astype(BF16)


def _prenorm(xf, mods, norm_w, layer, seq):
    t, d = xf.shape
    tm = 1024
    per_seq = seq // tm
    return pl.pallas_call(
        _prenorm_kernel,
        grid=(t // tm,),
        in_specs=[pl.BlockSpec((tm, d), lambda i: (i, 0)),
                  pl.BlockSpec((1, 1, 6, d), lambda i: (layer, i // per_seq, 0, 0)),
                  _layer_block(norm_w, layer)],
        out_specs=pl.BlockSpec((tm, d), lambda i: (i, 0)),
        out_shape=jax.ShapeDtypeStruct((t, d), BF16),
        compiler_params=_cparams(("arbitrary",)),
        name="prenorm",
    )(xf, mods, norm_w)


INPROJ_COLS = 512


def _inproj_kernel(h_ref, w_ref, cw_ref, cb_ref, proj_ref, dt_ref, car_ref):
    tm = h_ref.shape[0]
    wc = INPROJ_COLS

    @pl.when(pl.program_id(1) == 0)
    def _():
        car_ref[...] = jnp.zeros(car_ref.shape, F32)

    h = h_ref[...]

    def mm(c0, width):
        return jnp.dot(h, w_ref[0, :, c0:c0 + width], preferred_element_type=F32)

    for rel in range(0, SSD_INNER, wc):
        proj_ref[:, P_Z + rel:P_Z + rel + wc] = _silu(mm(P_Z + rel, wc)).astype(BF16)
    for rel in range(0, SSD_INNER + BC_W, wc):
        c0 = P_XS + rel if rel < SSD_INNER else P_BC + rel - SSD_INNER
        cs = slice(rel, rel + wc)
        u = mm(c0, wc)
        prev = car_ref[:, cs]
        car_ref[:, cs] = u[tm - CONV_HALO:, :]
        conv = _causal_conv(u, prev, cw_ref[0, :, cs], cb_ref[0, :, cs], SSD_CONV)
        proj_ref[:, c0:c0 + wc] = conv.astype(BF16)
    for c0 in range(P_GATE, P_BC, wc):
        proj_ref[:, c0:c0 + wc] = mm(c0, wc).astype(BF16)
    lat = mm(P_LAT, LAT_W)
    proj_ref[:, P_LAT:] = lat.astype(BF16)
    dt_ref[...] = lat[:, LAT_W - DT_PAD:]


def _inproj(h, w_main, conv_w, conv_b, layer, batch, seq):
    t, d = h.shape
    tm = 512
    per_seq = seq // tm
    rows = lambda n: pl.BlockSpec((tm, n), lambda b, j: (b * per_seq + j, 0))
    return pl.pallas_call(
        _inproj_kernel,
        grid=(batch, per_seq),
        in_specs=[rows(d), _layer_block(w_main, layer), _layer_block(conv_w, layer),
                  _layer_block(conv_b, layer)],
        out_specs=[rows(PROJ_W), rows(DT_PAD)],
        out_shape=[jax.ShapeDtypeStruct((t, PROJ_W), BF16),
                   jax.ShapeDtypeStruct((t, DT_PAD), F32)],
        scratch_shapes=[pltpu.VMEM((CONV_HALO, SSD_INNER + BC_W), F32)],
        compiler_params=_cparams(("arbitrary", "arbitrary")),
        name="in_projection",
    )(h, w_main, conv_w, conv_b)


def _rope_apply(y, cos, sin_signed, lane):
    half = QK_ROPE // 2
    partner = jnp.where(lane < QK_NOPE + half,
                        pltpu.roll(y, HEAD_PAD - half, 1),
                        pltpu.roll(y, half, 1))
    return y * cos + partner * sin_signed


def _mla_prep_kernel(lat_ref, cos_ref, sin_ref, qan_ref, kvan_ref, wq_ref, wkv_ref,
                     gq_ref, gk_ref, qw_ref, kw_ref, krw_ref, q_ref, k_ref, v_ref):
    lat = lat_ref[...].astype(F32)
    q_lat = lat[:, :Q_LORA]
    c_kv = lat[:, Q_LORA:Q_LORA + KV_LORA]
    cos = cos_ref[...]
    sin = sin_ref[...]
    lane = lax.broadcasted_iota(jnp.int32, cos.shape, 1)
    k_rope = jnp.where((lane >= QK_NOPE) & (lane < QK_HEAD), lat[:, Q_LORA + KV_LORA:], 0.0)

    qn = q_lat * lax.rsqrt(jnp.mean(q_lat * q_lat, axis=-1, keepdims=True) + EPS) * qan_ref[0]
    cn = c_kv * lax.rsqrt(jnp.mean(c_kv * c_kv, axis=-1, keepdims=True) + EPS) * kvan_ref[0]
    q = jnp.dot(qn.astype(BF16), wq_ref[0], preferred_element_type=F32)
    kv = jnp.dot(cn.astype(BF16), wkv_ref[0], preferred_element_type=F32)
    kw = MLA_HEADS * HEAD_PAD
    vlane = lax.broadcasted_iota(jnp.int32, (1, kw), 1)
    v_ref[...] = jnp.where(vlane % HEAD_PAD < V_HEAD, kv[:, kw:], 1.0).astype(BF16)

    kr_ms = jnp.sum(k_rope * k_rope, axis=-1, keepdims=True) * (1.0 / QK_ROPE)
    kr = _rope_apply(k_rope * lax.rsqrt(kr_ms + EPS) * krw_ref[0], cos, sin, lane)

    qw = qw_ref[0]
    kwt = kw_ref[0]
    pair = 2 * HEAD_PAD
    for c0 in range(0, kw, pair):
        xq = q[:, c0:c0 + pair]
        ms = jnp.dot((xq * xq).astype(BF16), gq_ref[...], preferred_element_type=F32)
        yq = xq * lax.rsqrt(ms + EPS) * qw[:, c0:c0 + pair]
        xk = kv[:, c0:c0 + pair]
        ms = jnp.dot((xk * xk).astype(BF16), gk_ref[...], preferred_element_type=F32)
        yk = xk * lax.rsqrt(ms + EPS) * kwt[:, c0:c0 + pair]
        for h0 in range(0, pair, HEAD_PAD):
            q_ref[:, c0 + h0:c0 + h0 + HEAD_PAD] = _rope_apply(
                yq[:, h0:h0 + HEAD_PAD], cos, sin, lane).astype(BF16)
            k_ref[:, c0 + h0:c0 + h0 + HEAD_PAD] = (yk[:, h0:h0 + HEAD_PAD] + kr).astype(BF16)


def _mla_prep(proj, cos_t, sin_t, qan, kvan, wq, wkv, gq, gk, qw, kw, krw, layer):
    t = proj.shape[0]
    tm = 512
    hw = MLA_HEADS * HEAD_PAD
    lb = lambda a: _layer_block(a, layer)
    return pl.pallas_call(
        _mla_prep_kernel,
        grid=(t // tm,),
        in_specs=[pl.BlockSpec((tm, LAT_W), lambda i: (i, P_LAT // LAT_W)),
                  pl.BlockSpec((tm, HEAD_PAD), lambda i: (i, 0)),
                  pl.BlockSpec((tm, HEAD_PAD), lambda i: (i, 0)),
                  lb(qan), lb(kvan), lb(wq), lb(wkv),
                  _const_block(gq), _const_block(gk),
                  lb(qw), lb(kw), lb(krw)],
        out_specs=[pl.BlockSpec((tm, hw), lambda i: (i, 0))] * 3,
        out_shape=[jax.ShapeDtypeStruct((t, hw), BF16)] * 3,
        compiler_params=_cparams(("arbitrary",)),
        name="mla_prep",
    )(proj, cos_t, sin_t, qan, kvan, wq, wkv, gq, gk, qw, kw, krw)


ATT_TILE = 512


def _attn_kernel(q_ref, k_ref, v_ref, o_ref):
    seq = q_ref.shape[0]
    tq = ATT_TILE
    hq = tq // 2
    lane = lax.broadcasted_iota(jnp.int32, (tq, HEAD_PAD), 1)
    mask_a = (lax.broadcasted_iota(jnp.int32, (tq, hq), 1)
              <= lax.broadcasted_iota(jnp.int32, (tq, hq), 0))
    mask_b = (lax.broadcasted_iota(jnp.int32, (hq, hq), 1)
              <= lax.broadcasted_iota(jnp.int32, (hq, hq), 0))
    nt = (((1,), (1,)), ((), ()))
    heads = (slice(0, HEAD_PAD), slice(HEAD_PAD, 2 * HEAD_PAD))

    def block(state, q, k0, width, hs, mask):
        m, acc = state
        s = lax.dot_general(q, k_ref[k0:k0 + width, hs], nt, preferred_element_type=F32)
        if mask is not None:
            s = jnp.where(mask, s, NEG_BIG)
        m_new = jnp.maximum(m, jnp.max(s, axis=-1, keepdims=True))
        p = jnp.exp((s - m_new).astype(BF16))
        acc = jnp.exp(m - m_new) * acc + jnp.dot(p, v_ref[k0:k0 + width, hs],
                                                 preferred_element_type=F32)
        return m_new, acc

    for qi in range(seq // tq):
        q0 = qi * tq
        qs = [q_ref[q0:q0 + tq, hs] for hs in heads]
        init = (jnp.full((tq, 1), NEG_BIG, F32), jnp.zeros((tq, HEAD_PAD), F32))
        states = [init, init]
        for j in range(qi):
            states = [block(st, q, j * tq, tq, hs, None) for st, q, hs in zip(states, qs, heads)]
        states = [block(st, q, q0, hq, hs, mask_a) for st, q, hs in zip(states, qs, heads)]
        outs = []
        for (m, acc), q, hs in zip(states, qs, heads):
            m_lo, acc_lo = block((m[hq:], acc[hq:]), q[hq:], q0 + hq, hq, hs, mask_b)
            acc = jnp.concatenate([acc[:hq], acc_lo], axis=0)
            denom = jnp.where(lane < V_HEAD, pltpu.roll(acc, V_HEAD, 1), 1.0)
            outs.append(acc / denom)
        o_ref[q0:q0 + tq, :] = jnp.where(lane < V_HEAD, outs[0],
                                         pltpu.roll(outs[1], V_HEAD, 1)).astype(BF16)


def _attention(q, k, v, batch, seq):
    t = q.shape[0]
    pairs = MLA_HEADS // 2
    blk = pl.BlockSpec((seq, 2 * HEAD_PAD), lambda b, p: (b, p))
    return pl.pallas_call(
        _attn_kernel,
        grid=(batch, pairs),
        in_specs=[blk, blk, blk],
        out_specs=pl.BlockSpec((seq, 2 * V_HEAD), lambda b, p: (b, p)),
        out_shape=jax.ShapeDtypeStruct((t, MLA_OUT), BF16),
        compiler_params=_cparams(("arbitrary", "arbitrary")),
        name="mla_attention",
    )(q, k, v)


POOL_LEVELS = 4


def _pool_kernel(u_ref, pw_ref, ps_ref, o_ref, car_ref):
    ts = u_ref.shape[0]
    j = pl.program_id(1)

    @pl.when(j == 0)
    def _():
        car_ref[...] = jnp.zeros(car_ref.shape, F32)

    t_pos = j * ts + lax.broadcasted_iota(jnp.int32, (ts, 1), 0)
    for g, w in enumerate(POOL_WINDOWS):
        gs = slice(g * POOL_GROUP, (g + 1) * POOL_GROUP)
        ug = u_ref[:, gs].astype(F32)
        acc = ug
        span = 1
        level = 0
        while span < w:
            prev = car_ref[level, :, gs]
            car_ref[level, :, gs] = acc[ts - SUBLANES:, :]
            acc = acc + _shift_rows(acc, prev, span)
            span *= 2
            level += 1
        count = jnp.minimum(t_pos + 1, w).astype(F32)
        pooled = acc / count - ug
        mixed = jnp.dot(pooled.astype(BF16), pw_ref[0, g], preferred_element_type=F32)
        o_ref[:, gs] = (mixed * ps_ref[0, :, gs]).astype(BF16)


def _pool(proj, pool_w, pool_scale, layer, batch, seq):
    t = proj.shape[0]
    ts = 512
    per_seq = seq // ts
    return pl.pallas_call(
        _pool_kernel,
        grid=(batch, per_seq),
        in_specs=[pl.BlockSpec((ts, POOL_DIM), lambda b, j: (b * per_seq + j, P_POOL // POOL_DIM)),
                  _layer_block(pool_w, layer), _layer_block(pool_scale, layer)],
        out_specs=pl.BlockSpec((ts, POOL_DIM), lambda b, j: (b * per_seq + j, 0)),
        out_shape=jax.ShapeDtypeStruct((t, POOL_DIM), BF16),
        scratch_shapes=[pltpu.VMEM((POOL_LEVELS, SUBLANES, POOL_DIM), F32)],
        compiler_params=_cparams(("arbitrary", "arbitrary")),
        name="pool_branch",
    )(proj, pool_w, pool_scale)


SSD_STEP_CHUNKS = 4


def _ssd_kernel(zs_ref, xs_ref, bc_ref, dt_ref, dtb_ref, alog_ref, dful_ref, nw_ref,
                exp_ref, tri_ref, o_ref, state_ref, gated_ref):
    lc = SSD_CHUNK

    @pl.when(pl.program_id(1) == 0)
    def _():
        state_ref[...] = jnp.zeros(state_ref.shape, F32)

    rows = lax.broadcasted_iota(jnp.int32, (lc, lc), 0)
    cols = lax.broadcasted_iota(jnp.int32, (lc, lc), 1)
    causal = cols <= rows
    lane = lax.broadcasted_iota(jnp.int32, (lc, 2 * SSD_HEAD_DIM), 1)
    nt = (((1,), (1,)), ((), ()))
    tn = (((0,), (0,)), ((), ()))
    heads_per_group = SSD_HEADS // SSD_GROUPS
    neg_a = -jnp.exp(alog_ref[0])
    dtb = dtb_ref[0]
    dful = dful_ref[0]
    nw = nw_ref[0]

    for r0 in range(0, xs_ref.shape[0], lc):
        rs = slice(r0, r0 + lc)
        xs = _silu(xs_ref[rs, :].astype(F32))
        bc = _silu(bc_ref[rs, :].astype(F32)).astype(BF16)
        bm = bc[:, :SSD_GROUPS * SSD_STATE]
        cm = bc[:, SSD_GROUPS * SSD_STATE:]

        v = dt_ref[rs, :] + dtb
        dt = jnp.maximum(v, 0.0) + jnp.log(1.0 + jnp.exp(-jnp.abs(v)))
        a_cs = _split_dot_left(tri_ref[...], dt * neg_a)
        a_last = a_cs[lc - 1:lc, :]
        ea = jnp.exp(a_cs)
        stacked = jnp.concatenate([dt, ea, jnp.exp(a_last - a_cs)], axis=0).astype(BF16)
        full = jnp.dot(stacked, exp_ref[...], preferred_element_type=F32)
        dt_full = full[0:lc]
        ea_full = full[lc:2 * lc]
        ds_full = full[2 * lc:3 * lc]
        chunk_decay = _split_dot(ea[lc - SUBLANES:], exp_ref[...])[SUBLANES - 1:SUBLANES, :]
        a_cs_t = a_cs.T

        xdt = xs * dt_full
        xdt_b = xdt.astype(BF16)
        xds_b = (xdt * ds_full).astype(BF16)

        for g in range(SSD_GROUPS):
            gs = slice(g * GROUP_W, (g + 1) * GROUP_W)
            bg = bm[:, g * SSD_STATE:(g + 1) * SSD_STATE]
            cg = cm[:, g * SSD_STATE:(g + 1) * SSD_STATE]
            cb = lax.dot_general(cg, bg, nt, preferred_element_type=F32)
            state = state_ref[g]
            y_off = jnp.dot(cg, state.astype(BF16), preferred_element_type=F32) * ea_full[:, gs]
            s_new = lax.dot_general(bg, xds_b[:, gs], tn, preferred_element_type=F32)
            state_ref[g] = state * chunk_decay[:, gs] + s_new
            for pp in range(heads_per_group // 2):
                h0 = g * heads_per_group + 2 * pp
                ps = slice(h0 * SSD_HEAD_DIM, (h0 + 2) * SSD_HEAD_DIM)
                ys = []
                for hd in (h0, h0 + 1):
                    seg = a_cs[:, hd:hd + 1] - a_cs_t[hd:hd + 1, :]
                    decay = jnp.exp(jnp.where(causal, seg, NEG_BIG))
                    ys.append(jnp.dot((cb * decay).astype(BF16), xdt_b[:, ps],
                                      preferred_element_type=F32))
                y_pair = jnp.where(lane < SSD_HEAD_DIM, ys[0], ys[1])
                ls = slice(2 * pp * SSD_HEAD_DIM, (2 * pp + 2) * SSD_HEAD_DIM)
                y = y_pair + y_off[:, ls] + xs[:, ps] * dful[:, ps]
                gated_ref[rs, ps] = y * zs_ref[rs, ps].astype(F32)

        for g in range(SSD_GROUPS):
            gs = slice(g * GROUP_W, (g + 1) * GROUP_W)
            gt = gated_ref[rs, gs]
            ms = jnp.mean(gt * gt, axis=-1, keepdims=True)
            o_ref[rs, gs] = (gt * lax.rsqrt(ms + EPS) * nw[:, gs]).astype(o_ref.dtype)


def _ssd(proj, dt, dtb, alog, dful, nw, expand, tri, layer, batch, seq):
    t = proj.shape[0]
    lc = SSD_STEP_CHUNKS * SSD_CHUNK
    nc = seq // lc
    lb = lambda a: _layer_block(a, layer)
    return pl.pallas_call(
        _ssd_kernel,
        grid=(batch, nc),
        in_specs=[pl.BlockSpec((lc, SSD_INNER), lambda b, j: (b * nc + j, P_Z // SSD_INNER)),
                  pl.BlockSpec((lc, SSD_INNER), lambda b, j: (b * nc + j, P_XS // SSD_INNER)),
                  pl.BlockSpec((lc, BC_W), lambda b, j: (b * nc + j, P_BC // BC_W)),
                  pl.BlockSpec((lc, DT_PAD), lambda b, j: (b * nc + j, 0)),
                  lb(dtb), lb(alog), lb(dful), lb(nw),
                  _const_block(expand), _const_block(tri)],
        out_specs=pl.BlockSpec((lc, SSD_INNER), lambda b, j: (b * nc + j, 0)),
        out_shape=jax.ShapeDtypeStruct((t, SSD_INNER), BF16),
        scratch_shapes=[pltpu.VMEM((SSD_GROUPS, SSD_STATE, GROUP_W), F32),
                        pltpu.VMEM((lc, SSD_INNER), F32)],
        compiler_params=_cparams(("arbitrary", "arbitrary")),
        name="ssd_branch",
    )(proj, proj, proj, dt, dtb, alog, dful, nw, expand, tri)


def _merge_kernel(x_ref, ga_ref, gb_ref, gc_ref, oa_ref, ob_ref, oc_ref, mod_ref,
                  wb_ref, wo_ref, nw_ref, x_out, h_out):
    m = mod_ref[0, 0]
    ya = jnp.dot(oa_ref[...], wb_ref[0, 0:MLA_OUT, :], preferred_element_type=F32)
    yb = jnp.dot(ob_ref[...], wb_ref[0, MLA_OUT:MLA_OUT + POOL_DIM, :], preferred_element_type=F32)
    yc = jnp.dot(oc_ref[...], wb_ref[0, MLA_OUT + POOL_DIM:, :], preferred_element_type=F32)
    merged = (_sigmoid(ga_ref[...].astype(F32)) * ya
              + _sigmoid(gb_ref[...].astype(F32)) * yb
              + _sigmoid(gc_ref[...].astype(F32)) * yc)
    y = jnp.dot(merged.astype(BF16), wo_ref[0], preferred_element_type=F32)
    x1 = x_ref[...] + m[2:3] * y
    x_out[...] = x1
    h_out[...] = _norm_mod(x1, nw_ref[0], m[3:4], m[4:5]).astype(BF16)


def _merge(xf, proj, o_a, o_b, o_c, mods, wb, wo, norm_w, layer, seq):
    t, d = xf.shape
    tm = 512
    per_seq = seq // tm
    gate = lambda k: pl.BlockSpec((tm, d), lambda i: (i, P_GATE // d + k))
    tile = lambda n: pl.BlockSpec((tm, n), lambda i: (i, 0))
    return pl.pallas_call(
        _merge_kernel,
        grid=(t // tm,),
        in_specs=[tile(d), gate(0), gate(1), gate(2), tile(MLA_OUT), tile(POOL_DIM), tile(SSD_INNER),
                  pl.BlockSpec((1, 1, 6, d), lambda i: (layer, i // per_seq, 0, 0)),
                  _layer_block(wb, layer), _layer_block(wo, layer), _layer_block(norm_w, layer)],
        out_specs=[tile(d), tile(d)],
        out_shape=[jax.ShapeDtypeStruct((t, d), F32), jax.ShapeDtypeStruct((t, d), BF16)],
        compiler_params=_cparams(("arbitrary",)),
        name="merge_out",
    )(xf, proj, proj, proj, o_a, o_b, o_c, mods, wb, wo, norm_w)


def _ffn_kernel(with_next, h_ref, x_ref, mod_ref, *rest):
    if with_next:
        modn_ref, nwn_ref, upw_ref, cw_ref, cb_ref, dw_ref, x_out, h_out, carry_ref, act_ref = rest
    else:
        upw_ref, cw_ref, cb_ref, dw_ref, x_out, carry_ref, act_ref = rest
    ts = h_ref.shape[0]

    @pl.when(pl.program_id(1) == 0)
    def _():
        carry_ref[...] = jnp.zeros(carry_ref.shape, F32)

    h = h_ref[...]

    def conv_half(cs):
        u = jnp.dot(h, upw_ref[0, :, cs], preferred_element_type=F32)
        prev = carry_ref[:, cs]
        carry_ref[:, cs] = u[ts - CONV_HALO:, :]
        return _causal_conv(u, prev, cw_ref[0, :, cs], cb_ref[0, :, cs], FFN_CONV)

    for c in range(FFN_NCHUNK):
        gate = conv_half(slice(c * FFN_CHUNK, (c + 1) * FFN_CHUNK))
        val = conv_half(slice(FFN_DIM + c * FFN_CHUNK, FFN_DIM + (c + 1) * FFN_CHUNK))
        act_ref[c] = (_silu(gate) * val).astype(BF16)
    y = jnp.dot(act_ref[0], dw_ref[0, 0:FFN_CHUNK, :], preferred_element_type=F32)
    for c in range(1, FFN_NCHUNK):
        y = y + jnp.dot(act_ref[c], dw_ref[0, c * FFN_CHUNK:(c + 1) * FFN_CHUNK, :],
                        preferred_element_type=F32)
    m = mod_ref[0, 0]
    x2 = x_ref[...] + m[5:6] * y
    x_out[...] = x2
    if with_next:
        mn = modn_ref[0, 0]
        h_out[...] = _norm_mod(x2, nwn_ref[0], mn[0:1], mn[1:2]).astype(BF16)


def _ffn(h2, x1, mods, upw, cw, cb, dw, norm1_w, layer, seq):
    t, d = x1.shape
    ts = 512
    per_seq = seq // ts
    batch = t // seq
    with_next = layer + 1 < mods.shape[0]
    tile = pl.BlockSpec((ts, d), lambda b, j: (b * per_seq + j, 0))
    modspec = lambda l: pl.BlockSpec((1, 1, 6, d), lambda b, j: (l, b, 0, 0))
    in_specs = [tile, tile, modspec(layer)]
    args = [h2, x1, mods]
    out_specs = [tile]
    out_shape = [jax.ShapeDtypeStruct((t, d), F32)]
    if with_next:
        in_specs += [modspec(layer + 1), _layer_block(norm1_w, layer + 1)]
        args += [mods, norm1_w]
        out_specs.append(tile)
        out_shape.append(jax.ShapeDtypeStruct((t, d), BF16))
    in_specs += [_layer_block(a, layer) for a in (upw, cw, cb, dw)]
    args += [upw, cw, cb, dw]
    outs = pl.pallas_call(
        functools.partial(_ffn_kernel, with_next),
        grid=(batch, per_seq),
        in_specs=in_specs,
        out_specs=out_specs,
        out_shape=out_shape,
        scratch_shapes=[pltpu.VMEM((CONV_HALO, 2 * FFN_DIM), F32),
                        pltpu.VMEM((FFN_NCHUNK, ts, FFN_CHUNK), BF16)],
        compiler_params=_cparams(("arbitrary", "arbitrary")),
        name="conv_mlp",
    )(*args)
    return (outs[0], outs[1]) if with_next else (outs[0], None)


def _lane_window(w, src0, width, dst_lane):
    base = src0 // HEAD_PAD * HEAD_PAD
    blk = w[:, base:base + HEAD_PAD]
    blk = pltpu.roll(blk, (dst_lane - (src0 - base)) % HEAD_PAD, 1)
    lane = lax.broadcasted_iota(jnp.int32, blk.shape, 1)
    return jnp.where((lane >= dst_lane) & (lane < dst_lane + width), blk, 0.0)


def _relayout_kernel(pieces, tail_pieces, w_ref, o_ref):
    w = w_ref[0]
    for src0, width, dst0 in pieces:
        o_ref[0, :, dst0:dst0 + width] = w[:, src0:src0 + width].astype(BF16)
    tail = None
    for src0, width, dst_lane in tail_pieces:
        part = _lane_window(w, src0, width, dst_lane)
        tail = part if tail is None else tail + part
    o_ref[0, :, PROJ_W - HEAD_PAD:] = tail.astype(BF16)


def _relayout_w_in(w_in, pieces, tail_pieces):
    depth, d, n_in = w_in.shape
    tk = 128
    return pl.pallas_call(
        functools.partial(_relayout_kernel, pieces, tail_pieces),
        grid=(depth, d // tk),
        in_specs=[pl.BlockSpec((1, tk, n_in), lambda l, i: (l, i, 0))],
        out_specs=pl.BlockSpec((1, tk, PROJ_W), lambda l, i: (l, i, 0)),
        out_shape=jax.ShapeDtypeStruct((depth, d, PROJ_W), BF16),
        compiler_params=_cparams(("arbitrary", "arbitrary")),
        name="w_in_relayout",
    )(w_in)


def _head_blocks(w, per_head, lo, hi):
    depth, kdim, _ = w.shape
    w = w.reshape(depth, kdim, MLA_HEADS, per_head)[..., lo:hi]
    w = jnp.pad(w, ((0, 0), (0, 0), (0, 0), (0, HEAD_PAD - (hi - lo))))
    return w.reshape(depth, kdim, MLA_HEADS * HEAD_PAD)


def _group_mean_matrix(sizes):
    blk = jnp.zeros((HEAD_PAD, HEAD_PAD), F32)
    start = 0
    for n in sizes:
        blk = blk.at[start:start + n, start:start + n].set(1.0 / n)
        start += n
    z = jnp.zeros_like(blk)
    return jnp.concatenate([jnp.concatenate([blk, z], 1), jnp.concatenate([z, blk], 1)], 0).astype(BF16)


def _rows(v, pad_to=None):
    if pad_to is not None:
        v = jnp.pad(v, ((0, 0), (0, pad_to - v.shape[1])))
    return v[:, None, :]


def kernel(x, c, positions, ada_w, ada_b, norm1_w, w_in, q_a_norm, w_q_b, kv_a_norm, w_kv_b, q_norm, k_norm, pool_w, pool_scale, ssd_conv_w, ssd_conv_b, ssd_dt_bias, ssd_a_log, ssd_d, ssd_norm_w, w_branch, w_out, norm2_w, ffn_up, ffn_conv_w, ffn_conv_b, ffn_down):
    batch, seq, d = x.shape
    depth = ada_w.shape[0]
    t = batch * seq
    xf = x.reshape(t, d)

    mods = _mods(c, ada_w, ada_b).reshape(depth, batch, 6, d)
    cos_t, sin_t = _rope_tables(positions)

    gq = _group_mean_matrix((QK_NOPE, QK_ROPE))
    gk = _group_mean_matrix((QK_NOPE,))
    head_of_col = jnp.arange(SSD_INNER) // SSD_HEAD_DIM
    expand = (jnp.arange(DT_PAD)[:, None] == head_of_col[None, :]).astype(BF16)
    tri = (jnp.arange(SSD_CHUNK)[None, :] <= jnp.arange(SSD_CHUNK)[:, None]).astype(BF16)

    o = 0
    offs = {}
    for name, size in (("q_lat", Q_LORA), ("c_kv", KV_LORA), ("k_rope", QK_ROPE), ("pool", POOL_DIM),
                       ("z", SSD_INNER), ("xs", SSD_INNER), ("bc", BC_W),
                       ("dt", SSD_HEADS), ("gate", N_BRANCH * D_MODEL)):
        offs[name] = (o, o + size)
        o += size
    pieces = []
    for name, dst0 in (("z", P_Z), ("xs", P_XS), ("gate", P_GATE), ("pool", P_POOL), ("bc", P_BC),
                       ("q_lat", P_LAT), ("c_kv", P_LAT + Q_LORA)):
        pieces.append((offs[name][0], offs[name][1] - offs[name][0], dst0))
    tail_pieces = ((offs["dt"][0], SSD_HEADS, 0), (offs["k_rope"][0], QK_ROPE, QK_NOPE))
    w_main = _relayout_w_in(w_in, tuple(pieces), tail_pieces)

    wq = _head_blocks(w_q_b, QK_HEAD, 0, QK_HEAD).astype(BF16)
    wkv = jnp.concatenate([_head_blocks(w_kv_b, QK_NOPE + V_HEAD, 0, QK_NOPE),
                           _head_blocks(w_kv_b, QK_NOPE + V_HEAD, QK_NOPE, QK_NOPE + V_HEAD)],
                          axis=2).astype(BF16)
    hw = MLA_HEADS * HEAD_PAD
    qw = jnp.tile(_rows(q_norm, HEAD_PAD), (1, 1, MLA_HEADS)) * (QK_HEAD ** -0.5)
    kw = jnp.tile(_rows(k_norm[:, :QK_NOPE], HEAD_PAD), (1, 1, MLA_HEADS))
    krw = _rows(jnp.pad(k_norm[:, QK_NOPE:], ((0, 0), (QK_NOPE, 0))), HEAD_PAD)
    qan = _rows(q_a_norm)
    kvan = _rows(kv_a_norm)

    pool_wb = pool_w.astype(BF16)
    pool_sc = _rows(pool_scale)
    conv_b = _rows(ssd_conv_b)
    dtb = _rows(ssd_dt_bias, DT_PAD)
    alog = _rows(ssd_a_log, DT_PAD)
    dful = _rows(jnp.repeat(ssd_d, SSD_HEAD_DIM, axis=1))
    ssd_nw = _rows(ssd_norm_w)
    wb = w_branch.astype(BF16)
    wo = w_out.astype(BF16)
    n1 = _rows(norm1_w)
    n2 = _rows(norm2_w)
    upw = ffn_up.astype(BF16)
    dw = ffn_down.astype(BF16)
    fcb = _rows(ffn_conv_b)

    h = _prenorm(xf, mods, n1, 0, seq)
    for l in range(depth):
        proj, dt = _inproj(h, w_main, ssd_conv_w, conv_b, l, batch, seq)
        q, k, v = _mla_prep(proj, cos_t, sin_t, qan, kvan, wq, wkv, gq, gk, qw, kw, krw, l)
        o_a = _attention(q, k, v, batch, seq)
        o_b = _pool(proj, pool_wb, pool_sc, l, batch, seq)
        o_c = _ssd(proj, dt, dtb, alog, dful, ssd_nw, expand, tri, l, batch, seq)
        x1, h2 = _merge(xf, proj, o_a, o_b, o_c, mods, wb, wo, n2, l, seq)
        xf, h = _ffn(h2, x1, mods, upw, ffn_conv_w, fcb, dw, n1, l, seq)
    return xf.reshape(batch, seq, d)
```

```python
import functools

import jax
import jax.numpy as jnp
from jax import lax
from jax.experimental import pallas as pl
from jax.experimental.pallas import tpu as pltpu

F32 = jnp.float32
BF16 = jnp.bfloat16

D_MODEL = 1024
MLA_HEADS = 8
QK_NOPE = 64
QK_ROPE = 32
QK_HEAD = QK_NOPE + QK_ROPE
V_HEAD = 64
Q_LORA = 384
KV_LORA = 256
ROPE_THETA = 10000.0
MLA_OUT = MLA_HEADS * V_HEAD
POOL_WINDOWS = (2, 4, 8, 16)
POOL_GROUP = 128
POOL_DIM = len(POOL_WINDOWS) * POOL_GROUP
SSD_HEADS = 16
SSD_HEAD_DIM = 64
SSD_INNER = SSD_HEADS * SSD_HEAD_DIM
SSD_GROUPS = 2
SSD_STATE = 128
SSD_CONV = 4
SSD_CHUNK = 128
FFN_DIM = 2816
FFN_CONV = 3
N_BRANCH = 3
EPS = 1e-6

SUBLANES = 8
HEAD_PAD = 128
NEG_BIG = -1e30
CONV_HALO = SUBLANES

BC_W = 2 * SSD_GROUPS * SSD_STATE
GROUP_W = SSD_INNER // SSD_GROUPS
P_Z = 0
P_XS = P_Z + SSD_INNER
P_GATE = P_XS + SSD_INNER
P_POOL = P_GATE + N_BRANCH * D_MODEL
P_BC = P_POOL + POOL_DIM
P_LAT = P_BC + BC_W
LAT_W = Q_LORA + KV_LORA + HEAD_PAD
PROJ_W = P_LAT + LAT_W
DT_PAD = HEAD_PAD

FFN_CHUNK = 256
FFN_NCHUNK = FFN_DIM // FFN_CHUNK

VMEM_LIMIT = 56 * 1024 * 1024


def _cparams(sem):
    return pltpu.CompilerParams(dimension_semantics=sem, vmem_limit_bytes=VMEM_LIMIT)


def _const_block(a):
    nd = a.ndim
    return pl.BlockSpec(a.shape, lambda *_: (0,) * nd, pipeline_mode=pl.Buffered(1))


def _layer_block(a, layer):
    nd = a.ndim
    return pl.BlockSpec((1,) + a.shape[1:], lambda *_: (layer,) + (0,) * (nd - 1),
                        pipeline_mode=pl.Buffered(1))


def _sigmoid(v):
    return 0.5 * jnp.tanh(0.5 * v) + 0.5


def _silu(v):
    half = 0.5 * v
    return half * jnp.tanh(half) + half


def _norm_mod(x, w, shift, scale):
    y = x * lax.rsqrt(jnp.mean(x * x, axis=-1, keepdims=True) + EPS)
    return (y * w) * (1.0 + scale) + shift


def _shift_rows(u, prev, k):
    full = pltpu.roll(u, k, 0)
    row = lax.broadcasted_iota(jnp.int32, prev.shape, 0)
    first = jnp.where(row < k, pltpu.roll(prev, k, 0), full[:SUBLANES])
    return jnp.concatenate([first, full[SUBLANES:]], axis=0)


def _causal_conv(u, prev, w, b, taps):
    out = w[taps - 1:taps, :] * u + b
    for k in range(taps - 1):
        out = out + w[k:k + 1, :] * _shift_rows(u, prev, taps - 1 - k)
    return out


def _split(x):
    hi = x.astype(BF16)
    return hi, (x - hi.astype(F32)).astype(BF16)


def _split_dot(x, m):
    hi, lo = _split(x)
    return jnp.dot(hi, m, preferred_element_type=F32) + jnp.dot(lo, m, preferred_element_type=F32)


def _split_dot_left(m, x):
    hi, lo = _split(x)
    return jnp.dot(m, hi, preferred_element_type=F32) + jnp.dot(m, lo, preferred_element_type=F32)


def _mods_kernel(c_ref, w_ref, b_ref, o_ref):
    ca = _silu(c_ref[...]).astype(BF16)
    w = w_ref[0].astype(BF16)
    o_ref[0] = jnp.dot(ca, w, preferred_element_type=F32) + b_ref[0]


def _mods(c, ada_w, ada_b):
    depth, d, n = ada_w.shape
    b = c.shape[0]
    tn = 1536
    return pl.pallas_call(
        _mods_kernel,
        grid=(depth, n // tn),
        in_specs=[pl.BlockSpec((b, d), lambda l, j: (0, 0)),
                  pl.BlockSpec((1, d, tn), lambda l, j: (l, 0, j)),
                  pl.BlockSpec((1, 1, tn), lambda l, j: (l, 0, j))],
        out_specs=pl.BlockSpec((1, b, tn), lambda l, j: (l, 0, j)),
        out_shape=jax.ShapeDtypeStruct((depth, b, n), F32),
        compiler_params=_cparams(("arbitrary", "arbitrary")),
        name="adaln_mods",
    )(c, ada_w, ada_b.reshape(depth, 1, n))


def _rope_kernel(pos_ref, freq_ref, sign_ref, c_ref, s_ref):
    ang = pos_ref[...] * freq_ref[...]
    c_ref[...] = jnp.cos(ang)
    s_ref[...] = jnp.sin(ang) * sign_ref[...]


def _rope_tables(positions):
    t = positions.size
    tm = 2048
    half = QK_ROPE // 2
    inv_freq = ROPE_THETA ** (-jnp.arange(0, QK_ROPE, 2, dtype=F32) / QK_ROPE)
    zeros = jnp.zeros((QK_NOPE,), F32)
    tail = jnp.zeros((HEAD_PAD - QK_HEAD,), F32)
    freq = jnp.concatenate([zeros, inv_freq, inv_freq, tail]).reshape(1, HEAD_PAD)
    sign = jnp.concatenate([zeros, -jnp.ones((half,), F32), jnp.ones((half,), F32), tail]).reshape(1, HEAD_PAD)
    pos = positions.astype(F32).reshape(t, 1)
    row = pl.BlockSpec((1, HEAD_PAD), lambda i: (0, 0))
    tab = pl.BlockSpec((tm, HEAD_PAD), lambda i: (i, 0))
    return pl.pallas_call(
        _rope_kernel,
        grid=(t // tm,),
        in_specs=[pl.BlockSpec((tm, 1), lambda i: (i, 0)), row, row],
        out_specs=[tab, tab],
        out_shape=[jax.ShapeDtypeStruct((t, HEAD_PAD), F32)] * 2,
        compiler_params=_cparams(("arbitrary",)),
        name="rope_tables",
    )(pos, freq, sign)


def _prenorm_kernel(x_ref, mod_ref, w_ref, h_ref):
    m = mod_ref[0, 0]
    h_ref[...] = _norm_mod(x_ref[...], w_ref[0], m[0:1], m[1:2]).astype(BF16)


def _prenorm(xf, mods, norm_w, layer, seq):
    t, d = xf.shape
    tm = 1024
    per_seq = seq // tm
    return pl.pallas_call(
        _prenorm_kernel,
        grid=(t // tm,),
        in_specs=[pl.BlockSpec((tm, d), lambda i: (i, 0)),
                  pl.BlockSpec((1, 1, 6, d), lambda i: (layer, i // per_seq, 0, 0)),
                  _layer_block(norm_w, layer)],
        out_specs=pl.BlockSpec((tm, d), lambda i: (i, 0)),
        out_shape=jax.ShapeDtypeStruct((t, d), BF16),
        compiler_params=_cparams(("arbitrary",)),
        name="prenorm",
    )(xf, mods, norm_w)


INPROJ_COLS = 512


def _inproj_kernel(h_ref, w_ref, cw_ref, cb_ref, proj_ref, dt_ref, car_ref):
    tm = h_ref.shape[0]
    wc = INPROJ_COLS

    @pl.when(pl.program_id(1) == 0)
    def _():
        car_ref[...] = jnp.zeros(car_ref.shape, F32)

    h = h_ref[...]

    def mm(c0, width):
        return lax.dot_general(h, w_ref[0, c0:c0 + width, :], (((1,), (1,)), ((), ())),
                               preferred_element_type=F32)

    for rel in range(0, SSD_INNER, wc):
        proj_ref[:, P_Z + rel:P_Z + rel + wc] = _silu(mm(P_Z + rel, wc)).astype(BF16)
    for rel in range(0, SSD_INNER + BC_W, wc):
        c0 = P_XS + rel if rel < SSD_INNER else P_BC + rel - SSD_INNER
        cs = slice(rel, rel + wc)
        u = mm(c0, wc)
        prev = car_ref[:, cs]
        car_ref[:, cs] = u[tm - CONV_HALO:, :]
        conv = _causal_conv(u, prev, cw_ref[0, :, cs], cb_ref[0, :, cs], SSD_CONV)
        proj_ref[:, c0:c0 + wc] = conv.astype(BF16)
    for c0 in range(P_GATE, P_BC, wc):
        proj_ref[:, c0:c0 + wc] = mm(c0, wc).astype(BF16)
    lat = mm(P_LAT, LAT_W)
    proj_ref[:, P_LAT:] = lat.astype(BF16)
    dt_ref[...] = lat[:, LAT_W - DT_PAD:]


def _inproj(h, w_main, conv_w, conv_b, layer, batch, seq):
    t, d = h.shape
    tm = 1024
    per_seq = seq // tm
    rows = lambda n: pl.BlockSpec((tm, n), lambda b, j: (b * per_seq + j, 0))
    return pl.pallas_call(
        _inproj_kernel,
        grid=(batch, per_seq),
        in_specs=[rows(d), _layer_block(w_main, layer), _layer_block(conv_w, layer),
                  _layer_block(conv_b, layer)],
        out_specs=[rows(PROJ_W), rows(DT_PAD)],
        out_shape=[jax.ShapeDtypeStruct((t, PROJ_W), BF16),
                   jax.ShapeDtypeStruct((t, DT_PAD), F32)],
        scratch_shapes=[pltpu.VMEM((CONV_HALO, SSD_INNER + BC_W), F32)],
        compiler_params=_cparams(("arbitrary", "arbitrary")),
        name="in_projection",
    )(h, w_main, conv_w, conv_b)


def _rope_apply(y, cos, sin_signed, lane):
    half = QK_ROPE // 2
    partner = jnp.where(lane < QK_NOPE + half,
                        pltpu.roll(y, HEAD_PAD - half, 1),
                        pltpu.roll(y, half, 1))
    return y * cos + partner * sin_signed


def _mla_prep_kernel(lat_ref, cos_ref, sin_ref, qan_ref, kvan_ref, wq_ref, wkv_ref,
                     gq_ref, gk_ref, qw_ref, kw_ref, krw_ref, q_ref, k_ref, v_ref):
    kw = MLA_HEADS * HEAD_PAD
    pair = 2 * HEAD_PAD
    qw = qw_ref[0]
    kwt = kw_ref[0]
    vlane = lax.broadcasted_iota(jnp.int32, (1, kw), 1)
    lane = lax.broadcasted_iota(jnp.int32, (MLA_SUB, HEAD_PAD), 1)

    for r0 in range(0, lat_ref.shape[0], MLA_SUB):
        rs = slice(r0, r0 + MLA_SUB)
        lat = lat_ref[rs, :].astype(F32)
        q_lat = lat[:, :Q_LORA]
        c_kv = lat[:, Q_LORA:Q_LORA + KV_LORA]
        cos = cos_ref[rs, :]
        sin = sin_ref[rs, :]
        k_rope = jnp.where((lane >= QK_NOPE) & (lane < QK_HEAD), lat[:, Q_LORA + KV_LORA:], 0.0)

        qn = q_lat * lax.rsqrt(jnp.mean(q_lat * q_lat, axis=-1, keepdims=True) + EPS) * qan_ref[0]
        cn = c_kv * lax.rsqrt(jnp.mean(c_kv * c_kv, axis=-1, keepdims=True) + EPS) * kvan_ref[0]
        q = jnp.dot(qn.astype(BF16), wq_ref[0], preferred_element_type=F32)
        kv = jnp.dot(cn.astype(BF16), wkv_ref[0], preferred_element_type=F32)
        v_ref[rs, :] = jnp.where(vlane % HEAD_PAD < V_HEAD, kv[:, kw:], 1.0).astype(BF16)

        kr_ms = jnp.sum(k_rope * k_rope, axis=-1, keepdims=True) * (1.0 / QK_ROPE)
        kr = _rope_apply(k_rope * lax.rsqrt(kr_ms + EPS) * krw_ref[0], cos, sin, lane)

        for c0 in range(0, kw, pair):
            xq = q[:, c0:c0 + pair]
            ms = jnp.dot((xq * xq).astype(BF16), gq_ref[...], preferred_element_type=F32)
            yq = xq * lax.rsqrt(ms + EPS) * qw[:, c0:c0 + pair]
            xk = kv[:, c0:c0 + pair]
            ms = jnp.dot((xk * xk).astype(BF16), gk_ref[...], preferred_element_type=F32)
            yk = xk * lax.rsqrt(ms + EPS) * kwt[:, c0:c0 + pair]
            for h0 in range(0, pair, HEAD_PAD):
                q_ref[rs, c0 + h0:c0 + h0 + HEAD_PAD] = _rope_apply(
                    yq[:, h0:h0 + HEAD_PAD], cos, sin, lane).astype(BF16)
                k_ref[rs, c0 + h0:c0 + h0 + HEAD_PAD] = (yk[:, h0:h0 + HEAD_PAD] + kr).astype(BF16)


MLA_SUB = 256


def _mla_prep(proj, cos_t, sin_t, qan, kvan, wq, wkv, gq, gk, qw, kw, krw, layer):
    t = proj.shape[0]
    tm = 1024
    hw = MLA_HEADS * HEAD_PAD
    lb = lambda a: _layer_block(a, layer)
    return pl.pallas_call(
        _mla_prep_kernel,
        grid=(t // tm,),
        in_specs=[pl.BlockSpec((tm, LAT_W), lambda i: (i, P_LAT // LAT_W)),
                  pl.BlockSpec((tm, HEAD_PAD), lambda i: (i, 0)),
                  pl.BlockSpec((tm, HEAD_PAD), lambda i: (i, 0)),
                  lb(qan), lb(kvan), lb(wq), lb(wkv),
                  _const_block(gq), _const_block(gk),
                  lb(qw), lb(kw), lb(krw)],
        out_specs=[pl.BlockSpec((tm, hw), lambda i: (i, 0))] * 3,
        out_shape=[jax.ShapeDtypeStruct((t, hw), BF16)] * 3,
        compiler_params=_cparams(("arbitrary",)),
        name="mla_prep",
    )(proj, cos_t, sin_t, qan, kvan, wq, wkv, gq, gk, qw, kw, krw)


ATT_TILE = 512


def _attn_kernel(q_ref, k_ref, v_ref, o_ref):
    seq = q_ref.shape[0]
    tq = ATT_TILE
    hq = tq // 2
    lane = lax.broadcasted_iota(jnp.int32, (tq, HEAD_PAD), 1)
    mask_a = (lax.broadcasted_iota(jnp.int32, (tq, hq), 1)
              <= lax.broadcasted_iota(jnp.int32, (tq, hq), 0))
    mask_b = (lax.broadcasted_iota(jnp.int32, (hq, hq), 1)
              <= lax.broadcasted_iota(jnp.int32, (hq, hq), 0))
    nt = (((1,), (1,)), ((), ()))
    heads = (slice(0, HEAD_PAD), slice(HEAD_PAD, 2 * HEAD_PAD))

    def block(state, q, k0, width, hs, mask):
        m, acc = state
        s = lax.dot_general(q, k_ref[k0:k0 + width, hs], nt, preferred_element_type=F32)
        if mask is not None:
            s = jnp.where(mask, s, NEG_BIG)
        m_new = jnp.maximum(m, jnp.max(s, axis=-1, keepdims=True))
        p = jnp.exp((s - m_new).astype(BF16))
        acc = jnp.exp(m - m_new) * acc + jnp.dot(p, v_ref[k0:k0 + width, hs],
                                                 preferred_element_type=F32)
        return m_new, acc

    for qi in range(seq // tq):
        q0 = qi * tq
        qs = [q_ref[q0:q0 + tq, hs] for hs in heads]
        init = (jnp.full((tq, 1), NEG_BIG, F32), jnp.zeros((tq, HEAD_PAD), F32))
        states = [init, init]
        for j in range(qi):
            states = [block(st, q, j * tq, tq, hs, None) for st, q, hs in zip(states, qs, heads)]
        states = [block(st, q, q0, hq, hs, mask_a) for st, q, hs in zip(states, qs, heads)]
        outs = []
        for (m, acc), q, hs in zip(states, qs, heads):
            m_lo, acc_lo = block((m[hq:], acc[hq:]), q[hq:], q0 + hq, hq, hs, mask_b)
            acc = jnp.concatenate([acc[:hq], acc_lo], axis=0)
            denom = jnp.where(lane < V_HEAD, pltpu.roll(acc, V_HEAD, 1), 1.0)
            outs.append(acc / denom)
        o_ref[q0:q0 + tq, :] = jnp.where(lane < V_HEAD, outs[0],
                                         pltpu.roll(outs[1], V_HEAD, 1)).astype(BF16)


def _attention(q, k, v, batch, seq):
    t = q.shape[0]
    pairs = MLA_HEADS // 2
    blk = pl.BlockSpec((seq, 2 * HEAD_PAD), lambda b, p: (b, p))
    return pl.pallas_call(
        _attn_kernel,
        grid=(batch, pairs),
        in_specs=[blk, blk, blk],
        out_specs=pl.BlockSpec((seq, 2 * V_HEAD), lambda b, p: (b, p)),
        out_shape=jax.ShapeDtypeStruct((t, MLA_OUT), BF16),
        compiler_params=_cparams(("arbitrary", "arbitrary")),
        name="mla_attention",
    )(q, k, v)


POOL_LEVELS = 4


def _pool_kernel(u_ref, pw_ref, ps_ref, o_ref, car_ref):
    ts = u_ref.shape[0]
    j = pl.program_id(1)

    @pl.when(j == 0)
    def _():
        car_ref[...] = jnp.zeros(car_ref.shape, F32)

    t_pos = j * ts + lax.broadcasted_iota(jnp.int32, (ts, 1), 0)
    for g, w in enumerate(POOL_WINDOWS):
        gs = slice(g * POOL_GROUP, (g + 1) * POOL_GROUP)
        ug = u_ref[:, gs].astype(F32)
        acc = ug
        span = 1
        level = 0
        while span < w:
            prev = car_ref[level, :, gs]
            car_ref[level, :, gs] = acc[ts - SUBLANES:, :]
            acc = acc + _shift_rows(acc, prev, span)
            span *= 2
            level += 1
        count = jnp.minimum(t_pos + 1, w).astype(F32)
        pooled = acc / count - ug
        mixed = jnp.dot(pooled.astype(BF16), pw_ref[0, g], preferred_element_type=F32)
        o_ref[:, gs] = (mixed * ps_ref[0, :, gs]).astype(BF16)


def _pool(proj, pool_w, pool_scale, layer, batch, seq):
    t = proj.shape[0]
    ts = 512
    per_seq = seq // ts
    return pl.pallas_call(
        _pool_kernel,
        grid=(batch, per_seq),
        in_specs=[pl.BlockSpec((ts, POOL_DIM), lambda b, j: (b * per_seq + j, P_POOL // POOL_DIM)),
                  _layer_block(pool_w, layer), _layer_block(pool_scale, layer)],
        out_specs=pl.BlockSpec((ts, POOL_DIM), lambda b, j: (b * per_seq + j, 0)),
        out_shape=jax.ShapeDtypeStruct((t, POOL_DIM), BF16),
        scratch_shapes=[pltpu.VMEM((POOL_LEVELS, SUBLANES, POOL_DIM), F32)],
        compiler_params=_cparams(("arbitrary", "arbitrary")),
        name="pool_branch",
    )(proj, pool_w, pool_scale)


SSD_STEP_CHUNKS = 8


def _ssd_kernel(zs_ref, xs_ref, bc_ref, dt_ref, dtb_ref, alog_ref, dful_ref, nw_ref,
                exp_ref, tri_ref, o_ref, state_ref, gated_ref):
    lc = SSD_CHUNK

    @pl.when(pl.program_id(1) == 0)
    def _():
        state_ref[...] = jnp.zeros(state_ref.shape, F32)

    rows = lax.broadcasted_iota(jnp.int32, (lc, lc), 0)
    cols = lax.broadcasted_iota(jnp.int32, (lc, lc), 1)
    causal = cols <= rows
    lane = lax.broadcasted_iota(jnp.int32, (lc, 2 * SSD_HEAD_DIM), 1)
    nt = (((1,), (1,)), ((), ()))
    tn = (((0,), (0,)), ((), ()))
    heads_per_group = SSD_HEADS // SSD_GROUPS
    neg_a = -jnp.exp(alog_ref[0])
    dtb = dtb_ref[0]
    dful = dful_ref[0]
    nw = nw_ref[0]

    for r0 in range(0, xs_ref.shape[0], lc):
        rs = slice(r0, r0 + lc)
        xs = _silu(xs_ref[rs, :].astype(F32))
        bc = _silu(bc_ref[rs, :].astype(F32)).astype(BF16)
        bm = bc[:, :SSD_GROUPS * SSD_STATE]
        cm = bc[:, SSD_GROUPS * SSD_STATE:]

        v = dt_ref[rs, :] + dtb
        dt = jnp.maximum(v, 0.0) + jnp.log(1.0 + jnp.exp(-jnp.abs(v)))
        a_cs = _split_dot_left(tri_ref[...], dt * neg_a)
        a_last = a_cs[lc - 1:lc, :]
        ea = jnp.exp(a_cs)
        stacked = jnp.concatenate([dt, ea, jnp.exp(a_last - a_cs)], axis=0).astype(BF16)
        full = jnp.dot(stacked, exp_ref[...], preferred_element_type=F32)
        dt_full = full[0:lc]
        ea_full = full[lc:2 * lc]
        ds_full = full[2 * lc:3 * lc]
        chunk_decay = _split_dot(ea[lc - SUBLANES:], exp_ref[...])[SUBLANES - 1:SUBLANES, :]
        a_cs_t = a_cs.T

        xdt = xs * dt_full
        xdt_b = xdt.astype(BF16)
        xds_b = (xdt * ds_full).astype(BF16)

        for g in range(SSD_GROUPS):
            gs = slice(g * GROUP_W, (g + 1) * GROUP_W)
            bg = bm[:, g * SSD_STATE:(g + 1) * SSD_STATE]
            cg = cm[:, g * SSD_STATE:(g + 1) * SSD_STATE]
            cb = lax.dot_general(cg, bg, nt, preferred_element_type=F32)
            state = state_ref[g]
            y_off = jnp.dot(cg, state.astype(BF16), preferred_element_type=F32) * ea_full[:, gs]
            s_new = lax.dot_general(bg, xds_b[:, gs], tn, preferred_element_type=F32)
            state_ref[g] = state * chunk_decay[:, gs] + s_new
            for pp in range(heads_per_group // 2):
                h0 = g * heads_per_group + 2 * pp
                ps = slice(h0 * SSD_HEAD_DIM, (h0 + 2) * SSD_HEAD_DIM)
                ys = []
                for hd in (h0, h0 + 1):
                    seg = a_cs[:, hd:hd + 1] - a_cs_t[hd:hd + 1, :]
                    decay = jnp.exp(jnp.where(causal, seg, NEG_BIG))
                    ys.append(jnp.dot((cb * decay).astype(BF16), xdt_b[:, ps],
                                      preferred_element_type=F32))
                y_pair = jnp.where(lane < SSD_HEAD_DIM, ys[0], ys[1])
                ls = slice(2 * pp * SSD_HEAD_DIM, (2 * pp + 2) * SSD_HEAD_DIM)
                y = y_pair + y_off[:, ls] + xs[:, ps] * dful[:, ps]
                gated_ref[rs, ps] = y * zs_ref[rs, ps].astype(F32)

        for g in range(SSD_GROUPS):
            gs = slice(g * GROUP_W, (g + 1) * GROUP_W)
            gt = gated_ref[rs, gs]
            ms = jnp.mean(gt * gt, axis=-1, keepdims=True)
            o_ref[rs, gs] = (gt * lax.rsqrt(ms + EPS) * nw[:, gs]).astype(o_ref.dtype)


def _ssd(proj, dt, dtb, alog, dful, nw, expand, tri, layer, batch, seq):
    t = proj.shape[0]
    lc = SSD_STEP_CHUNKS * SSD_CHUNK
    nc = seq // lc
    lb = lambda a: _layer_block(a, layer)
    return pl.pallas_call(
        _ssd_kernel,
        grid=(batch, nc),
        in_specs=[pl.BlockSpec((lc, SSD_INNER), lambda b, j: (b * nc + j, P_Z // SSD_INNER)),
                  pl.BlockSpec((lc, SSD_INNER), lambda b, j: (b * nc + j, P_XS // SSD_INNER)),
                  pl.BlockSpec((lc, BC_W), lambda b, j: (b * nc + j, P_BC // BC_W)),
                  pl.BlockSpec((lc, DT_PAD), lambda b, j: (b * nc + j, 0)),
                  lb(dtb), lb(alog), lb(dful), lb(nw),
                  _const_block(expand), _const_block(tri)],
        out_specs=pl.BlockSpec((lc, SSD_INNER), lambda b, j: (b * nc + j, 0)),
        out_shape=jax.ShapeDtypeStruct((t, SSD_INNER), BF16),
        scratch_shapes=[pltpu.VMEM((SSD_GROUPS, SSD_STATE, GROUP_W), F32),
                        pltpu.VMEM((lc, SSD_INNER), F32)],
        compiler_params=_cparams(("arbitrary", "arbitrary")),
        name="ssd_branch",
    )(proj, proj, proj, dt, dtb, alog, dful, nw, expand, tri)


def _merge_kernel(x_ref, ga_ref, gb_ref, gc_ref, oa_ref, ob_ref, oc_ref, mod_ref,
                  wb_ref, wo_ref, nw_ref, x_out, h_out):
    m = mod_ref[0, 0]
    ya = jnp.dot(oa_ref[...], wb_ref[0, 0:MLA_OUT, :], preferred_element_type=F32)
    yb = jnp.dot(ob_ref[...], wb_ref[0, MLA_OUT:MLA_OUT + POOL_DIM, :], preferred_element_type=F32)
    yc = jnp.dot(oc_ref[...], wb_ref[0, MLA_OUT + POOL_DIM:, :], preferred_element_type=F32)
    merged = (_sigmoid(ga_ref[...].astype(F32)) * ya
              + _sigmoid(gb_ref[...].astype(F32)) * yb
              + _sigmoid(gc_ref[...].astype(F32)) * yc)
    y = jnp.dot(merged.astype(BF16), wo_ref[0], preferred_element_type=F32)
    x1 = x_ref[...] + m[2:3] * y
    x_out[...] = x1
    h_out[...] = _norm_mod(x1, nw_ref[0], m[3:4], m[4:5]).astype(BF16)


def _merge(xf, proj, o_a, o_b, o_c, mods, wb, wo, norm_w, layer, seq):
    t, d = xf.shape
    tm = 512
    per_seq = seq // tm
    gate = lambda k: pl.BlockSpec((tm, d), lambda i: (i, P_GATE // d + k))
    tile = lambda n: pl.BlockSpec((tm, n), lambda i: (i, 0))
    return pl.pallas_call(
        _merge_kernel,
        grid=(t // tm,),
        in_specs=[tile(d), gate(0), gate(1), gate(2), tile(MLA_OUT), tile(POOL_DIM), tile(SSD_INNER),
                  pl.BlockSpec((1, 1, 6, d), lambda i: (layer, i // per_seq, 0, 0)),
                  _layer_block(wb, layer), _layer_block(wo, layer), _layer_block(norm_w, layer)],
        out_specs=[tile(d), tile(d)],
        out_shape=[jax.ShapeDtypeStruct((t, d), F32), jax.ShapeDtypeStruct((t, d), BF16)],
        compiler_params=_cparams(("arbitrary",)),
        name="merge_out",
    )(xf, proj, proj, proj, o_a, o_b, o_c, mods, wb, wo, norm_w)


def _ffn_kernel(with_next, h_ref, x_ref, mod_ref, *rest):
    if with_next:
        modn_ref, nwn_ref, upw_ref, cw_ref, cb_ref, dw_ref, x_out, h_out, carry_ref, act_ref = rest
    else:
        upw_ref, cw_ref, cb_ref, dw_ref, x_out, carry_ref, act_ref = rest
    ts = h_ref.shape[0]

    @pl.when(pl.program_id(1) == 0)
    def _():
        carry_ref[...] = jnp.zeros(carry_ref.shape, F32)

    h = h_ref[...]

    def conv_half(cs):
        u = jnp.dot(h, upw_ref[0, :, cs], preferred_element_type=F32)
        prev = carry_ref[:, cs]
        carry_ref[:, cs] = u[ts - CONV_HALO:, :]
        return _causal_conv(u, prev, cw_ref[0, :, cs], cb_ref[0, :, cs], FFN_CONV)

    for c in range(FFN_NCHUNK):
        gate = conv_half(slice(c * FFN_CHUNK, (c + 1) * FFN_CHUNK))
        val = conv_half(slice(FFN_DIM + c * FFN_CHUNK, FFN_DIM + (c + 1) * FFN_CHUNK))
        act_ref[c] = (_silu(gate) * val).astype(BF16)
    y = jnp.dot(act_ref[0], dw_ref[0, 0:FFN_CHUNK, :], preferred_element_type=F32)
    for c in range(1, FFN_NCHUNK):
        y = y + jnp.dot(act_ref[c], dw_ref[0, c * FFN_CHUNK:(c + 1) * FFN_CHUNK, :],
                        preferred_element_type=F32)
    m = mod_ref[0, 0]
    x2 = x_ref[...] + m[5:6] * y
    x_out[...] = x2
    if with_next:
        mn = modn_ref[0, 0]
        h_out[...] = _norm_mod(x2, nwn_ref[0], mn[0:1], mn[1:2]).astype(BF16)


def _ffn(h2, x1, mods, upw, cw, cb, dw, norm1_w, layer, seq):
    t, d = x1.shape
    ts = 512
    per_seq = seq // ts
    batch = t // seq
    with_next = layer + 1 < mods.shape[0]
    tile = pl.BlockSpec((ts, d), lambda b, j: (b * per_seq + j, 0))
    modspec = lambda l: pl.BlockSpec((1, 1, 6, d), lambda b, j: (l, b, 0, 0))
    in_specs = [tile, tile, modspec(layer)]
    args = [h2, x1, mods]
    out_specs = [tile]
    out_shape = [jax.ShapeDtypeStruct((t, d), F32)]
    if with_next:
        in_specs += [modspec(layer + 1), _layer_block(norm1_w, layer + 1)]
        args += [mods, norm1_w]
        out_specs.append(tile)
        out_shape.append(jax.ShapeDtypeStruct((t, d), BF16))
    in_specs += [_layer_block(a, layer) for a in (upw, cw, cb, dw)]
    args += [upw, cw, cb, dw]
    outs = pl.pallas_call(
        functools.partial(_ffn_kernel, with_next),
        grid=(batch, per_seq),
        in_specs=in_specs,
        out_specs=out_specs,
        out_shape=out_shape,
        scratch_shapes=[pltpu.VMEM((CONV_HALO, 2 * FFN_DIM), F32),
                        pltpu.VMEM((FFN_NCHUNK, ts, FFN_CHUNK), BF16)],
        compiler_params=_cparams(("arbitrary", "arbitrary")),
        name="conv_mlp",
    )(*args)
    return (outs[0], outs[1]) if with_next else (outs[0], None)


def _relayout_kernel(pieces, zero_rows, w_ref, o_ref):
    for src0, width, dst0 in pieces:
        o_ref[0, dst0:dst0 + width, :] = w_ref[0, src0:src0 + width, :].astype(BF16)
    for dst0, width in zero_rows:
        o_ref[0, dst0:dst0 + width, :] = jnp.zeros((width, o_ref.shape[2]), BF16)


def _relayout_w_in(w_t, pieces, zero_rows):
    depth, n_in, d = w_t.shape
    tk = 256
    return pl.pallas_call(
        functools.partial(_relayout_kernel, pieces, zero_rows),
        grid=(depth, d // tk),
        in_specs=[pl.BlockSpec((1, n_in, tk), lambda l, i: (l, 0, i))],
        out_specs=pl.BlockSpec((1, PROJ_W, tk), lambda l, i: (l, 0, i)),
        out_shape=jax.ShapeDtypeStruct((depth, PROJ_W, d), BF16),
        compiler_params=_cparams(("arbitrary", "arbitrary")),
        name="w_in_relayout",
    )(w_t)


def _head_blocks(w, per_head, lo, hi):
    depth, kdim, _ = w.shape
    w = w.reshape(depth, kdim, MLA_HEADS, per_head)[..., lo:hi]
    w = jnp.pad(w, ((0, 0), (0, 0), (0, 0), (0, HEAD_PAD - (hi - lo))))
    return w.reshape(depth, kdim, MLA_HEADS * HEAD_PAD)


def _group_mean_matrix(sizes):
    blk = jnp.zeros((HEAD_PAD, HEAD_PAD), F32)
    start = 0
    for n in sizes:
        blk = blk.at[start:start + n, start:start + n].set(1.0 / n)
        start += n
    z = jnp.zeros_like(blk)
    return jnp.concatenate([jnp.concatenate([blk, z], 1), jnp.concatenate([z, blk], 1)], 0).astype(BF16)


def _rows(v, pad_to=None):
    if pad_to is not None:
        v = jnp.pad(v, ((0, 0), (0, pad_to - v.shape[1])))
    return v[:, None, :]


def kernel(x, c, positions, ada_w, ada_b, norm1_w, w_in, q_a_norm, w_q_b, kv_a_norm, w_kv_b, q_norm, k_norm, pool_w, pool_scale, ssd_conv_w, ssd_conv_b, ssd_dt_bias, ssd_a_log, ssd_d, ssd_norm_w, w_branch, w_out, norm2_w, ffn_up, ffn_conv_w, ffn_conv_b, ffn_down):
    batch, seq, d = x.shape
    depth = ada_w.shape[0]
    t = batch * seq
    xf = x.reshape(t, d)

    mods = _mods(c, ada_w, ada_b).reshape(depth, batch, 6, d)
    cos_t, sin_t = _rope_tables(positions)

    gq = _group_mean_matrix((QK_NOPE, QK_ROPE))
    gk = _group_mean_matrix((QK_NOPE,))
    head_of_col = jnp.arange(SSD_INNER) // SSD_HEAD_DIM
    expand = (jnp.arange(DT_PAD)[:, None] == head_of_col[None, :]).astype(BF16)
    tri = (jnp.arange(SSD_CHUNK)[None, :] <= jnp.arange(SSD_CHUNK)[:, None]).astype(BF16)

    o = 0
    offs = {}
    for name, size in (("q_lat", Q_LORA), ("c_kv", KV_LORA), ("k_rope", QK_ROPE), ("pool", POOL_DIM),
                       ("z", SSD_INNER), ("xs", SSD_INNER), ("bc", BC_W),
                       ("dt", SSD_HEADS), ("gate", N_BRANCH * D_MODEL)):
        offs[name] = (o, o + size)
        o += size
    pieces = []
    for name, dst0 in (("z", P_Z), ("xs", P_XS), ("gate", P_GATE), ("pool", P_POOL), ("bc", P_BC),
                       ("q_lat", P_LAT), ("c_kv", P_LAT + Q_LORA)):
        pieces.append((offs[name][0], offs[name][1] - offs[name][0], dst0))
    tail0 = PROJ_W - HEAD_PAD
    pieces.append((offs["dt"][0], SSD_HEADS, tail0))
    pieces.append((offs["k_rope"][0], QK_ROPE, tail0 + QK_NOPE))
    zero_rows = ((tail0 + SSD_HEADS, QK_NOPE - SSD_HEADS), (tail0 + QK_HEAD, HEAD_PAD - QK_HEAD))
    w_main = _relayout_w_in(jnp.swapaxes(w_in, 1, 2), tuple(pieces), zero_rows)

    wq = _head_blocks(w_q_b, QK_HEAD, 0, QK_HEAD).astype(BF16)
    wkv = jnp.concatenate([_head_blocks(w_kv_b, QK_NOPE + V_HEAD, 0, QK_NOPE),
                           _head_blocks(w_kv_b, QK_NOPE + V_HEAD, QK_NOPE, QK_NOPE + V_HEAD)],
                          axis=2).astype(BF16)
    hw = MLA_HEADS * HEAD_PAD
    qw = jnp.tile(_rows(q_norm, HEAD_PAD), (1, 1, MLA_HEADS)) * (QK_HEAD ** -0.5)
    kw = jnp.tile(_rows(k_norm[:, :QK_NOPE], HEAD_PAD), (1, 1, MLA_HEADS))
    krw = _rows(jnp.pad(k_norm[:, QK_NOPE:], ((0, 0), (QK_NOPE, 0))), HEAD_PAD)
    qan = _rows(q_a_norm)
    kvan = _rows(kv_a_norm)

    pool_wb = pool_w.astype(BF16)
    pool_sc = _rows(pool_scale)
    conv_b = _rows(ssd_conv_b)
    dtb = _rows(ssd_dt_bias, DT_PAD)
    alog = _rows(ssd_a_log, DT_PAD)
    dful = _rows(jnp.repeat(ssd_d, SSD_HEAD_DIM, axis=1))
    ssd_nw = _rows(ssd_norm_w)
    wb = w_branch.astype(BF16)
    wo = w_out.astype(BF16)
    n1 = _rows(norm1_w)
    n2 = _rows(norm2_w)
    upw = ffn_up.astype(BF16)
    dw = ffn_down.astype(BF16)
    fcb = _rows(ffn_conv_b)

    h = _prenorm(xf, mods, n1, 0, seq)
    for l in range(depth):
        proj, dt = _inproj(h, w_main, ssd_conv_w, conv_b, l, batch, seq)
        q, k, v = _mla_prep(proj, cos_t, sin_t, qan, kvan, wq, wkv, gq, gk, qw, kw, krw, l)
        o_a = _attention(q, k, v, batch, seq)
        o_b = _pool(proj, pool_wb, pool_sc, l, batch, seq)
        o_c = _ssd(proj, dt, dtb, alog, dful, ssd_nw, expand, tri, l, batch, seq)
        x1, h2 = _merge(xf, proj, o_a, o_b, o_c, mods, wb, wo, n2, l, seq)
        xf, h = _ffn(h2, x1, mods, upw, ffn_conv_w, fcb, dw, n1, l, seq)
    return xf.reshape(batch, seq, d)
```

```python
import functools

import jax
import jax.numpy as jnp
import numpy as np
from jax import lax
from jax.experimental import pallas as pl
from jax.experimental.pallas import tpu as pltpu

F32 = jnp.float32
BF16 = jnp.bfloat16

D_MODEL = 1024
MLA_HEADS = 8
QK_NOPE = 64
QK_ROPE = 32
QK_HEAD = QK_NOPE + QK_ROPE
V_HEAD = 64
Q_LORA = 384
KV_LORA = 256
ROPE_THETA = 10000.0
MLA_OUT = MLA_HEADS * V_HEAD
POOL_WINDOWS = (2, 4, 8, 16)
POOL_GROUP = 128
POOL_DIM = len(POOL_WINDOWS) * POOL_GROUP
SSD_HEADS = 16
SSD_HEAD_DIM = 64
SSD_INNER = SSD_HEADS * SSD_HEAD_DIM
SSD_GROUPS = 2
SSD_STATE = 128
SSD_CONV = 4
SSD_CHUNK = 128
FFN_DIM = 2816
FFN_CONV = 3
N_BRANCH = 3
EPS = 1e-6

SUBLANES = 8
HEAD_PAD = 128
NEG_BIG = -1e30
CONV_HALO = SUBLANES

BC_W = 2 * SSD_GROUPS * SSD_STATE
GROUP_W = SSD_INNER // SSD_GROUPS
P_Z = 0
P_XS = P_Z + SSD_INNER
P_GATE = P_XS + SSD_INNER
P_POOL = P_GATE + N_BRANCH * D_MODEL
P_BC = P_POOL + POOL_DIM
P_LAT = P_BC + BC_W
LAT_W = Q_LORA + KV_LORA + HEAD_PAD
PROJ_W = P_LAT + LAT_W
DT_PAD = HEAD_PAD

FFN_CHUNK = 256
FFN_NCHUNK = FFN_DIM // FFN_CHUNK

VMEM_LIMIT = 56 * 1024 * 1024


def _cparams(sem):
    return pltpu.CompilerParams(dimension_semantics=sem, vmem_limit_bytes=VMEM_LIMIT)


def _const_block(a):
    nd = a.ndim
    return pl.BlockSpec(a.shape, lambda *_: (0,) * nd, pipeline_mode=pl.Buffered(1))


def _layer_block(a, layer):
    nd = a.ndim
    return pl.BlockSpec((1,) + a.shape[1:], lambda *_: (layer,) + (0,) * (nd - 1),
                        pipeline_mode=pl.Buffered(1))


def _sigmoid(v):
    return 0.5 * jnp.tanh(0.5 * v) + 0.5


def _silu(v):
    half = 0.5 * v
    return half * jnp.tanh(half) + half


def _norm_mod(x, w, shift, scale):
    y = x * lax.rsqrt(jnp.mean(x * x, axis=-1, keepdims=True) + EPS)
    return (y * w) * (1.0 + scale) + shift


def _shift_rows(u, prev, k):
    full = pltpu.roll(u, k, 0)
    row = lax.broadcasted_iota(jnp.int32, prev.shape, 0)
    first = jnp.where(row < k, pltpu.roll(prev, k, 0), full[:SUBLANES])
    return jnp.concatenate([first, full[SUBLANES:]], axis=0)


def _causal_conv(u, prev, w, b, taps):
    out = w[taps - 1:taps, :] * u + b
    for k in range(taps - 1):
        out = out + w[k:k + 1, :] * _shift_rows(u, prev, taps - 1 - k)
    return out


def _split(x):
    hi = x.astype(BF16)
    return hi, (x - hi.astype(F32)).astype(BF16)


def _split_dot(x, m):
    hi, lo = _split(x)
    return jnp.dot(hi, m, preferred_element_type=F32) + jnp.dot(lo, m, preferred_element_type=F32)


def _split_dot_left(m, x):
    hi, lo = _split(x)
    return jnp.dot(m, hi, preferred_element_type=F32) + jnp.dot(m, lo, preferred_element_type=F32)


def _mods_kernel(c_ref, w_ref, b_ref, o_ref):
    ca = _silu(c_ref[...]).astype(BF16)
    w = w_ref[0].astype(BF16)
    o_ref[0] = jnp.dot(ca, w, preferred_element_type=F32) + b_ref[0]


def _mods(c, ada_w, ada_b):
    depth, d, n = ada_w.shape
    b = c.shape[0]
    tn = 1536
    return pl.pallas_call(
        _mods_kernel,
        grid=(depth, n // tn),
        in_specs=[pl.BlockSpec((b, d), lambda l, j: (0, 0)),
                  pl.BlockSpec((1, d, tn), lambda l, j: (l, 0, j)),
                  pl.BlockSpec((1, 1, tn), lambda l, j: (l, 0, j))],
        out_specs=pl.BlockSpec((1, b, tn), lambda l, j: (l, 0, j)),
        out_shape=jax.ShapeDtypeStruct((depth, b, n), F32),
        compiler_params=_cparams(("arbitrary", "arbitrary")),
        name="adaln_mods",
    )(c, ada_w, ada_b.reshape(depth, 1, n))


def _rope_kernel(pos_ref, freq_ref, sign_ref, c_ref, s_ref):
    ang = pos_ref[...] * freq_ref[...]
    c_ref[...] = jnp.cos(ang)
    s_ref[...] = jnp.sin(ang) * sign_ref[...]


def _rope_tables(positions):
    t = positions.size
    tm = 2048
    half = QK_ROPE // 2
    inv_freq = ROPE_THETA ** (-jnp.arange(0, QK_ROPE, 2, dtype=F32) / QK_ROPE)
    zeros = jnp.zeros((QK_NOPE,), F32)
    tail = jnp.zeros((HEAD_PAD - QK_HEAD,), F32)
    freq = jnp.concatenate([zeros, inv_freq, inv_freq, tail]).reshape(1, HEAD_PAD)
    sign = jnp.concatenate([zeros, -jnp.ones((half,), F32), jnp.ones((half,), F32), tail]).reshape(1, HEAD_PAD)
    pos = positions.astype(F32).reshape(t, 1)
    row = pl.BlockSpec((1, HEAD_PAD), lambda i: (0, 0))
    tab = pl.BlockSpec((tm, HEAD_PAD), lambda i: (i, 0))
    return pl.pallas_call(
        _rope_kernel,
        grid=(t // tm,),
        in_specs=[pl.BlockSpec((tm, 1), lambda i: (i, 0)), row, row],
        out_specs=[tab, tab],
        out_shape=[jax.ShapeDtypeStruct((t, HEAD_PAD), F32)] * 2,
        compiler_params=_cparams(("arbitrary",)),
        name="rope_tables",
    )(pos, freq, sign)


def _prenorm_kernel(x_ref, mod_ref, w_ref, h_ref):
    m = mod_ref[0, 0]
    h_ref[...] = _norm_mod(x_ref[...], w_ref[0], m[0:1], m[1:2]).astype(BF16)


def _prenorm(xf, mods, norm_w, layer, seq):
    t, d = xf.shape
    tm = 1024
    per_seq = seq // tm
    return pl.pallas_call(
        _prenorm_kernel,
        grid=(t // tm,),
        in_specs=[pl.BlockSpec((tm, d), lambda i: (i, 0)),
                  pl.BlockSpec((1, 1, 6, d), lambda i: (layer, i // per_seq, 0, 0)),
                  _layer_block(norm_w, layer)],
        out_specs=pl.BlockSpec((tm, d), lambda i: (i, 0)),
        out_shape=jax.ShapeDtypeStruct((t, d), BF16),
        compiler_params=_cparams(("arbitrary",)),
        name="prenorm",
    )(xf, mods, norm_w)


INPROJ_COLS = 512


def _inproj_kernel(h_ref, w_ref, cw_ref, cb_ref, proj_ref, dt_ref, car_ref):
    tm = h_ref.shape[0]
    wc = INPROJ_COLS

    @pl.when(pl.program_id(1) == 0)
    def _():
        car_ref[...] = jnp.zeros(car_ref.shape, F32)

    h = h_ref[...]

    def mm(c0, width):
        return lax.dot_general(h, w_ref[0, c0:c0 + width, :], (((1,), (1,)), ((), ())),
                               preferred_element_type=F32)

    for rel in range(0, SSD_INNER, wc):
        proj_ref[:, P_Z + rel:P_Z + rel + wc] = _silu(mm(P_Z + rel, wc)).astype(BF16)
    for rel in range(0, SSD_INNER + BC_W, wc):
        c0 = P_XS + rel if rel < SSD_INNER else P_BC + rel - SSD_INNER
        cs = slice(rel, rel + wc)
        u = mm(c0, wc)
        prev = car_ref[:, cs]
        car_ref[:, cs] = u[tm - CONV_HALO:, :]
        conv = _causal_conv(u, prev, cw_ref[0, :, cs], cb_ref[0, :, cs], SSD_CONV)
        proj_ref[:, c0:c0 + wc] = conv.astype(BF16)
    for c0 in range(P_GATE, P_BC, wc):
        proj_ref[:, c0:c0 + wc] = mm(c0, wc).astype(BF16)
    lat = mm(P_LAT, LAT_W)
    proj_ref[:, P_LAT:] = lat.astype(BF16)
    dt_ref[...] = lat[:, LAT_W - DT_PAD:]


def _inproj(h, w_main, conv_w, conv_b, layer, batch, seq):
    t, d = h.shape
    tm = 1024
    per_seq = seq // tm
    rows = lambda n: pl.BlockSpec((tm, n), lambda b, j: (b * per_seq + j, 0))
    return pl.pallas_call(
        _inproj_kernel,
        grid=(batch, per_seq),
        in_specs=[rows(d), _layer_block(w_main, layer), _layer_block(conv_w, layer),
                  _layer_block(conv_b, layer)],
        out_specs=[rows(PROJ_W), rows(DT_PAD)],
        out_shape=[jax.ShapeDtypeStruct((t, PROJ_W), BF16),
                   jax.ShapeDtypeStruct((t, DT_PAD), F32)],
        scratch_shapes=[pltpu.VMEM((CONV_HALO, SSD_INNER + BC_W), F32)],
        compiler_params=_cparams(("arbitrary", "arbitrary")),
        name="in_projection",
    )(h, w_main, conv_w, conv_b)


def _rope_apply(y, cos, sin_signed, lane):
    half = QK_ROPE // 2
    partner = jnp.where(lane < QK_NOPE + half,
                        pltpu.roll(y, HEAD_PAD - half, 1),
                        pltpu.roll(y, half, 1))
    return y * cos + partner * sin_signed


def _mla_prep_kernel(lat_ref, cos_ref, sin_ref, qan_ref, kvan_ref, wq_ref, wkv_ref,
                     gq_ref, gk_ref, qw_ref, kw_ref, krw_ref, q_ref, k_ref, v_ref):
    kw = MLA_HEADS * HEAD_PAD
    pair = 2 * HEAD_PAD
    qw = qw_ref[0]
    kwt = kw_ref[0]
    vlane = lax.broadcasted_iota(jnp.int32, (1, kw), 1)
    lane = lax.broadcasted_iota(jnp.int32, (MLA_SUB, HEAD_PAD), 1)

    for r0 in range(0, lat_ref.shape[0], MLA_SUB):
        rs = slice(r0, r0 + MLA_SUB)
        lat = lat_ref[rs, :].astype(F32)
        q_lat = lat[:, :Q_LORA]
        c_kv = lat[:, Q_LORA:Q_LORA + KV_LORA]
        cos = cos_ref[rs, :]
        sin = sin_ref[rs, :]
        k_rope = jnp.where((lane >= QK_NOPE) & (lane < QK_HEAD), lat[:, Q_LORA + KV_LORA:], 0.0)

        qn = q_lat * lax.rsqrt(jnp.mean(q_lat * q_lat, axis=-1, keepdims=True) + EPS) * qan_ref[0]
        cn = c_kv * lax.rsqrt(jnp.mean(c_kv * c_kv, axis=-1, keepdims=True) + EPS) * kvan_ref[0]
        q = jnp.dot(qn.astype(BF16), wq_ref[0], preferred_element_type=F32)
        kv = jnp.dot(cn.astype(BF16), wkv_ref[0], preferred_element_type=F32)
        v_ref[rs, :] = jnp.where(vlane % HEAD_PAD < V_HEAD, kv[:, kw:], 1.0).astype(BF16)

        kr_ms = jnp.sum(k_rope * k_rope, axis=-1, keepdims=True) * (1.0 / QK_ROPE)
        kr = _rope_apply(k_rope * lax.rsqrt(kr_ms + EPS) * krw_ref[0], cos, sin, lane)

        for c0 in range(0, kw, pair):
            xq = q[:, c0:c0 + pair]
            ms = jnp.dot((xq * xq).astype(BF16), gq_ref[...], preferred_element_type=F32)
            yq = xq * lax.rsqrt(ms + EPS) * qw[:, c0:c0 + pair]
            xk = kv[:, c0:c0 + pair]
            ms = jnp.dot((xk * xk).astype(BF16), gk_ref[...], preferred_element_type=F32)
            yk = xk * lax.rsqrt(ms + EPS) * kwt[:, c0:c0 + pair]
            for h0 in range(0, pair, HEAD_PAD):
                q_ref[rs, c0 + h0:c0 + h0 + HEAD_PAD] = _rope_apply(
                    yq[:, h0:h0 + HEAD_PAD], cos, sin, lane).astype(BF16)
                k_ref[rs, c0 + h0:c0 + h0 + HEAD_PAD] = (yk[:, h0:h0 + HEAD_PAD] + kr).astype(BF16)


MLA_SUB = 256


def _mla_prep(proj, cos_t, sin_t, qan, kvan, wq, wkv, gq, gk, qw, kw, krw, layer):
    t = proj.shape[0]
    tm = 1024
    hw = MLA_HEADS * HEAD_PAD
    lb = lambda a: _layer_block(a, layer)
    return pl.pallas_call(
        _mla_prep_kernel,
        grid=(t // tm,),
        in_specs=[pl.BlockSpec((tm, LAT_W), lambda i: (i, P_LAT // LAT_W)),
                  pl.BlockSpec((tm, HEAD_PAD), lambda i: (i, 0)),
                  pl.BlockSpec((tm, HEAD_PAD), lambda i: (i, 0)),
                  lb(qan), lb(kvan), lb(wq), lb(wkv),
                  _const_block(gq), _const_block(gk),
                  lb(qw), lb(kw), lb(krw)],
        out_specs=[pl.BlockSpec((tm, hw), lambda i: (i, 0))] * 3,
        out_shape=[jax.ShapeDtypeStruct((t, hw), BF16)] * 3,
        compiler_params=_cparams(("arbitrary",)),
        name="mla_prep",
    )(proj, cos_t, sin_t, qan, kvan, wq, wkv, gq, gk, qw, kw, krw)


ATT_TILE = 512


def _attn_kernel(q_ref, k_ref, v_ref, o_ref):
    seq = q_ref.shape[0]
    tq = ATT_TILE
    hq = tq // 2
    lane = lax.broadcasted_iota(jnp.int32, (tq, HEAD_PAD), 1)
    mask_a = (lax.broadcasted_iota(jnp.int32, (tq, hq), 1)
              <= lax.broadcasted_iota(jnp.int32, (tq, hq), 0))
    mask_b = (lax.broadcasted_iota(jnp.int32, (hq, hq), 1)
              <= lax.broadcasted_iota(jnp.int32, (hq, hq), 0))
    nt = (((1,), (1,)), ((), ()))
    heads = (slice(0, HEAD_PAD), slice(HEAD_PAD, 2 * HEAD_PAD))

    def block(state, q, k0, width, hs, mask):
        m, acc = state
        s = lax.dot_general(q, k_ref[k0:k0 + width, hs], nt, preferred_element_type=F32)
        if mask is not None:
            s = jnp.where(mask, s, NEG_BIG)
        m_new = jnp.maximum(m, jnp.max(s, axis=-1, keepdims=True))
        p = jnp.exp((s - m_new).astype(BF16))
        acc = jnp.exp(m - m_new) * acc + jnp.dot(p, v_ref[k0:k0 + width, hs],
                                                 preferred_element_type=F32)
        return m_new, acc

    for qi in range(seq // tq):
        q0 = qi * tq
        qs = [q_ref[q0:q0 + tq, hs] for hs in heads]
        init = (jnp.full((tq, 1), NEG_BIG, F32), jnp.zeros((tq, HEAD_PAD), F32))
        states = [init, init]
        for j in range(qi):
            states = [block(st, q, j * tq, tq, hs, None) for st, q, hs in zip(states, qs, heads)]
        states = [block(st, q, q0, hq, hs, mask_a) for st, q, hs in zip(states, qs, heads)]
        outs = []
        for (m, acc), q, hs in zip(states, qs, heads):
            m_lo, acc_lo = block((m[hq:], acc[hq:]), q[hq:], q0 + hq, hq, hs, mask_b)
            acc = jnp.concatenate([acc[:hq], acc_lo], axis=0)
            denom = jnp.where(lane < V_HEAD, pltpu.roll(acc, V_HEAD, 1), 1.0)
            outs.append(acc / denom)
        o_ref[q0:q0 + tq, :] = jnp.where(lane < V_HEAD, outs[0],
                                         pltpu.roll(outs[1], V_HEAD, 1)).astype(BF16)


def _attention(q, k, v, batch, seq):
    t = q.shape[0]
    pairs = MLA_HEADS // 2
    blk = pl.BlockSpec((seq, 2 * HEAD_PAD), lambda b, p: (b, p))
    return pl.pallas_call(
        _attn_kernel,
        grid=(batch, pairs),
        in_specs=[blk, blk, blk],
        out_specs=pl.BlockSpec((seq, 2 * V_HEAD), lambda b, p: (b, p)),
        out_shape=jax.ShapeDtypeStruct((t, MLA_OUT), BF16),
        compiler_params=_cparams(("arbitrary", "arbitrary")),
        name="mla_attention",
    )(q, k, v)


POOL_LEVELS = 4


def _pool_kernel(u_ref, pw_ref, ps_ref, o_ref, car_ref):
    ts = u_ref.shape[0]
    j = pl.program_id(1)

    @pl.when(j == 0)
    def _():
        car_ref[...] = jnp.zeros(car_ref.shape, F32)

    t_pos = j * ts + lax.broadcasted_iota(jnp.int32, (ts, 1), 0)
    for g, w in enumerate(POOL_WINDOWS):
        gs = slice(g * POOL_GROUP, (g + 1) * POOL_GROUP)
        ug = u_ref[:, gs].astype(F32)
        acc = ug
        span = 1
        level = 0
        while span < w:
            prev = car_ref[level, :, gs]
            car_ref[level, :, gs] = acc[ts - SUBLANES:, :]
            acc = acc + _shift_rows(acc, prev, span)
            span *= 2
            level += 1
        count = jnp.minimum(t_pos + 1, w).astype(F32)
        pooled = acc / count - ug
        mixed = jnp.dot(pooled.astype(BF16), pw_ref[0, g], preferred_element_type=F32)
        o_ref[:, gs] = (mixed * ps_ref[0, :, gs]).astype(BF16)


def _pool(proj, pool_w, pool_scale, layer, batch, seq):
    t = proj.shape[0]
    ts = 512
    per_seq = seq // ts
    return pl.pallas_call(
        _pool_kernel,
        grid=(batch, per_seq),
        in_specs=[pl.BlockSpec((ts, POOL_DIM), lambda b, j: (b * per_seq + j, P_POOL // POOL_DIM)),
                  _layer_block(pool_w, layer), _layer_block(pool_scale, layer)],
        out_specs=pl.BlockSpec((ts, POOL_DIM), lambda b, j: (b * per_seq + j, 0)),
        out_shape=jax.ShapeDtypeStruct((t, POOL_DIM), BF16),
        scratch_shapes=[pltpu.VMEM((POOL_LEVELS, SUBLANES, POOL_DIM), F32)],
        compiler_params=_cparams(("arbitrary", "arbitrary")),
        name="pool_branch",
    )(proj, pool_w, pool_scale)


SSD_STEP_CHUNKS = 8


def _ssd_kernel(zs_ref, xs_ref, bc_ref, dt_ref, dtb_ref, alog_ref, dful_ref, nw_ref,
                exp_ref, tri_ref, o_ref, state_ref, gated_ref):
    lc = SSD_CHUNK

    @pl.when(pl.program_id(1) == 0)
    def _():
        state_ref[...] = jnp.zeros(state_ref.shape, F32)

    rows = lax.broadcasted_iota(jnp.int32, (lc, lc), 0)
    cols = lax.broadcasted_iota(jnp.int32, (lc, lc), 1)
    causal = cols <= rows
    lane = lax.broadcasted_iota(jnp.int32, (lc, 2 * SSD_HEAD_DIM), 1)
    nt = (((1,), (1,)), ((), ()))
    tn = (((0,), (0,)), ((), ()))
    heads_per_group = SSD_HEADS // SSD_GROUPS
    neg_a = -jnp.exp(alog_ref[0])
    dtb = dtb_ref[0]
    dful = dful_ref[0]
    nw = nw_ref[0]

    for r0 in range(0, xs_ref.shape[0], lc):
        rs = slice(r0, r0 + lc)
        xs = _silu(xs_ref[rs, :].astype(F32))
        bc = _silu(bc_ref[rs, :].astype(F32)).astype(BF16)
        bm = bc[:, :SSD_GROUPS * SSD_STATE]
        cm = bc[:, SSD_GROUPS * SSD_STATE:]

        v = dt_ref[rs, :] + dtb
        dt = jnp.maximum(v, 0.0) + jnp.log(1.0 + jnp.exp(-jnp.abs(v)))
        a_cs = _split_dot_left(tri_ref[...], dt * neg_a)
        a_last = a_cs[lc - 1:lc, :]
        ea = jnp.exp(a_cs)
        stacked = jnp.concatenate([dt, ea, jnp.exp(a_last - a_cs)], axis=0).astype(BF16)
        full = jnp.dot(stacked, exp_ref[...], preferred_element_type=F32)
        dt_full = full[0:lc]
        ea_full = full[lc:2 * lc]
        ds_full = full[2 * lc:3 * lc]
        chunk_decay = _split_dot(ea[lc - SUBLANES:], exp_ref[...])[SUBLANES - 1:SUBLANES, :]
        a_cs_t = a_cs.T

        xdt = xs * dt_full
        xdt_b = xdt.astype(BF16)
        xds_b = (xdt * ds_full).astype(BF16)

        for g in range(SSD_GROUPS):
            gs = slice(g * GROUP_W, (g + 1) * GROUP_W)
            bg = bm[:, g * SSD_STATE:(g + 1) * SSD_STATE]
            cg = cm[:, g * SSD_STATE:(g + 1) * SSD_STATE]
            cb = lax.dot_general(cg, bg, nt, preferred_element_type=F32)
            state = state_ref[g]
            y_off = jnp.dot(cg, state.astype(BF16), preferred_element_type=F32) * ea_full[:, gs]
            s_new = lax.dot_general(bg, xds_b[:, gs], tn, preferred_element_type=F32)
            state_ref[g] = state * chunk_decay[:, gs] + s_new
            for pp in range(heads_per_group // 2):
                h0 = g * heads_per_group + 2 * pp
                ps = slice(h0 * SSD_HEAD_DIM, (h0 + 2) * SSD_HEAD_DIM)
                ys = []
                for hd in (h0, h0 + 1):
                    seg = a_cs[:, hd:hd + 1] - a_cs_t[hd:hd + 1, :]
                    decay = jnp.exp(jnp.where(causal, seg, NEG_BIG))
                    ys.append(jnp.dot((cb * decay).astype(BF16), xdt_b[:, ps],
                                      preferred_element_type=F32))
                y_pair = jnp.where(lane < SSD_HEAD_DIM, ys[0], ys[1])
                ls = slice(2 * pp * SSD_HEAD_DIM, (2 * pp + 2) * SSD_HEAD_DIM)
                y = y_pair + y_off[:, ls] + xs[:, ps] * dful[:, ps]
                gated_ref[rs, ps] = y * zs_ref[rs, ps].astype(F32)

        for g in range(SSD_GROUPS):
            gs = slice(g * GROUP_W, (g + 1) * GROUP_W)
            gt = gated_ref[rs, gs]
            ms = jnp.mean(gt * gt, axis=-1, keepdims=True)
            o_ref[rs, gs] = (gt * lax.rsqrt(ms + EPS) * nw[:, gs]).astype(o_ref.dtype)


def _ssd(proj, dt, dtb, alog, dful, nw, expand, tri, layer, batch, seq):
    t = proj.shape[0]
    lc = SSD_STEP_CHUNKS * SSD_CHUNK
    nc = seq // lc
    lb = lambda a: _layer_block(a, layer)
    return pl.pallas_call(
        _ssd_kernel,
        grid=(batch, nc),
        in_specs=[pl.BlockSpec((lc, SSD_INNER), lambda b, j: (b * nc + j, P_Z // SSD_INNER)),
                  pl.BlockSpec((lc, SSD_INNER), lambda b, j: (b * nc + j, P_XS // SSD_INNER)),
                  pl.BlockSpec((lc, BC_W), lambda b, j: (b * nc + j, P_BC // BC_W)),
                  pl.BlockSpec((lc, DT_PAD), lambda b, j: (b * nc + j, 0)),
                  lb(dtb), lb(alog), lb(dful), lb(nw),
                  _const_block(expand), _const_block(tri)],
        out_specs=pl.BlockSpec((lc, SSD_INNER), lambda b, j: (b * nc + j, 0)),
        out_shape=jax.ShapeDtypeStruct((t, SSD_INNER), BF16),
        scratch_shapes=[pltpu.VMEM((SSD_GROUPS, SSD_STATE, GROUP_W), F32),
                        pltpu.VMEM((lc, SSD_INNER), F32)],
        compiler_params=_cparams(("arbitrary", "arbitrary")),
        name="ssd_branch",
    )(proj, proj, proj, dt, dtb, alog, dful, nw, expand, tri)


def _merge_kernel(x_ref, ga_ref, gb_ref, gc_ref, oa_ref, ob_ref, oc_ref, mod_ref,
                  wb_ref, wo_ref, nw_ref, x_out, h_out):
    m = mod_ref[0, 0]
    ya = jnp.dot(oa_ref[...], wb_ref[0, 0:MLA_OUT, :], preferred_element_type=F32)
    yb = jnp.dot(ob_ref[...], wb_ref[0, MLA_OUT:MLA_OUT + POOL_DIM, :], preferred_element_type=F32)
    yc = jnp.dot(oc_ref[...], wb_ref[0, MLA_OUT + POOL_DIM:, :], preferred_element_type=F32)
    merged = (_sigmoid(ga_ref[...].astype(F32)) * ya
              + _sigmoid(gb_ref[...].astype(F32)) * yb
              + _sigmoid(gc_ref[...].astype(F32)) * yc)
    y = jnp.dot(merged.astype(BF16), wo_ref[0], preferred_element_type=F32)
    x1 = x_ref[...] + m[2:3] * y
    x_out[...] = x1
    h_out[...] = _norm_mod(x1, nw_ref[0], m[3:4], m[4:5]).astype(BF16)


def _merge(xf, proj, o_a, o_b, o_c, mods, wb, wo, norm_w, layer, seq):
    t, d = xf.shape
    tm = 1024
    per_seq = seq // tm
    gate = lambda k: pl.BlockSpec((tm, d), lambda i: (i, P_GATE // d + k))
    tile = lambda n: pl.BlockSpec((tm, n), lambda i: (i, 0))
    return pl.pallas_call(
        _merge_kernel,
        grid=(t // tm,),
        in_specs=[tile(d), gate(0), gate(1), gate(2), tile(MLA_OUT), tile(POOL_DIM), tile(SSD_INNER),
                  pl.BlockSpec((1, 1, 6, d), lambda i: (layer, i // per_seq, 0, 0)),
                  _layer_block(wb, layer), _layer_block(wo, layer), _layer_block(norm_w, layer)],
        out_specs=[tile(d), tile(d)],
        out_shape=[jax.ShapeDtypeStruct((t, d), F32), jax.ShapeDtypeStruct((t, d), BF16)],
        compiler_params=_cparams(("arbitrary",)),
        name="merge_out",
    )(xf, proj, proj, proj, o_a, o_b, o_c, mods, wb, wo, norm_w)


def _ffn_kernel(with_next, h_ref, x_ref, mod_ref, *rest):
    if with_next:
        modn_ref, nwn_ref, upw_ref, cw_ref, cb_ref, dw_ref, x_out, h_out, carry_ref, act_ref = rest
    else:
        upw_ref, cw_ref, cb_ref, dw_ref, x_out, carry_ref, act_ref = rest
    ts = h_ref.shape[0]

    @pl.when(pl.program_id(1) == 0)
    def _():
        carry_ref[...] = jnp.zeros(carry_ref.shape, F32)

    h = h_ref[...]

    def conv_half(cs):
        u = jnp.dot(h, upw_ref[0, :, cs], preferred_element_type=F32)
        prev = carry_ref[:, cs]
        carry_ref[:, cs] = u[ts - CONV_HALO:, :]
        return _causal_conv(u, prev, cw_ref[0, :, cs], cb_ref[0, :, cs], FFN_CONV)

    for c in range(FFN_NCHUNK):
        gate = conv_half(slice(c * FFN_CHUNK, (c + 1) * FFN_CHUNK))
        val = conv_half(slice(FFN_DIM + c * FFN_CHUNK, FFN_DIM + (c + 1) * FFN_CHUNK))
        act_ref[c] = (_silu(gate) * val).astype(BF16)
    y = jnp.dot(act_ref[0], dw_ref[0, 0:FFN_CHUNK, :], preferred_element_type=F32)
    for c in range(1, FFN_NCHUNK):
        y = y + jnp.dot(act_ref[c], dw_ref[0, c * FFN_CHUNK:(c + 1) * FFN_CHUNK, :],
                        preferred_element_type=F32)
    m = mod_ref[0, 0]
    x2 = x_ref[...] + m[5:6] * y
    x_out[...] = x2
    if with_next:
        mn = modn_ref[0, 0]
        h_out[...] = _norm_mod(x2, nwn_ref[0], mn[0:1], mn[1:2]).astype(BF16)


def _ffn(h2, x1, mods, upw, cw, cb, dw, norm1_w, layer, seq):
    t, d = x1.shape
    ts = 512
    per_seq = seq // ts
    batch = t // seq
    with_next = layer + 1 < mods.shape[0]
    tile = pl.BlockSpec((ts, d), lambda b, j: (b * per_seq + j, 0))
    modspec = lambda l: pl.BlockSpec((1, 1, 6, d), lambda b, j: (l, b, 0, 0))
    in_specs = [tile, tile, modspec(layer)]
    args = [h2, x1, mods]
    out_specs = [tile]
    out_shape = [jax.ShapeDtypeStruct((t, d), F32)]
    if with_next:
        in_specs += [modspec(layer + 1), _layer_block(norm1_w, layer + 1)]
        args += [mods, norm1_w]
        out_specs.append(tile)
        out_shape.append(jax.ShapeDtypeStruct((t, d), BF16))
    in_specs += [_layer_block(a, layer) for a in (upw, cw, cb, dw)]
    args += [upw, cw, cb, dw]
    outs = pl.pallas_call(
        functools.partial(_ffn_kernel, with_next),
        grid=(batch, per_seq),
        in_specs=in_specs,
        out_specs=out_specs,
        out_shape=out_shape,
        scratch_shapes=[pltpu.VMEM((CONV_HALO, 2 * FFN_DIM), F32),
                        pltpu.VMEM((FFN_NCHUNK, ts, FFN_CHUNK), BF16)],
        compiler_params=_cparams(("arbitrary", "arbitrary")),
        name="conv_mlp",
    )(*args)
    return (outs[0], outs[1]) if with_next else (outs[0], None)


def _relayout_kernel(pieces, zero_rows, w_ref, o_ref):
    for src0, width, dst0 in pieces:
        o_ref[0, dst0:dst0 + width, :] = w_ref[0, src0:src0 + width, :].astype(BF16)
    for dst0, width in zero_rows:
        o_ref[0, dst0:dst0 + width, :] = jnp.zeros((width, o_ref.shape[2]), BF16)


def _relayout_w_in(w_t, pieces, zero_rows):
    depth, n_in, d = w_t.shape
    tk = 256
    return pl.pallas_call(
        functools.partial(_relayout_kernel, pieces, zero_rows),
        grid=(depth, d // tk),
        in_specs=[pl.BlockSpec((1, n_in, tk), lambda l, i: (l, 0, i))],
        out_specs=pl.BlockSpec((1, PROJ_W, tk), lambda l, i: (l, 0, i)),
        out_shape=jax.ShapeDtypeStruct((depth, PROJ_W, d), BF16),
        compiler_params=_cparams(("arbitrary", "arbitrary")),
        name="w_in_relayout",
    )(w_t)


def _head_blocks(w, per_head, lo, hi):
    depth, kdim, _ = w.shape
    w = w.reshape(depth, kdim, MLA_HEADS, per_head)[..., lo:hi]
    w = jnp.pad(w, ((0, 0), (0, 0), (0, 0), (0, HEAD_PAD - (hi - lo))))
    return w.reshape(depth, kdim, MLA_HEADS * HEAD_PAD)


def _group_mean_matrix(sizes):
    blk = np.zeros((HEAD_PAD, HEAD_PAD), np.float32)
    start = 0
    for n in sizes:
        blk[start:start + n, start:start + n] = 1.0 / n
        start += n
    return jnp.asarray(np.kron(np.eye(2, dtype=np.float32), blk), BF16)


def _rows(v, pad_to=None):
    if pad_to is not None:
        v = jnp.pad(v, ((0, 0), (0, pad_to - v.shape[1])))
    return v[:, None, :]


def kernel(x, c, positions, ada_w, ada_b, norm1_w, w_in, q_a_norm, w_q_b, kv_a_norm, w_kv_b, q_norm, k_norm, pool_w, pool_scale, ssd_conv_w, ssd_conv_b, ssd_dt_bias, ssd_a_log, ssd_d, ssd_norm_w, w_branch, w_out, norm2_w, ffn_up, ffn_conv_w, ffn_conv_b, ffn_down):
    batch, seq, d = x.shape
    depth = ada_w.shape[0]
    t = batch * seq
    xf = x.reshape(t, d)

    mods = _mods(c, ada_w, ada_b).reshape(depth, batch, 6, d)
    cos_t, sin_t = _rope_tables(positions)

    gq = _group_mean_matrix((QK_NOPE, QK_ROPE))
    gk = _group_mean_matrix((QK_NOPE,))
    head_of_col = np.arange(SSD_INNER) // SSD_HEAD_DIM
    expand = jnp.asarray(np.arange(DT_PAD)[:, None] == head_of_col[None, :], BF16)
    tri = jnp.asarray(np.tril(np.ones((SSD_CHUNK, SSD_CHUNK), np.float32)), BF16)

    o = 0
    offs = {}
    for name, size in (("q_lat", Q_LORA), ("c_kv", KV_LORA), ("k_rope", QK_ROPE), ("pool", POOL_DIM),
                       ("z", SSD_INNER), ("xs", SSD_INNER), ("bc", BC_W),
                       ("dt", SSD_HEADS), ("gate", N_BRANCH * D_MODEL)):
        offs[name] = (o, o + size)
        o += size
    pieces = []
    for name, dst0 in (("z", P_Z), ("xs", P_XS), ("gate", P_GATE), ("pool", P_POOL), ("bc", P_BC),
                       ("q_lat", P_LAT), ("c_kv", P_LAT + Q_LORA)):
        pieces.append((offs[name][0], offs[name][1] - offs[name][0], dst0))
    tail0 = PROJ_W - HEAD_PAD
    pieces.append((offs["dt"][0], SSD_HEADS, tail0))
    pieces.append((offs["k_rope"][0], QK_ROPE, tail0 + QK_NOPE))
    zero_rows = ((tail0 + SSD_HEADS, QK_NOPE - SSD_HEADS), (tail0 + QK_HEAD, HEAD_PAD - QK_HEAD))
    w_main = _relayout_w_in(jnp.swapaxes(w_in, 1, 2), tuple(pieces), zero_rows)

    wq = _head_blocks(w_q_b, QK_HEAD, 0, QK_HEAD).astype(BF16)
    wkv = jnp.concatenate([_head_blocks(w_kv_b, QK_NOPE + V_HEAD, 0, QK_NOPE),
                           _head_blocks(w_kv_b, QK_NOPE + V_HEAD, QK_NOPE, QK_NOPE + V_HEAD)],
                          axis=2).astype(BF16)
    hw = MLA_HEADS * HEAD_PAD
    qw = jnp.tile(_rows(q_norm, HEAD_PAD), (1, 1, MLA_HEADS)) * (QK_HEAD ** -0.5)
    kw = jnp.tile(_rows(k_norm[:, :QK_NOPE], HEAD_PAD), (1, 1, MLA_HEADS))
    krw = _rows(jnp.pad(k_norm[:, QK_NOPE:], ((0, 0), (QK_NOPE, 0))), HEAD_PAD)
    qan = _rows(q_a_norm)
    kvan = _rows(kv_a_norm)

    pool_wb = pool_w.astype(BF16)
    pool_sc = _rows(pool_scale)
    conv_b = _rows(ssd_conv_b)
    dtb = _rows(ssd_dt_bias, DT_PAD)
    alog = _rows(ssd_a_log, DT_PAD)
    dful = _rows(jnp.repeat(ssd_d, SSD_HEAD_DIM, axis=1))
    ssd_nw = _rows(ssd_norm_w)
    wb = w_branch.astype(BF16)
    wo = w_out.astype(BF16)
    n1 = _rows(norm1_w)
    n2 = _rows(norm2_w)
    upw = ffn_up.astype(BF16)
    dw = ffn_down.astype(BF16)
    fcb = _rows(ffn_conv_b)

    h = _prenorm(xf, mods, n1, 0, seq)
    for l in range(depth):
        proj, dt = _inproj(h, w_main, ssd_conv_w, conv_b, l, batch, seq)
        q, k, v = _mla_prep(proj, cos_t, sin_t, qan, kvan, wq, wkv, gq, gk, qw, kw, krw, l)
        o_a = _attention(q, k, v, batch, seq)
        o_b = _pool(proj, pool_wb, pool_sc, l, batch, seq)
        o_c = _ssd(proj, dt, dtb, alog, dful, ssd_nw, expand, tri, l, batch, seq)
        x1, h2 = _merge(xf, proj, o_a, o_b, o_c, mods, wb, wo, n2, l, seq)
        xf, h = _ffn(h2, x1, mods, upw, ffn_conv_w, fcb, dw, n1, l, seq)
    return xf.reshape(batch, seq, d)
```

```python
import functools

import jax
import jax.numpy as jnp
import numpy as np
from jax import lax
from jax.experimental import pallas as pl
from jax.experimental.pallas import tpu as pltpu

F32 = jnp.float32
BF16 = jnp.bfloat16

D_MODEL = 1024
MLA_HEADS = 8
QK_NOPE = 64
QK_ROPE = 32
QK_HEAD = QK_NOPE + QK_ROPE
V_HEAD = 64
Q_LORA = 384
KV_LORA = 256
ROPE_THETA = 10000.0
MLA_OUT = MLA_HEADS * V_HEAD
POOL_WINDOWS = (2, 4, 8, 16)
POOL_GROUP = 128
POOL_DIM = len(POOL_WINDOWS) * POOL_GROUP
SSD_HEADS = 16
SSD_HEAD_DIM = 64
SSD_INNER = SSD_HEADS * SSD_HEAD_DIM
SSD_GROUPS = 2
SSD_STATE = 128
SSD_CONV = 4
SSD_CHUNK = 128
FFN_DIM = 2816
FFN_CONV = 3
N_BRANCH = 3
EPS = 1e-6

SUBLANES = 8
HEAD_PAD = 128
NEG_BIG = -1e30
CONV_HALO = SUBLANES

BC_W = 2 * SSD_GROUPS * SSD_STATE
GROUP_W = SSD_INNER // SSD_GROUPS
P_Z = 0
P_XS = P_Z + SSD_INNER
P_GATE = P_XS + SSD_INNER
P_POOL = P_GATE + N_BRANCH * D_MODEL
P_BC = P_POOL + POOL_DIM
P_LAT = P_BC + BC_W
LAT_W = Q_LORA + KV_LORA + HEAD_PAD
PROJ_W = P_LAT + LAT_W
DT_PAD = HEAD_PAD

FFN_CHUNK = 256
FFN_NCHUNK = FFN_DIM // FFN_CHUNK

VMEM_LIMIT = 56 * 1024 * 1024


def _cparams(sem):
    return pltpu.CompilerParams(dimension_semantics=sem, vmem_limit_bytes=VMEM_LIMIT)


def _const_block(a):
    nd = a.ndim
    return pl.BlockSpec(a.shape, lambda *_: (0,) * nd, pipeline_mode=pl.Buffered(1))


def _layer_block(a, layer):
    nd = a.ndim
    return pl.BlockSpec((1,) + a.shape[1:], lambda *_: (layer,) + (0,) * (nd - 1),
                        pipeline_mode=pl.Buffered(1))


def _sigmoid(v):
    return 0.5 * jnp.tanh(0.5 * v) + 0.5


def _silu(v):
    half = 0.5 * v
    return half * jnp.tanh(half) + half


def _norm_mod(x, w, shift, scale):
    y = x * lax.rsqrt(jnp.mean(x * x, axis=-1, keepdims=True) + EPS)
    return (y * w) * (1.0 + scale) + shift


def _shift_rows(u, prev, k):
    full = pltpu.roll(u, k, 0)
    row = lax.broadcasted_iota(jnp.int32, prev.shape, 0)
    first = jnp.where(row < k, pltpu.roll(prev, k, 0), full[:SUBLANES])
    return jnp.concatenate([first, full[SUBLANES:]], axis=0)


def _causal_conv(u, prev, w, b, taps):
    out = w[taps - 1:taps, :] * u + b
    for k in range(taps - 1):
        out = out + w[k:k + 1, :] * _shift_rows(u, prev, taps - 1 - k)
    return out


def _split(x):
    hi = x.astype(BF16)
    return hi, (x - hi.astype(F32)).astype(BF16)


def _split_dot(x, m):
    hi, lo = _split(x)
    return jnp.dot(hi, m, preferred_element_type=F32) + jnp.dot(lo, m, preferred_element_type=F32)


def _split_dot_left(m, x):
    hi, lo = _split(x)
    return jnp.dot(m, hi, preferred_element_type=F32) + jnp.dot(m, lo, preferred_element_type=F32)


def _mods_kernel(c_ref, w_ref, b_ref, o_ref):
    ca = _silu(c_ref[...]).astype(BF16)
    w = w_ref[0].astype(BF16)
    o_ref[0] = jnp.dot(ca, w, preferred_element_type=F32) + b_ref[0]


def _mods(c, ada_w, ada_b):
    depth, d, n = ada_w.shape
    b = c.shape[0]
    tn = 1536
    return pl.pallas_call(
        _mods_kernel,
        grid=(depth, n // tn),
        in_specs=[pl.BlockSpec((b, d), lambda l, j: (0, 0)),
                  pl.BlockSpec((1, d, tn), lambda l, j: (l, 0, j)),
                  pl.BlockSpec((1, 1, tn), lambda l, j: (l, 0, j))],
        out_specs=pl.BlockSpec((1, b, tn), lambda l, j: (l, 0, j)),
        out_shape=jax.ShapeDtypeStruct((depth, b, n), F32),
        compiler_params=_cparams(("arbitrary", "arbitrary")),
        name="adaln_mods",
    )(c, ada_w, ada_b.reshape(depth, 1, n))


def _rope_kernel(pos_ref, freq_ref, sign_ref, c_ref, s_ref):
    ang = pos_ref[...] * freq_ref[...]
    c_ref[...] = jnp.cos(ang)
    s_ref[...] = jnp.sin(ang) * sign_ref[...]


def _rope_tables(positions):
    t = positions.size
    tm = 2048
    half = QK_ROPE // 2
    inv_freq = ROPE_THETA ** (-jnp.arange(0, QK_ROPE, 2, dtype=F32) / QK_ROPE)
    zeros = jnp.zeros((QK_NOPE,), F32)
    tail = jnp.zeros((HEAD_PAD - QK_HEAD,), F32)
    freq = jnp.concatenate([zeros, inv_freq, inv_freq, tail]).reshape(1, HEAD_PAD)
    sign = jnp.concatenate([zeros, -jnp.ones((half,), F32), jnp.ones((half,), F32), tail]).reshape(1, HEAD_PAD)
    pos = positions.astype(F32).reshape(t, 1)
    row = pl.BlockSpec((1, HEAD_PAD), lambda i: (0, 0))
    tab = pl.BlockSpec((tm, HEAD_PAD), lambda i: (i, 0))
    return pl.pallas_call(
        _rope_kernel,
        grid=(t // tm,),
        in_specs=[pl.BlockSpec((tm, 1), lambda i: (i, 0)), row, row],
        out_specs=[tab, tab],
        out_shape=[jax.ShapeDtypeStruct((t, HEAD_PAD), F32)] * 2,
        compiler_params=_cparams(("arbitrary",)),
        name="rope_tables",
    )(pos, freq, sign)


def _prenorm_kernel(x_ref, mod_ref, w_ref, h_ref):
    m = mod_ref[0, 0]
    h_ref[...] = _norm_mod(x_ref[...], w_ref[0], m[0:1], m[1:2]).astype(BF16)


def _prenorm(xf, mods, norm_w, layer, seq):
    t, d = xf.shape
    tm = 1024
    per_seq = seq // tm
    return pl.pallas_call(
        _prenorm_kernel,
        grid=(t // tm,),
        in_specs=[pl.BlockSpec((tm, d), lambda i: (i, 0)),
                  pl.BlockSpec((1, 1, 6, d), lambda i: (layer, i // per_seq, 0, 0)),
                  _layer_block(norm_w, layer)],
        out_specs=pl.BlockSpec((tm, d), lambda i: (i, 0)),
        out_shape=jax.ShapeDtypeStruct((t, d), BF16),
        compiler_params=_cparams(("arbitrary",)),
        name="prenorm",
    )(xf, mods, norm_w)


INPROJ_COLS = 512


def _inproj_kernel(h_ref, w_ref, cw_ref, cb_ref, proj_ref, dt_ref, car_ref):
    tm = h_ref.shape[0]
    wc = INPROJ_COLS

    @pl.when(pl.program_id(1) == 0)
    def _():
        car_ref[...] = jnp.zeros(car_ref.shape, F32)

    h = h_ref[...]

    def mm(c0, width):
        return lax.dot_general(h, w_ref[0, c0:c0 + width, :], (((1,), (1,)), ((), ())),
                               preferred_element_type=F32)

    for rel in range(0, SSD_INNER, wc):
        proj_ref[:, P_Z + rel:P_Z + rel + wc] = _silu(mm(P_Z + rel, wc)).astype(BF16)
    for rel in range(0, SSD_INNER + BC_W, wc):
        c0 = P_XS + rel if rel < SSD_INNER else P_BC + rel - SSD_INNER
        cs = slice(rel, rel + wc)
        u = mm(c0, wc)
        prev = car_ref[:, cs]
        car_ref[:, cs] = u[tm - CONV_HALO:, :]
        conv = _causal_conv(u, prev, cw_ref[0, :, cs], cb_ref[0, :, cs], SSD_CONV)
        proj_ref[:, c0:c0 + wc] = conv.astype(BF16)
    for c0 in range(P_GATE, P_BC, wc):
        proj_ref[:, c0:c0 + wc] = mm(c0, wc).astype(BF16)
    lat = mm(P_LAT, LAT_W)
    proj_ref[:, P_LAT:] = lat.astype(BF16)
    dt_ref[...] = lat[:, LAT_W - DT_PAD:]


def _inproj(h, w_main, conv_w, conv_b, layer, batch, seq):
    t, d = h.shape
    tm = 1024
    per_seq = seq // tm
    rows = lambda n: pl.BlockSpec((tm, n), lambda b, j: (b * per_seq + j, 0))
    return pl.pallas_call(
        _inproj_kernel,
        grid=(batch, per_seq),
        in_specs=[rows(d), _layer_block(w_main, layer), _layer_block(conv_w, layer),
                  _layer_block(conv_b, layer)],
        out_specs=[rows(PROJ_W), rows(DT_PAD)],
        out_shape=[jax.ShapeDtypeStruct((t, PROJ_W), BF16),
                   jax.ShapeDtypeStruct((t, DT_PAD), F32)],
        scratch_shapes=[pltpu.VMEM((CONV_HALO, SSD_INNER + BC_W), F32)],
        compiler_params=_cparams(("arbitrary", "arbitrary")),
        name="in_projection",
    )(h, w_main, conv_w, conv_b)


def _rope_apply(y, cos, sin_signed, lane):
    half = QK_ROPE // 2
    partner = jnp.where(lane < QK_NOPE + half,
                        pltpu.roll(y, HEAD_PAD - half, 1),
                        pltpu.roll(y, half, 1))
    return y * cos + partner * sin_signed


def _mla_prep_kernel(lat_ref, cos_ref, sin_ref, qan_ref, kvan_ref, wq_ref, wkv_ref,
                     gq_ref, gk_ref, qw_ref, kw_ref, krw_ref, q_ref, k_ref, v_ref):
    kw = MLA_HEADS * HEAD_PAD
    pair = 2 * HEAD_PAD
    qw = qw_ref[0]
    kwt = kw_ref[0]
    vlane = lax.broadcasted_iota(jnp.int32, (1, kw), 1)
    lane = lax.broadcasted_iota(jnp.int32, (MLA_SUB, HEAD_PAD), 1)

    for r0 in range(0, lat_ref.shape[0], MLA_SUB):
        rs = slice(r0, r0 + MLA_SUB)
        lat = lat_ref[rs, :].astype(F32)
        q_lat = lat[:, :Q_LORA]
        c_kv = lat[:, Q_LORA:Q_LORA + KV_LORA]
        cos = cos_ref[rs, :]
        sin = sin_ref[rs, :]
        k_rope = jnp.where((lane >= QK_NOPE) & (lane < QK_HEAD), lat[:, Q_LORA + KV_LORA:], 0.0)

        qn = q_lat * lax.rsqrt(jnp.mean(q_lat * q_lat, axis=-1, keepdims=True) + EPS) * qan_ref[0]
        cn = c_kv * lax.rsqrt(jnp.mean(c_kv * c_kv, axis=-1, keepdims=True) + EPS) * kvan_ref[0]
        q = jnp.dot(qn.astype(BF16), wq_ref[0], preferred_element_type=F32)
        kv = jnp.dot(cn.astype(BF16), wkv_ref[0], preferred_element_type=F32)
        v_ref[rs, :] = jnp.where(vlane % HEAD_PAD < V_HEAD, kv[:, kw:], 1.0).astype(BF16)

        kr_ms = jnp.sum(k_rope * k_rope, axis=-1, keepdims=True) * (1.0 / QK_ROPE)
        kr = _rope_apply(k_rope * lax.rsqrt(kr_ms + EPS) * krw_ref[0], cos, sin, lane)

        for c0 in range(0, kw, pair):
            xq = q[:, c0:c0 + pair]
            ms = jnp.dot((xq * xq).astype(BF16), gq_ref[...], preferred_element_type=F32)
            yq = xq * lax.rsqrt(ms + EPS) * qw[:, c0:c0 + pair]
            xk = kv[:, c0:c0 + pair]
            ms = jnp.dot((xk * xk).astype(BF16), gk_ref[...], preferred_element_type=F32)
            yk = xk * lax.rsqrt(ms + EPS) * kwt[:, c0:c0 + pair]
            for h0 in range(0, pair, HEAD_PAD):
                q_ref[rs, c0 + h0:c0 + h0 + HEAD_PAD] = _rope_apply(
                    yq[:, h0:h0 + HEAD_PAD], cos, sin, lane).astype(BF16)
                k_ref[rs, c0 + h0:c0 + h0 + HEAD_PAD] = (yk[:, h0:h0 + HEAD_PAD] + kr).astype(BF16)


MLA_SUB = 256


def _mla_prep(proj, cos_t, sin_t, qan, kvan, wq, wkv, gq, gk, qw, kw, krw, layer):
    t = proj.shape[0]
    tm = 1024
    hw = MLA_HEADS * HEAD_PAD
    lb = lambda a: _layer_block(a, layer)
    return pl.pallas_call(
        _mla_prep_kernel,
        grid=(t // tm,),
        in_specs=[pl.BlockSpec((tm, LAT_W), lambda i: (i, P_LAT // LAT_W)),
                  pl.BlockSpec((tm, HEAD_PAD), lambda i: (i, 0)),
                  pl.BlockSpec((tm, HEAD_PAD), lambda i: (i, 0)),
                  lb(qan), lb(kvan), lb(wq), lb(wkv),
                  _const_block(gq), _const_block(gk),
                  lb(qw), lb(kw), lb(krw)],
        out_specs=[pl.BlockSpec((tm, hw), lambda i: (i, 0))] * 3,
        out_shape=[jax.ShapeDtypeStruct((t, hw), BF16)] * 3,
        compiler_params=_cparams(("arbitrary",)),
        name="mla_prep",
    )(proj, cos_t, sin_t, qan, kvan, wq, wkv, gq, gk, qw, kw, krw)


ATT_TILE = 512


def _attn_kernel(q_ref, k_ref, v_ref, o_ref):
    seq = q_ref.shape[0]
    tq = ATT_TILE
    hq = tq // 2
    lane = lax.broadcasted_iota(jnp.int32, (tq, HEAD_PAD), 1)
    mask_a = (lax.broadcasted_iota(jnp.int32, (tq, hq), 1)
              <= lax.broadcasted_iota(jnp.int32, (tq, hq), 0))
    mask_b = (lax.broadcasted_iota(jnp.int32, (hq, hq), 1)
              <= lax.broadcasted_iota(jnp.int32, (hq, hq), 0))
    nt = (((1,), (1,)), ((), ()))
    heads = (slice(0, HEAD_PAD), slice(HEAD_PAD, 2 * HEAD_PAD))

    def block(state, q, k0, width, hs, mask):
        m, acc = state
        s = lax.dot_general(q, k_ref[k0:k0 + width, hs], nt, preferred_element_type=F32)
        if mask is not None:
            s = jnp.where(mask, s, NEG_BIG)
        m_new = jnp.maximum(m, jnp.max(s, axis=-1, keepdims=True))
        p = jnp.exp((s - m_new).astype(BF16))
        acc = jnp.exp(m - m_new) * acc + jnp.dot(p, v_ref[k0:k0 + width, hs],
                                                 preferred_element_type=F32)
        return m_new, acc

    for qi in range(seq // tq):
        q0 = qi * tq
        qs = [q_ref[q0:q0 + tq, hs] for hs in heads]
        init = (jnp.full((tq, 1), NEG_BIG, F32), jnp.zeros((tq, HEAD_PAD), F32))
        states = [init, init]
        for j in range(qi):
            states = [block(st, q, j * tq, tq, hs, None) for st, q, hs in zip(states, qs, heads)]
        states = [block(st, q, q0, hq, hs, mask_a) for st, q, hs in zip(states, qs, heads)]
        outs = []
        for (m, acc), q, hs in zip(states, qs, heads):
            m_lo, acc_lo = block((m[hq:], acc[hq:]), q[hq:], q0 + hq, hq, hs, mask_b)
            acc = jnp.concatenate([acc[:hq], acc_lo], axis=0)
            denom = jnp.where(lane < V_HEAD, pltpu.roll(acc, V_HEAD, 1), 1.0)
            outs.append(acc / denom)
        o_ref[q0:q0 + tq, :] = jnp.where(lane < V_HEAD, outs[0],
                                         pltpu.roll(outs[1], V_HEAD, 1)).astype(BF16)


def _attention(q, k, v, batch, seq):
    t = q.shape[0]
    pairs = MLA_HEADS // 2
    blk = pl.BlockSpec((seq, 2 * HEAD_PAD), lambda b, p: (b, p))
    return pl.pallas_call(
        _attn_kernel,
        grid=(batch, pairs),
        in_specs=[blk, blk, blk],
        out_specs=pl.BlockSpec((seq, 2 * V_HEAD), lambda b, p: (b, p)),
        out_shape=jax.ShapeDtypeStruct((t, MLA_OUT), BF16),
        compiler_params=_cparams(("arbitrary", "arbitrary")),
        name="mla_attention",
    )(q, k, v)


POOL_LEVELS = 4


def _pool_rows(u_ref, pw_ref, ps_ref, car_ref, j):
    ts = u_ref.shape[0]

    @pl.when(j == 0)
    def _():
        car_ref[...] = jnp.zeros(car_ref.shape, F32)

    t_pos = j * ts + lax.broadcasted_iota(jnp.int32, (ts, 1), 0)
    outs = []
    for g, w in enumerate(POOL_WINDOWS):
        gs = slice(g * POOL_GROUP, (g + 1) * POOL_GROUP)
        ug = u_ref[:, gs].astype(F32)
        acc = ug
        span = 1
        level = 0
        while span < w:
            prev = car_ref[level, :, gs]
            car_ref[level, :, gs] = acc[ts - SUBLANES:, :]
            acc = acc + _shift_rows(acc, prev, span)
            span *= 2
            level += 1
        count = jnp.minimum(t_pos + 1, w).astype(F32)
        pooled = acc / count - ug
        mixed = jnp.dot(pooled.astype(BF16), pw_ref[0, g], preferred_element_type=F32)
        outs.append((mixed * ps_ref[0, :, gs]).astype(BF16))
    return jnp.concatenate(outs, axis=1)


SSD_STEP_CHUNKS = 8


def _ssd_kernel(zs_ref, xs_ref, bc_ref, dt_ref, dtb_ref, alog_ref, dful_ref, nw_ref,
                exp_ref, tri_ref, o_ref, state_ref, gated_ref):
    lc = SSD_CHUNK

    @pl.when(pl.program_id(1) == 0)
    def _():
        state_ref[...] = jnp.zeros(state_ref.shape, F32)

    rows = lax.broadcasted_iota(jnp.int32, (lc, lc), 0)
    cols = lax.broadcasted_iota(jnp.int32, (lc, lc), 1)
    causal = cols <= rows
    lane = lax.broadcasted_iota(jnp.int32, (lc, 2 * SSD_HEAD_DIM), 1)
    nt = (((1,), (1,)), ((), ()))
    tn = (((0,), (0,)), ((), ()))
    heads_per_group = SSD_HEADS // SSD_GROUPS
    neg_a = -jnp.exp(alog_ref[0])
    dtb = dtb_ref[0]
    dful = dful_ref[0]
    nw = nw_ref[0]

    for r0 in range(0, xs_ref.shape[0], lc):
        rs = slice(r0, r0 + lc)
        xs = _silu(xs_ref[rs, :].astype(F32))
        bc = _silu(bc_ref[rs, :].astype(F32)).astype(BF16)
        bm = bc[:, :SSD_GROUPS * SSD_STATE]
        cm = bc[:, SSD_GROUPS * SSD_STATE:]

        v = dt_ref[rs, :] + dtb
        dt = jnp.maximum(v, 0.0) + jnp.log(1.0 + jnp.exp(-jnp.abs(v)))
        a_cs = _split_dot_left(tri_ref[...], dt * neg_a)
        a_last = a_cs[lc - 1:lc, :]
        ea = jnp.exp(a_cs)
        stacked = jnp.concatenate([dt, ea, jnp.exp(a_last - a_cs)], axis=0).astype(BF16)
        full = jnp.dot(stacked, exp_ref[...], preferred_element_type=F32)
        dt_full = full[0:lc]
        ea_full = full[lc:2 * lc]
        ds_full = full[2 * lc:3 * lc]
        chunk_decay = _split_dot(ea[lc - SUBLANES:], exp_ref[...])[SUBLANES - 1:SUBLANES, :]
        a_cs_t = a_cs.T

        xdt = xs * dt_full
        xdt_b = xdt.astype(BF16)
        xds_b = (xdt * ds_full).astype(BF16)

        for g in range(SSD_GROUPS):
            gs = slice(g * GROUP_W, (g + 1) * GROUP_W)
            bg = bm[:, g * SSD_STATE:(g + 1) * SSD_STATE]
            cg = cm[:, g * SSD_STATE:(g + 1) * SSD_STATE]
            cb = lax.dot_general(cg, bg, nt, preferred_element_type=F32)
            state = state_ref[g]
            y_off = jnp.dot(cg, state.astype(BF16), preferred_element_type=F32) * ea_full[:, gs]
            s_new = lax.dot_general(bg, xds_b[:, gs], tn, preferred_element_type=F32)
            state_ref[g] = state * chunk_decay[:, gs] + s_new
            for pp in range(heads_per_group // 2):
                h0 = g * heads_per_group + 2 * pp
                ps = slice(h0 * SSD_HEAD_DIM, (h0 + 2) * SSD_HEAD_DIM)
                ys = []
                for hd in (h0, h0 + 1):
                    seg = a_cs[:, hd:hd + 1] - a_cs_t[hd:hd + 1, :]
                    decay = jnp.exp(jnp.where(causal, seg, NEG_BIG))
                    ys.append(jnp.dot((cb * decay).astype(BF16), xdt_b[:, ps],
                                      preferred_element_type=F32))
                y_pair = jnp.where(lane < SSD_HEAD_DIM, ys[0], ys[1])
                ls = slice(2 * pp * SSD_HEAD_DIM, (2 * pp + 2) * SSD_HEAD_DIM)
                y = y_pair + y_off[:, ls] + xs[:, ps] * dful[:, ps]
                gated_ref[rs, ps] = y * zs_ref[rs, ps].astype(F32)

        for g in range(SSD_GROUPS):
            gs = slice(g * GROUP_W, (g + 1) * GROUP_W)
            gt = gated_ref[rs, gs]
            ms = jnp.mean(gt * gt, axis=-1, keepdims=True)
            o_ref[rs, gs] = (gt * lax.rsqrt(ms + EPS) * nw[:, gs]).astype(o_ref.dtype)


def _ssd(proj, dt, dtb, alog, dful, nw, expand, tri, layer, batch, seq):
    t = proj.shape[0]
    lc = SSD_STEP_CHUNKS * SSD_CHUNK
    nc = seq // lc
    lb = lambda a: _layer_block(a, layer)
    return pl.pallas_call(
        _ssd_kernel,
        grid=(batch, nc),
        in_specs=[pl.BlockSpec((lc, SSD_INNER), lambda b, j: (b * nc + j, P_Z // SSD_INNER)),
                  pl.BlockSpec((lc, SSD_INNER), lambda b, j: (b * nc + j, P_XS // SSD_INNER)),
                  pl.BlockSpec((lc, BC_W), lambda b, j: (b * nc + j, P_BC // BC_W)),
                  pl.BlockSpec((lc, DT_PAD), lambda b, j: (b * nc + j, 0)),
                  lb(dtb), lb(alog), lb(dful), lb(nw),
                  _const_block(expand), _const_block(tri)],
        out_specs=pl.BlockSpec((lc, SSD_INNER), lambda b, j: (b * nc + j, 0)),
        out_shape=jax.ShapeDtypeStruct((t, SSD_INNER), BF16),
        scratch_shapes=[pltpu.VMEM((SSD_GROUPS, SSD_STATE, GROUP_W), F32),
                        pltpu.VMEM((lc, SSD_INNER), F32)],
        compiler_params=_cparams(("arbitrary", "arbitrary")),
        name="ssd_branch",
    )(proj, proj, proj, dt, dtb, alog, dful, nw, expand, tri)


def _merge_kernel(x_ref, ga_ref, gb_ref, gc_ref, oa_ref, u_ref, oc_ref, mod_ref,
                  wb_ref, wo_ref, nw_ref, pw_ref, ps_ref, x_out, h_out, car_ref):
    m = mod_ref[0, 0]
    o_b = _pool_rows(u_ref, pw_ref, ps_ref, car_ref, pl.program_id(1))
    ya = jnp.dot(oa_ref[...], wb_ref[0, 0:MLA_OUT, :], preferred_element_type=F32)
    yb = jnp.dot(o_b, wb_ref[0, MLA_OUT:MLA_OUT + POOL_DIM, :], preferred_element_type=F32)
    yc = jnp.dot(oc_ref[...], wb_ref[0, MLA_OUT + POOL_DIM:, :], preferred_element_type=F32)
    merged = (_sigmoid(ga_ref[...].astype(F32)) * ya
              + _sigmoid(gb_ref[...].astype(F32)) * yb
              + _sigmoid(gc_ref[...].astype(F32)) * yc)
    y = jnp.dot(merged.astype(BF16), wo_ref[0], preferred_element_type=F32)
    x1 = x_ref[...] + m[2:3] * y
    x_out[...] = x1
    h_out[...] = _norm_mod(x1, nw_ref[0], m[3:4], m[4:5]).astype(BF16)


def _merge(xf, proj, o_a, o_c, mods, wb, wo, norm_w, pool_w, pool_scale, layer, batch, seq):
    t, d = xf.shape
    tm = 1024
    per_seq = seq // tm
    col = lambda n, blk: pl.BlockSpec((tm, n), lambda b, j: (b * per_seq + j, blk))
    tile = lambda n: col(n, 0)
    return pl.pallas_call(
        _merge_kernel,
        grid=(batch, per_seq),
        in_specs=[tile(d), col(d, P_GATE // d), col(d, P_GATE // d + 1), col(d, P_GATE // d + 2),
                  tile(MLA_OUT), col(POOL_DIM, P_POOL // POOL_DIM), tile(SSD_INNER),
                  pl.BlockSpec((1, 1, 6, d), lambda b, j: (layer, b, 0, 0)),
                  _layer_block(wb, layer), _layer_block(wo, layer), _layer_block(norm_w, layer),
                  _layer_block(pool_w, layer), _layer_block(pool_scale, layer)],
        out_specs=[tile(d), tile(d)],
        out_shape=[jax.ShapeDtypeStruct((t, d), F32), jax.ShapeDtypeStruct((t, d), BF16)],
        scratch_shapes=[pltpu.VMEM((POOL_LEVELS, SUBLANES, POOL_DIM), F32)],
        compiler_params=_cparams(("arbitrary", "arbitrary")),
        name="merge_out",
    )(xf, proj, proj, proj, o_a, proj, o_c, mods, wb, wo, norm_w, pool_w, pool_scale)


def _ffn_kernel(with_next, h_ref, x_ref, mod_ref, *rest):
    if with_next:
        modn_ref, nwn_ref, upw_ref, cw_ref, cb_ref, dw_ref, x_out, h_out, carry_ref, act_ref = rest
    else:
        upw_ref, cw_ref, cb_ref, dw_ref, x_out, carry_ref, act_ref = rest
    ts = h_ref.shape[0]

    @pl.when(pl.program_id(1) == 0)
    def _():
        carry_ref[...] = jnp.zeros(carry_ref.shape, F32)

    h = h_ref[...]

    def conv_half(cs):
        u = jnp.dot(h, upw_ref[0, :, cs], preferred_element_type=F32)
        prev = carry_ref[:, cs]
        carry_ref[:, cs] = u[ts - CONV_HALO:, :]
        return _causal_conv(u, prev, cw_ref[0, :, cs], cb_ref[0, :, cs], FFN_CONV)

    for c in range(FFN_NCHUNK):
        gate = conv_half(slice(c * FFN_CHUNK, (c + 1) * FFN_CHUNK))
        val = conv_half(slice(FFN_DIM + c * FFN_CHUNK, FFN_DIM + (c + 1) * FFN_CHUNK))
        act_ref[c] = (_silu(gate) * val).astype(BF16)
    y = jnp.dot(act_ref[0], dw_ref[0, 0:FFN_CHUNK, :], preferred_element_type=F32)
    for c in range(1, FFN_NCHUNK):
        y = y + jnp.dot(act_ref[c], dw_ref[0, c * FFN_CHUNK:(c + 1) * FFN_CHUNK, :],
                        preferred_element_type=F32)
    m = mod_ref[0, 0]
    x2 = x_ref[...] + m[5:6] * y
    x_out[...] = x2
    if with_next:
        mn = modn_ref[0, 0]
        h_out[...] = _norm_mod(x2, nwn_ref[0], mn[0:1], mn[1:2]).astype(BF16)


def _ffn(h2, x1, mods, upw, cw, cb, dw, norm1_w, layer, seq):
    t, d = x1.shape
    ts = 512
    per_seq = seq // ts
    batch = t // seq
    with_next = layer + 1 < mods.shape[0]
    tile = pl.BlockSpec((ts, d), lambda b, j: (b * per_seq + j, 0))
    modspec = lambda l: pl.BlockSpec((1, 1, 6, d), lambda b, j: (l, b, 0, 0))
    in_specs = [tile, tile, modspec(layer)]
    args = [h2, x1, mods]
    out_specs = [tile]
    out_shape = [jax.ShapeDtypeStruct((t, d), F32)]
    if with_next:
        in_specs += [modspec(layer + 1), _layer_block(norm1_w, layer + 1)]
        args += [mods, norm1_w]
        out_specs.append(tile)
        out_shape.append(jax.ShapeDtypeStruct((t, d), BF16))
    in_specs += [_layer_block(a, layer) for a in (upw, cw, cb, dw)]
    args += [upw, cw, cb, dw]
    outs = pl.pallas_call(
        functools.partial(_ffn_kernel, with_next),
        grid=(batch, per_seq),
        in_specs=in_specs,
        out_specs=out_specs,
        out_shape=out_shape,
        scratch_shapes=[pltpu.VMEM((CONV_HALO, 2 * FFN_DIM), F32),
                        pltpu.VMEM((FFN_NCHUNK, ts, FFN_CHUNK), BF16)],
        compiler_params=_cparams(("arbitrary", "arbitrary")),
        name="conv_mlp",
    )(*args)
    return (outs[0], outs[1]) if with_next else (outs[0], None)


def _relayout_kernel(pieces, zero_rows, w_ref, o_ref):
    for src0, width, dst0 in pieces:
        o_ref[0, dst0:dst0 + width, :] = w_ref[0, src0:src0 + width, :].astype(BF16)
    for dst0, width in zero_rows:
        o_ref[0, dst0:dst0 + width, :] = jnp.zeros((width, o_ref.shape[2]), BF16)


def _relayout_w_in(w_t, pieces, zero_rows):
    depth, n_in, d = w_t.shape
    tk = 256
    return pl.pallas_call(
        functools.partial(_relayout_kernel, pieces, zero_rows),
        grid=(depth, d // tk),
        in_specs=[pl.BlockSpec((1, n_in, tk), lambda l, i: (l, 0, i))],
        out_specs=pl.BlockSpec((1, PROJ_W, tk), lambda l, i: (l, 0, i)),
        out_shape=jax.ShapeDtypeStruct((depth, PROJ_W, d), BF16),
        compiler_params=_cparams(("arbitrary", "arbitrary")),
        name="w_in_relayout",
    )(w_t)


def _head_blocks(w, per_head, lo, hi):
    depth, kdim, _ = w.shape
    w = w.reshape(depth, kdim, MLA_HEADS, per_head)[..., lo:hi]
    w = jnp.pad(w, ((0, 0), (0, 0), (0, 0), (0, HEAD_PAD - (hi - lo))))
    return w.reshape(depth, kdim, MLA_HEADS * HEAD_PAD)


def _group_mean_matrix(sizes):
    blk = np.zeros((HEAD_PAD, HEAD_PAD), np.float32)
    start = 0
    for n in sizes:
        blk[start:start + n, start:start + n] = 1.0 / n
        start += n
    return jnp.asarray(np.kron(np.eye(2, dtype=np.float32), blk), BF16)


def _rows(v, pad_to=None):
    if pad_to is not None:
        v = jnp.pad(v, ((0, 0), (0, pad_to - v.shape[1])))
    return v[:, None, :]


def kernel(x, c, positions, ada_w, ada_b, norm1_w, w_in, q_a_norm, w_q_b, kv_a_norm, w_kv_b, q_norm, k_norm, pool_w, pool_scale, ssd_conv_w, ssd_conv_b, ssd_dt_bias, ssd_a_log, ssd_d, ssd_norm_w, w_branch, w_out, norm2_w, ffn_up, ffn_conv_w, ffn_conv_b, ffn_down):
    batch, seq, d = x.shape
    depth = ada_w.shape[0]
    t = batch * seq
    xf = x.reshape(t, d)

    mods = _mods(c, ada_w, ada_b).reshape(depth, batch, 6, d)
    cos_t, sin_t = _rope_tables(positions)

    gq = _group_mean_matrix((QK_NOPE, QK_ROPE))
    gk = _group_mean_matrix((QK_NOPE,))
    head_of_col = np.arange(SSD_INNER) // SSD_HEAD_DIM
    expand = jnp.asarray(np.arange(DT_PAD)[:, None] == head_of_col[None, :], BF16)
    tri = jnp.asarray(np.tril(np.ones((SSD_CHUNK, SSD_CHUNK), np.float32)), BF16)

    o = 0
    offs = {}
    for name, size in (("q_lat", Q_LORA), ("c_kv", KV_LORA), ("k_rope", QK_ROPE), ("pool", POOL_DIM),
                       ("z", SSD_INNER), ("xs", SSD_INNER), ("bc", BC_W),
                       ("dt", SSD_HEADS), ("gate", N_BRANCH * D_MODEL)):
        offs[name] = (o, o + size)
        o += size
    pieces = []
    for name, dst0 in (("z", P_Z), ("xs", P_XS), ("gate", P_GATE), ("pool", P_POOL), ("bc", P_BC),
                       ("q_lat", P_LAT), ("c_kv", P_LAT + Q_LORA)):
        pieces.append((offs[name][0], offs[name][1] - offs[name][0], dst0))
    tail0 = PROJ_W - HEAD_PAD
    pieces.append((offs["dt"][0], SSD_HEADS, tail0))
    pieces.append((offs["k_rope"][0], QK_ROPE, tail0 + QK_NOPE))
    zero_rows = ((tail0 + SSD_HEADS, QK_NOPE - SSD_HEADS), (tail0 + QK_HEAD, HEAD_PAD - QK_HEAD))
    w_main = _relayout_w_in(jnp.swapaxes(w_in, 1, 2), tuple(pieces), zero_rows)

    wq = _head_blocks(w_q_b, QK_HEAD, 0, QK_HEAD).astype(BF16)
    wkv = jnp.concatenate([_head_blocks(w_kv_b, QK_NOPE + V_HEAD, 0, QK_NOPE),
                           _head_blocks(w_kv_b, QK_NOPE + V_HEAD, QK_NOPE, QK_NOPE + V_HEAD)],
                          axis=2).astype(BF16)
    hw = MLA_HEADS * HEAD_PAD
    qw = jnp.tile(_rows(q_norm, HEAD_PAD), (1, 1, MLA_HEADS)) * (QK_HEAD ** -0.5)
    kw = jnp.tile(_rows(k_norm[:, :QK_NOPE], HEAD_PAD), (1, 1, MLA_HEADS))
    krw = _rows(jnp.pad(k_norm[:, QK_NOPE:], ((0, 0), (QK_NOPE, 0))), HEAD_PAD)
    qan = _rows(q_a_norm)
    kvan = _rows(kv_a_norm)

    pool_wb = pool_w.astype(BF16)
    pool_sc = _rows(pool_scale)
    conv_b = _rows(ssd_conv_b)
    dtb = _rows(ssd_dt_bias, DT_PAD)
    alog = _rows(ssd_a_log, DT_PAD)
    dful = _rows(jnp.repeat(ssd_d, SSD_HEAD_DIM, axis=1))
    ssd_nw = _rows(ssd_norm_w)
    wb = w_branch.astype(BF16)
    wo = w_out.astype(BF16)
    n1 = _rows(norm1_w)
    n2 = _rows(norm2_w)
    upw = ffn_up.astype(BF16)
    dw = ffn_down.astype(BF16)
    fcb = _rows(ffn_conv_b)

    h = _prenorm(xf, mods, n1, 0, seq)
    for l in range(depth):
        proj, dt = _inproj(h, w_main, ssd_conv_w, conv_b, l, batch, seq)
        q, k, v = _mla_prep(proj, cos_t, sin_t, qan, kvan, wq, wkv, gq, gk, qw, kw, krw, l)
        o_a = _attention(q, k, v, batch, seq)
        o_c = _ssd(proj, dt, dtb, alog, dful, ssd_nw, expand, tri, l, batch, seq)
        x1, h2 = _merge(xf, proj, o_a, o_c, mods, wb, wo, n2, pool_wb, pool_sc, l, batch, seq)
        xf, h = _ffn(h2, x1, mods, upw, ffn_conv_w, fcb, dw, n1, l, seq)
    return xf.reshape(batch, seq, d)
```

```python
import functools

import jax
import jax.numpy as jnp
import numpy as np
from jax import lax
from jax.experimental import pallas as pl
from jax.experimental.pallas import tpu as pltpu

F32 = jnp.float32
BF16 = jnp.bfloat16

D_MODEL = 1024
MLA_HEADS = 8
QK_NOPE = 64
QK_ROPE = 32
QK_HEAD = QK_NOPE + QK_ROPE
V_HEAD = 64
Q_LORA = 384
KV_LORA = 256
ROPE_THETA = 10000.0
MLA_OUT = MLA_HEADS * V_HEAD
POOL_WINDOWS = (2, 4, 8, 16)
POOL_GROUP = 128
POOL_DIM = len(POOL_WINDOWS) * POOL_GROUP
SSD_HEADS = 16
SSD_HEAD_DIM = 64
SSD_INNER = SSD_HEADS * SSD_HEAD_DIM
SSD_GROUPS = 2
SSD_STATE = 128
SSD_CONV = 4
SSD_CHUNK = 128
FFN_DIM = 2816
FFN_CONV = 3
N_BRANCH = 3
EPS = 1e-6

SUBLANES = 8
HEAD_PAD = 128
NEG_BIG = -1e30
CONV_HALO = SUBLANES

BC_W = 2 * SSD_GROUPS * SSD_STATE
GROUP_W = SSD_INNER // SSD_GROUPS
P_Z = 0
P_XS = P_Z + SSD_INNER
P_GATE = P_XS + SSD_INNER
P_POOL = P_GATE + N_BRANCH * D_MODEL
P_BC = P_POOL + POOL_DIM
P_LAT = P_BC + BC_W
LAT_W = Q_LORA + KV_LORA + HEAD_PAD
PROJ_W = P_LAT + LAT_W
DT_PAD = HEAD_PAD

FFN_CHUNK = 256
FFN_NCHUNK = FFN_DIM // FFN_CHUNK

VMEM_LIMIT = 56 * 1024 * 1024


def _cparams(sem):
    return pltpu.CompilerParams(dimension_semantics=sem, vmem_limit_bytes=VMEM_LIMIT)


def _const_block(a):
    nd = a.ndim
    return pl.BlockSpec(a.shape, lambda *_: (0,) * nd, pipeline_mode=pl.Buffered(1))


def _layer_block(a, layer):
    nd = a.ndim
    return pl.BlockSpec((1,) + a.shape[1:], lambda *_: (layer,) + (0,) * (nd - 1),
                        pipeline_mode=pl.Buffered(1))


def _sigmoid(v):
    return 0.5 * jnp.tanh(0.5 * v) + 0.5


def _silu(v):
    half = 0.5 * v
    return half * jnp.tanh(half) + half


def _norm_mod(x, w, shift, scale):
    y = x * lax.rsqrt(jnp.mean(x * x, axis=-1, keepdims=True) + EPS)
    return (y * w) * (1.0 + scale) + shift


def _shift_rows(u, prev, k):
    full = pltpu.roll(u, k, 0)
    row = lax.broadcasted_iota(jnp.int32, prev.shape, 0)
    first = jnp.where(row < k, pltpu.roll(prev, k, 0), full[:SUBLANES])
    return jnp.concatenate([first, full[SUBLANES:]], axis=0)


def _causal_conv(u, prev, w, b, taps):
    out = w[taps - 1:taps, :] * u + b
    for k in range(taps - 1):
        out = out + w[k:k + 1, :] * _shift_rows(u, prev, taps - 1 - k)
    return out


def _split(x):
    hi = x.astype(BF16)
    return hi, (x - hi.astype(F32)).astype(BF16)


def _split_dot(x, m):
    hi, lo = _split(x)
    return jnp.dot(hi, m, preferred_element_type=F32) + jnp.dot(lo, m, preferred_element_type=F32)


def _split_dot_left(m, x):
    hi, lo = _split(x)
    return jnp.dot(m, hi, preferred_element_type=F32) + jnp.dot(m, lo, preferred_element_type=F32)


def _mods_kernel(c_ref, w_ref, b_ref, o_ref):
    ca = _silu(c_ref[...]).astype(BF16)
    w = w_ref[0].astype(BF16)
    o_ref[0] = jnp.dot(ca, w, preferred_element_type=F32) + b_ref[0]


def _mods(c, ada_w, ada_b):
    depth, d, n = ada_w.shape
    b = c.shape[0]
    tn = 1536
    return pl.pallas_call(
        _mods_kernel,
        grid=(depth, n // tn),
        in_specs=[pl.BlockSpec((b, d), lambda l, j: (0, 0)),
                  pl.BlockSpec((1, d, tn), lambda l, j: (l, 0, j)),
                  pl.BlockSpec((1, 1, tn), lambda l, j: (l, 0, j))],
        out_specs=pl.BlockSpec((1, b, tn), lambda l, j: (l, 0, j)),
        out_shape=jax.ShapeDtypeStruct((depth, b, n), F32),
        compiler_params=_cparams(("arbitrary", "arbitrary")),
        name="adaln_mods",
    )(c, ada_w, ada_b.reshape(depth, 1, n))


def _rope_rows(positions):
    half = QK_ROPE // 2
    inv_freq = ROPE_THETA ** (-jnp.arange(0, QK_ROPE, 2, dtype=F32) / QK_ROPE)
    zeros = jnp.zeros((QK_NOPE,), F32)
    tail = jnp.zeros((HEAD_PAD - QK_HEAD,), F32)
    freq = jnp.concatenate([zeros, inv_freq, inv_freq, tail]).reshape(1, HEAD_PAD)
    sign = np.concatenate([np.zeros(QK_NOPE), -np.ones(half), np.ones(half),
                           np.zeros(HEAD_PAD - QK_HEAD)]).astype(np.float32).reshape(1, HEAD_PAD)
    return positions.astype(F32).reshape(positions.size, 1), freq, jnp.asarray(sign)


def _prenorm_kernel(x_ref, mod_ref, w_ref, pos_ref, freq_ref, sign_ref, h_ref, c_ref, s_ref):
    m = mod_ref[0, 0]
    h_ref[...] = _norm_mod(x_ref[...], w_ref[0], m[0:1], m[1:2]).astype(BF16)
    ang = pos_ref[...] * freq_ref[...]
    c_ref[...] = jnp.cos(ang)
    s_ref[...] = jnp.sin(ang) * sign_ref[...]


def _prenorm(xf, mods, norm_w, positions, layer, seq):
    t, d = xf.shape
    tm = 1024
    per_seq = seq // tm
    pos, freq, sign = _rope_rows(positions)
    row = pl.BlockSpec((1, HEAD_PAD), lambda i: (0, 0))
    tab = pl.BlockSpec((tm, HEAD_PAD), lambda i: (i, 0))
    return pl.pallas_call(
        _prenorm_kernel,
        grid=(t // tm,),
        in_specs=[pl.BlockSpec((tm, d), lambda i: (i, 0)),
                  pl.BlockSpec((1, 1, 6, d), lambda i: (layer, i // per_seq, 0, 0)),
                  _layer_block(norm_w, layer),
                  pl.BlockSpec((tm, 1), lambda i: (i, 0)), row, row],
        out_specs=[pl.BlockSpec((tm, d), lambda i: (i, 0)), tab, tab],
        out_shape=[jax.ShapeDtypeStruct((t, d), BF16)] + [jax.ShapeDtypeStruct((t, HEAD_PAD), F32)] * 2,
        compiler_params=_cparams(("arbitrary",)),
        name="prenorm_rope",
    )(xf, mods, norm_w, pos, freq, sign)


INPROJ_COLS = 512


def _inproj_kernel(h_ref, w_ref, cw_ref, cb_ref, proj_ref, dt_ref, car_ref):
    tm = h_ref.shape[0]
    wc = INPROJ_COLS

    @pl.when(pl.program_id(1) == 0)
    def _():
        car_ref[...] = jnp.zeros(car_ref.shape, F32)

    h = h_ref[...]

    def mm(c0, width):
        return lax.dot_general(h, w_ref[0, c0:c0 + width, :], (((1,), (1,)), ((), ())),
                               preferred_element_type=F32)

    for rel in range(0, SSD_INNER, wc):
        proj_ref[:, P_Z + rel:P_Z + rel + wc] = _silu(mm(P_Z + rel, wc)).astype(BF16)
    for rel in range(0, SSD_INNER + BC_W, wc):
        c0 = P_XS + rel if rel < SSD_INNER else P_BC + rel - SSD_INNER
        cs = slice(rel, rel + wc)
        u = mm(c0, wc)
        prev = car_ref[:, cs]
        car_ref[:, cs] = u[tm - CONV_HALO:, :]
        conv = _causal_conv(u, prev, cw_ref[0, :, cs], cb_ref[0, :, cs], SSD_CONV)
        proj_ref[:, c0:c0 + wc] = conv.astype(BF16)
    for c0 in range(P_GATE, P_BC, wc):
        proj_ref[:, c0:c0 + wc] = mm(c0, wc).astype(BF16)
    lat = mm(P_LAT, LAT_W)
    proj_ref[:, P_LAT:] = lat.astype(BF16)
    dt_ref[...] = lat[:, LAT_W - DT_PAD:]


def _inproj(h, w_main, conv_w, conv_b, layer, batch, seq):
    t, d = h.shape
    tm = 1024
    per_seq = seq // tm
    rows = lambda n: pl.BlockSpec((tm, n), lambda b, j: (b * per_seq + j, 0))
    return pl.pallas_call(
        _inproj_kernel,
        grid=(batch, per_seq),
        in_specs=[rows(d), _layer_block(w_main, layer), _layer_block(conv_w, layer),
                  _layer_block(conv_b, layer)],
        out_specs=[rows(PROJ_W), rows(DT_PAD)],
        out_shape=[jax.ShapeDtypeStruct((t, PROJ_W), BF16),
                   jax.ShapeDtypeStruct((t, DT_PAD), F32)],
        scratch_shapes=[pltpu.VMEM((CONV_HALO, SSD_INNER + BC_W), F32)],
        compiler_params=_cparams(("arbitrary", "arbitrary")),
        name="in_projection",
    )(h, w_main, conv_w, conv_b)


def _rope_apply(y, cos, sin_signed, lane):
    half = QK_ROPE // 2
    partner = jnp.where(lane < QK_NOPE + half,
                        pltpu.roll(y, HEAD_PAD - half, 1),
                        pltpu.roll(y, half, 1))
    return y * cos + partner * sin_signed


def _mla_prep_kernel(lat_ref, cos_ref, sin_ref, qan_ref, kvan_ref, wq_ref, wkv_ref,
                     gq_ref, gk_ref, qw_ref, kw_ref, krw_ref, q_ref, k_ref, v_ref):
    kw = MLA_HEADS * HEAD_PAD
    pair = 2 * HEAD_PAD
    qw = qw_ref[0]
    kwt = kw_ref[0]
    vlane = lax.broadcasted_iota(jnp.int32, (1, kw), 1)
    lane = lax.broadcasted_iota(jnp.int32, (MLA_SUB, HEAD_PAD), 1)

    for r0 in range(0, lat_ref.shape[0], MLA_SUB):
        rs = slice(r0, r0 + MLA_SUB)
        lat = lat_ref[rs, :].astype(F32)
        q_lat = lat[:, :Q_LORA]
        c_kv = lat[:, Q_LORA:Q_LORA + KV_LORA]
        cos = cos_ref[rs, :]
        sin = sin_ref[rs, :]
        k_rope = jnp.where((lane >= QK_NOPE) & (lane < QK_HEAD), lat[:, Q_LORA + KV_LORA:], 0.0)

        qn = q_lat * lax.rsqrt(jnp.mean(q_lat * q_lat, axis=-1, keepdims=True) + EPS) * qan_ref[0]
        cn = c_kv * lax.rsqrt(jnp.mean(c_kv * c_kv, axis=-1, keepdims=True) + EPS) * kvan_ref[0]
        q = jnp.dot(qn.astype(BF16), wq_ref[0], preferred_element_type=F32)
        kv = jnp.dot(cn.astype(BF16), wkv_ref[0], preferred_element_type=F32)
        v_ref[rs, :] = jnp.where(vlane % HEAD_PAD < V_HEAD, kv[:, kw:], 1.0).astype(BF16)

        kr_ms = jnp.sum(k_rope * k_rope, axis=-1, keepdims=True) * (1.0 / QK_ROPE)
        kr = _rope_apply(k_rope * lax.rsqrt(kr_ms + EPS) * krw_ref[0], cos, sin, lane)

        for c0 in range(0, kw, pair):
            xq = q[:, c0:c0 + pair]
            ms = jnp.dot((xq * xq).astype(BF16), gq_ref[...], preferred_element_type=F32)
            yq = xq * lax.rsqrt(ms + EPS) * qw[:, c0:c0 + pair]
            xk = kv[:, c0:c0 + pair]
            ms = jnp.dot((xk * xk).astype(BF16), gk_ref[...], preferred_element_type=F32)
            yk = xk * lax.rsqrt(ms + EPS) * kwt[:, c0:c0 + pair]
            for h0 in range(0, pair, HEAD_PAD):
                q_ref[rs, c0 + h0:c0 + h0 + HEAD_PAD] = _rope_apply(
                    yq[:, h0:h0 + HEAD_PAD], cos, sin, lane).astype(BF16)
                k_ref[rs, c0 + h0:c0 + h0 + HEAD_PAD] = (yk[:, h0:h0 + HEAD_PAD] + kr).astype(BF16)


MLA_SUB = 256


def _mla_prep(proj, cos_t, sin_t, qan, kvan, wq, wkv, gq, gk, qw, kw, krw, layer):
    t = proj.shape[0]
    tm = 1024
    hw = MLA_HEADS * HEAD_PAD
    lb = lambda a: _layer_block(a, layer)
    return pl.pallas_call(
        _mla_prep_kernel,
        grid=(t // tm,),
        in_specs=[pl.BlockSpec((tm, LAT_W), lambda i: (i, P_LAT // LAT_W)),
                  pl.BlockSpec((tm, HEAD_PAD), lambda i: (i, 0)),
                  pl.BlockSpec((tm, HEAD_PAD), lambda i: (i, 0)),
                  lb(qan), lb(kvan), lb(wq), lb(wkv),
                  _const_block(gq), _const_block(gk),
                  lb(qw), lb(kw), lb(krw)],
        out_specs=[pl.BlockSpec((tm, hw), lambda i: (i, 0))] * 3,
        out_shape=[jax.ShapeDtypeStruct((t, hw), BF16)] * 3,
        compiler_params=_cparams(("arbitrary",)),
        name="mla_prep",
    )(proj, cos_t, sin_t, qan, kvan, wq, wkv, gq, gk, qw, kw, krw)


ATT_TILE = 512


def _attn_kernel(q_ref, k_ref, v_ref, o_ref):
    seq = q_ref.shape[0]
    tq = ATT_TILE
    hq = tq // 2
    lane = lax.broadcasted_iota(jnp.int32, (tq, HEAD_PAD), 1)
    mask_a = (lax.broadcasted_iota(jnp.int32, (tq, hq), 1)
              <= lax.broadcasted_iota(jnp.int32, (tq, hq), 0))
    mask_b = (lax.broadcasted_iota(jnp.int32, (hq, hq), 1)
              <= lax.broadcasted_iota(jnp.int32, (hq, hq), 0))
    nt = (((1,), (1,)), ((), ()))
    heads = (slice(0, HEAD_PAD), slice(HEAD_PAD, 2 * HEAD_PAD))

    def block(state, q, k0, width, hs, mask):
        m, acc = state
        s = lax.dot_general(q, k_ref[k0:k0 + width, hs], nt, preferred_element_type=F32)
        if mask is not None:
            s = jnp.where(mask, s, NEG_BIG)
        m_new = jnp.maximum(m, jnp.max(s, axis=-1, keepdims=True))
        p = jnp.exp((s - m_new).astype(BF16))
        acc = jnp.exp(m - m_new) * acc + jnp.dot(p, v_ref[k0:k0 + width, hs],
                                                 preferred_element_type=F32)
        return m_new, acc

    for qi in range(seq // tq):
        q0 = qi * tq
        qs = [q_ref[q0:q0 + tq, hs] for hs in heads]
        init = (jnp.full((tq, 1), NEG_BIG, F32), jnp.zeros((tq, HEAD_PAD), F32))
        states = [init, init]
        for j in range(qi):
            states = [block(st, q, j * tq, tq, hs, None) for st, q, hs in zip(states, qs, heads)]
        states = [block(st, q, q0, hq, hs, mask_a) for st, q, hs in zip(states, qs, heads)]
        outs = []
        for (m, acc), q, hs in zip(states, qs, heads):
            m_lo, acc_lo = block((m[hq:], acc[hq:]), q[hq:], q0 + hq, hq, hs, mask_b)
            acc = jnp.concatenate([acc[:hq], acc_lo], axis=0)
            denom = jnp.where(lane < V_HEAD, pltpu.roll(acc, V_HEAD, 1), 1.0)
            outs.append(acc / denom)
        o_ref[q0:q0 + tq, :] = jnp.where(lane < V_HEAD, outs[0],
                                         pltpu.roll(outs[1], V_HEAD, 1)).astype(BF16)


def _attention(q, k, v, batch, seq):
    t = q.shape[0]
    pairs = MLA_HEADS // 2
    blk = pl.BlockSpec((seq, 2 * HEAD_PAD), lambda b, p: (b, p))
    return pl.pallas_call(
        _attn_kernel,
        grid=(batch, pairs),
        in_specs=[blk, blk, blk],
        out_specs=pl.BlockSpec((seq, 2 * V_HEAD), lambda b, p: (b, p)),
        out_shape=jax.ShapeDtypeStruct((t, MLA_OUT), BF16),
        compiler_params=_cparams(("arbitrary", "arbitrary")),
        name="mla_attention",
    )(q, k, v)


POOL_LEVELS = 4


def _pool_rows(u_ref, pw_ref, ps_ref, car_ref, j):
    ts = u_ref.shape[0]

    @pl.when(j == 0)
    def _():
        car_ref[...] = jnp.zeros(car_ref.shape, F32)

    t_pos = j * ts + lax.broadcasted_iota(jnp.int32, (ts, 1), 0)
    outs = []
    for g, w in enumerate(POOL_WINDOWS):
        gs = slice(g * POOL_GROUP, (g + 1) * POOL_GROUP)
        ug = u_ref[:, gs].astype(F32)
        acc = ug
        span = 1
        level = 0
        while span < w:
            prev = car_ref[level, :, gs]
            car_ref[level, :, gs] = acc[ts - SUBLANES:, :]
            acc = acc + _shift_rows(acc, prev, span)
            span *= 2
            level += 1
        count = jnp.minimum(t_pos + 1, w).astype(F32)
        pooled = acc / count - ug
        mixed = jnp.dot(pooled.astype(BF16), pw_ref[0, g], preferred_element_type=F32)
        outs.append((mixed * ps_ref[0, :, gs]).astype(BF16))
    return jnp.concatenate(outs, axis=1)


SSD_STEP_CHUNKS = 8


def _ssd_kernel(zs_ref, xs_ref, bc_ref, dt_ref, dtb_ref, alog_ref, dful_ref, nw_ref,
                exp_ref, tri_ref, o_ref, state_ref, gated_ref):
    lc = SSD_CHUNK

    @pl.when(pl.program_id(1) == 0)
    def _():
        state_ref[...] = jnp.zeros(state_ref.shape, F32)

    rows = lax.broadcasted_iota(jnp.int32, (lc, lc), 0)
    cols = lax.broadcasted_iota(jnp.int32, (lc, lc), 1)
    causal = cols <= rows
    lane = lax.broadcasted_iota(jnp.int32, (lc, 2 * SSD_HEAD_DIM), 1)
    nt = (((1,), (1,)), ((), ()))
    tn = (((0,), (0,)), ((), ()))
    heads_per_group = SSD_HEADS // SSD_GROUPS
    neg_a = -jnp.exp(alog_ref[0])
    dtb = dtb_ref[0]
    dful = dful_ref[0]
    nw = nw_ref[0]

    for r0 in range(0, xs_ref.shape[0], lc):
        rs = slice(r0, r0 + lc)
        xs = _silu(xs_ref[rs, :].astype(F32))
        bc = _silu(bc_ref[rs, :].astype(F32)).astype(BF16)
        bm = bc[:, :SSD_GROUPS * SSD_STATE]
        cm = bc[:, SSD_GROUPS * SSD_STATE:]

        v = dt_ref[rs, :] + dtb
        dt = jnp.maximum(v, 0.0) + jnp.log(1.0 + jnp.exp(-jnp.abs(v)))
        a_cs = _split_dot_left(tri_ref[...], dt * neg_a)
        a_last = a_cs[lc - 1:lc, :]
        ea = jnp.exp(a_cs)
        stacked = jnp.concatenate([dt, ea, jnp.exp(a_last - a_cs)], axis=0).astype(BF16)
        full = jnp.dot(stacked, exp_ref[...], preferred_element_type=F32)
        dt_full = full[0:lc]
        ea_full = full[lc:2 * lc]
        ds_full = full[2 * lc:3 * lc]
        chunk_decay = _split_dot(ea[lc - SUBLANES:], exp_ref[...])[SUBLANES - 1:SUBLANES, :]
        a_cs_t = a_cs.T

        xdt = xs * dt_full
        xdt_b = xdt.astype(BF16)
        xds_b = (xdt * ds_full).astype(BF16)

        for g in range(SSD_GROUPS):
            gs = slice(g * GROUP_W, (g + 1) * GROUP_W)
            bg = bm[:, g * SSD_STATE:(g + 1) * SSD_STATE]
            cg = cm[:, g * SSD_STATE:(g + 1) * SSD_STATE]
            cb = lax.dot_general(cg, bg, nt, preferred_element_type=F32)
            state = state_ref[g]
            y_off = jnp.dot(cg, state.astype(BF16), preferred_element_type=F32) * ea_full[:, gs]
            s_new = lax.dot_general(bg, xds_b[:, gs], tn, preferred_element_type=F32)
            state_ref[g] = state * chunk_decay[:, gs] + s_new
            for pp in range(heads_per_group // 2):
                h0 = g * heads_per_group + 2 * pp
                ps = slice(h0 * SSD_HEAD_DIM, (h0 + 2) * SSD_HEAD_DIM)
                ys = []
                for hd in (h0, h0 + 1):
                    seg = a_cs[:, hd:hd + 1] - a_cs_t[hd:hd + 1, :]
                    decay = jnp.exp(jnp.where(causal, seg, NEG_BIG))
                    ys.append(jnp.dot((cb * decay).astype(BF16), xdt_b[:, ps],
                                      preferred_element_type=F32))
                y_pair = jnp.where(lane < SSD_HEAD_DIM, ys[0], ys[1])
                ls = slice(2 * pp * SSD_HEAD_DIM, (2 * pp + 2) * SSD_HEAD_DIM)
                y = y_pair + y_off[:, ls] + xs[:, ps] * dful[:, ps]
                gated_ref[rs, ps] = y * zs_ref[rs, ps].astype(F32)

        for g in range(SSD_GROUPS):
            gs = slice(g * GROUP_W, (g + 1) * GROUP_W)
            gt = gated_ref[rs, gs]
            ms = jnp.mean(gt * gt, axis=-1, keepdims=True)
            o_ref[rs, gs] = (gt * lax.rsqrt(ms + EPS) * nw[:, gs]).astype(o_ref.dtype)


def _ssd(proj, dt, dtb, alog, dful, nw, expand, tri, layer, batch, seq):
    t = proj.shape[0]
    lc = SSD_STEP_CHUNKS * SSD_CHUNK
    nc = seq // lc
    lb = lambda a: _layer_block(a, layer)
    return pl.pallas_call(
        _ssd_kernel,
        grid=(batch, nc),
        in_specs=[pl.BlockSpec((lc, SSD_INNER), lambda b, j: (b * nc + j, P_Z // SSD_INNER)),
                  pl.BlockSpec((lc, SSD_INNER), lambda b, j: (b * nc + j, P_XS // SSD_INNER)),
                  pl.BlockSpec((lc, BC_W), lambda b, j: (b * nc + j, P_BC // BC_W)),
                  pl.BlockSpec((lc, DT_PAD), lambda b, j: (b * nc + j, 0)),
                  lb(dtb), lb(alog), lb(dful), lb(nw),
                  _const_block(expand), _const_block(tri)],
        out_specs=pl.BlockSpec((lc, SSD_INNER), lambda b, j: (b * nc + j, 0)),
        out_shape=jax.ShapeDtypeStruct((t, SSD_INNER), BF16),
        scratch_shapes=[pltpu.VMEM((SSD_GROUPS, SSD_STATE, GROUP_W), F32),
                        pltpu.VMEM((lc, SSD_INNER), F32)],
        compiler_params=_cparams(("arbitrary", "arbitrary")),
        name="ssd_branch",
    )(proj, proj, proj, dt, dtb, alog, dful, nw, expand, tri)


def _merge_kernel(x_ref, ga_ref, gb_ref, gc_ref, oa_ref, u_ref, oc_ref, mod_ref,
                  wb_ref, wo_ref, nw_ref, pw_ref, ps_ref, x_out, h_out, car_ref):
    m = mod_ref[0, 0]
    o_b = _pool_rows(u_ref, pw_ref, ps_ref, car_ref, pl.program_id(1))
    ya = jnp.dot(oa_ref[...], wb_ref[0, 0:MLA_OUT, :], preferred_element_type=F32)
    yb = jnp.dot(o_b, wb_ref[0, MLA_OUT:MLA_OUT + POOL_DIM, :], preferred_element_type=F32)
    yc = jnp.dot(oc_ref[...], wb_ref[0, MLA_OUT + POOL_DIM:, :], preferred_element_type=F32)
    merged = (_sigmoid(ga_ref[...].astype(F32)) * ya
              + _sigmoid(gb_ref[...].astype(F32)) * yb
              + _sigmoid(gc_ref[...].astype(F32)) * yc)
    y = jnp.dot(merged.astype(BF16), wo_ref[0], preferred_element_type=F32)
    x1 = x_ref[...] + m[2:3] * y
    x_out[...] = x1
    h_out[...] = _norm_mod(x1, nw_ref[0], m[3:4], m[4:5]).astype(BF16)


def _merge(xf, proj, o_a, o_c, mods, wb, wo, norm_w, pool_w, pool_scale, layer, batch, seq):
    t, d = xf.shape
    tm = 1024
    per_seq = seq // tm
    col = lambda n, blk: pl.BlockSpec((tm, n), lambda b, j: (b * per_seq + j, blk))
    tile = lambda n: col(n, 0)
    return pl.pallas_call(
        _merge_kernel,
        grid=(batch, per_seq),
        in_specs=[tile(d), col(d, P_GATE // d), col(d, P_GATE // d + 1), col(d, P_GATE // d + 2),
                  tile(MLA_OUT), col(POOL_DIM, P_POOL // POOL_DIM), tile(SSD_INNER),
                  pl.BlockSpec((1, 1, 6, d), lambda b, j: (layer, b, 0, 0)),
                  _layer_block(wb, layer), _layer_block(wo, layer), _layer_block(norm_w, layer),
                  _layer_block(pool_w, layer), _layer_block(pool_scale, layer)],
        out_specs=[tile(d), tile(d)],
        out_shape=[jax.ShapeDtypeStruct((t, d), F32), jax.ShapeDtypeStruct((t, d), BF16)],
        scratch_shapes=[pltpu.VMEM((POOL_LEVELS, SUBLANES, POOL_DIM), F32)],
        compiler_params=_cparams(("arbitrary", "arbitrary")),
        name="merge_out",
    )(xf, proj, proj, proj, o_a, proj, o_c, mods, wb, wo, norm_w, pool_w, pool_scale)


def _ffn_kernel(with_next, h_ref, x_ref, mod_ref, *rest):
    if with_next:
        modn_ref, nwn_ref, upw_ref, cw_ref, cb_ref, dw_ref, x_out, h_out, carry_ref, act_ref = rest
    else:
        upw_ref, cw_ref, cb_ref, dw_ref, x_out, carry_ref, act_ref = rest
    ts = h_ref.shape[0]

    @pl.when(pl.program_id(1) == 0)
    def _():
        carry_ref[...] = jnp.zeros(carry_ref.shape, F32)

    h = h_ref[...]

    def conv_half(cs):
        u = jnp.dot(h, upw_ref[0, :, cs], preferred_element_type=F32)
        prev = carry_ref[:, cs]
        carry_ref[:, cs] = u[ts - CONV_HALO:, :]
        return _causal_conv(u, prev, cw_ref[0, :, cs], cb_ref[0, :, cs], FFN_CONV)

    for c in range(FFN_NCHUNK):
        gate = conv_half(slice(c * FFN_CHUNK, (c + 1) * FFN_CHUNK))
        val = conv_half(slice(FFN_DIM + c * FFN_CHUNK, FFN_DIM + (c + 1) * FFN_CHUNK))
        act_ref[c] = (_silu(gate) * val).astype(BF16)
    y = jnp.dot(act_ref[0], dw_ref[0, 0:FFN_CHUNK, :], preferred_element_type=F32)
    for c in range(1, FFN_NCHUNK):
        y = y + jnp.dot(act_ref[c], dw_ref[0, c * FFN_CHUNK:(c + 1) * FFN_CHUNK, :],
                        preferred_element_type=F32)
    m = mod_ref[0, 0]
    x2 = x_ref[...] + m[5:6] * y
    x_out[...] = x2
    if with_next:
        mn = modn_ref[0, 0]
        h_out[...] = _norm_mod(x2, nwn_ref[0], mn[0:1], mn[1:2]).astype(BF16)


def _ffn(h2, x1, mods, upw, cw, cb, dw, norm1_w, layer, seq):
    t, d = x1.shape
    ts = 512
    per_seq = seq // ts
    batch = t // seq
    with_next = layer + 1 < mods.shape[0]
    tile = pl.BlockSpec((ts, d), lambda b, j: (b * per_seq + j, 0))
    modspec = lambda l: pl.BlockSpec((1, 1, 6, d), lambda b, j: (l, b, 0, 0))
    in_specs = [tile, tile, modspec(layer)]
    args = [h2, x1, mods]
    out_specs = [tile]
    out_shape = [jax.ShapeDtypeStruct((t, d), F32)]
    if with_next:
        in_specs += [modspec(layer + 1), _layer_block(norm1_w, layer + 1)]
        args += [mods, norm1_w]
        out_specs.append(tile)
        out_shape.append(jax.ShapeDtypeStruct((t, d), BF16))
    in_specs += [_layer_block(a, layer) for a in (upw, cw, cb, dw)]
    args += [upw, cw, cb, dw]
    outs = pl.pallas_call(
        functools.partial(_ffn_kernel, with_next),
        grid=(batch, per_seq),
        in_specs=in_specs,
        out_specs=out_specs,
        out_shape=out_shape,
        scratch_shapes=[pltpu.VMEM((CONV_HALO, 2 * FFN_DIM), F32),
                        pltpu.VMEM((FFN_NCHUNK, ts, FFN_CHUNK), BF16)],
        compiler_params=_cparams(("arbitrary", "arbitrary")),
        name="conv_mlp",
    )(*args)
    return (outs[0], outs[1]) if with_next else (outs[0], None)


def _relayout_kernel(pieces, zero_rows, w_ref, o_ref):
    for src0, width, dst0 in pieces:
        o_ref[0, dst0:dst0 + width, :] = w_ref[0, src0:src0 + width, :].astype(BF16)
    for dst0, width in zero_rows:
        o_ref[0, dst0:dst0 + width, :] = jnp.zeros((width, o_ref.shape[2]), BF16)


def _relayout_w_in(w_t, pieces, zero_rows):
    depth, n_in, d = w_t.shape
    tk = 256
    return pl.pallas_call(
        functools.partial(_relayout_kernel, pieces, zero_rows),
        grid=(depth, d // tk),
        in_specs=[pl.BlockSpec((1, n_in, tk), lambda l, i: (l, 0, i))],
        out_specs=pl.BlockSpec((1, PROJ_W, tk), lambda l, i: (l, 0, i)),
        out_shape=jax.ShapeDtypeStruct((depth, PROJ_W, d), BF16),
        compiler_params=_cparams(("arbitrary", "arbitrary")),
        name="w_in_relayout",
    )(w_t)


def _head_blocks(w, per_head, lo, hi):
    depth, kdim, _ = w.shape
    w = w.reshape(depth, kdim, MLA_HEADS, per_head)[..., lo:hi]
    w = jnp.pad(w, ((0, 0), (0, 0), (0, 0), (0, HEAD_PAD - (hi - lo))))
    return w.reshape(depth, kdim, MLA_HEADS * HEAD_PAD)


def _group_mean_matrix(sizes):
    blk = np.zeros((HEAD_PAD, HEAD_PAD), np.float32)
    start = 0
    for n in sizes:
        blk[start:start + n, start:start + n] = 1.0 / n
        start += n
    return jnp.asarray(np.kron(np.eye(2, dtype=np.float32), blk), BF16)


def _rows(v, pad_to=None):
    if pad_to is not None:
        v = jnp.pad(v, ((0, 0), (0, pad_to - v.shape[1])))
    return v[:, None, :]


def kernel(x, c, positions, ada_w, ada_b, norm1_w, w_in, q_a_norm, w_q_b, kv_a_norm, w_kv_b, q_norm, k_norm, pool_w, pool_scale, ssd_conv_w, ssd_conv_b, ssd_dt_bias, ssd_a_log, ssd_d, ssd_norm_w, w_branch, w_out, norm2_w, ffn_up, ffn_conv_w, ffn_conv_b, ffn_down):
    batch, seq, d = x.shape
    depth = ada_w.shape[0]
    t = batch * seq
    xf = x.reshape(t, d)

    mods = _mods(c, ada_w, ada_b).reshape(depth, batch, 6, d)

    gq = _group_mean_matrix((QK_NOPE, QK_ROPE))
    gk = _group_mean_matrix((QK_NOPE,))
    head_of_col = np.arange(SSD_INNER) // SSD_HEAD_DIM
    expand = jnp.asarray(np.arange(DT_PAD)[:, None] == head_of_col[None, :], BF16)
    tri = jnp.asarray(np.tril(np.ones((SSD_CHUNK, SSD_CHUNK), np.float32)), BF16)

    o = 0
    offs = {}
    for name, size in (("q_lat", Q_LORA), ("c_kv", KV_LORA), ("k_rope", QK_ROPE), ("pool", POOL_DIM),
                       ("z", SSD_INNER), ("xs", SSD_INNER), ("bc", BC_W),
                       ("dt", SSD_HEADS), ("gate", N_BRANCH * D_MODEL)):
        offs[name] = (o, o + size)
        o += size
    pieces = []
    for name, dst0 in (("z", P_Z), ("xs", P_XS), ("gate", P_GATE), ("pool", P_POOL), ("bc", P_BC),
                       ("q_lat", P_LAT), ("c_kv", P_LAT + Q_LORA)):
        pieces.append((offs[name][0], offs[name][1] - offs[name][0], dst0))
    tail0 = PROJ_W - HEAD_PAD
    pieces.append((offs["dt"][0], SSD_HEADS, tail0))
    pieces.append((offs["k_rope"][0], QK_ROPE, tail0 + QK_NOPE))
    zero_rows = ((tail0 + SSD_HEADS, QK_NOPE - SSD_HEADS), (tail0 + QK_HEAD, HEAD_PAD - QK_HEAD))
    w_main = _relayout_w_in(jnp.swapaxes(w_in, 1, 2), tuple(pieces), zero_rows)

    wq = _head_blocks(w_q_b, QK_HEAD, 0, QK_HEAD).astype(BF16)
    wkv = jnp.concatenate([_head_blocks(w_kv_b, QK_NOPE + V_HEAD, 0, QK_NOPE),
                           _head_blocks(w_kv_b, QK_NOPE + V_HEAD, QK_NOPE, QK_NOPE + V_HEAD)],
                          axis=2).astype(BF16)
    hw = MLA_HEADS * HEAD_PAD
    qw = jnp.tile(_rows(q_norm, HEAD_PAD), (1, 1, MLA_HEADS)) * (QK_HEAD ** -0.5)
    kw = jnp.tile(_rows(k_norm[:, :QK_NOPE], HEAD_PAD), (1, 1, MLA_HEADS))
    krw = _rows(jnp.pad(k_norm[:, QK_NOPE:], ((0, 0), (QK_NOPE, 0))), HEAD_PAD)
    qan = _rows(q_a_norm)
    kvan = _rows(kv_a_norm)

    pool_wb = pool_w.astype(BF16)
    pool_sc = _rows(pool_scale)
    conv_b = _rows(ssd_conv_b)
    dtb = _rows(ssd_dt_bias, DT_PAD)
    alog = _rows(ssd_a_log, DT_PAD)
    dful = _rows(jnp.repeat(ssd_d, SSD_HEAD_DIM, axis=1))
    ssd_nw = _rows(ssd_norm_w)
    wb = w_branch.astype(BF16)
    wo = w_out.astype(BF16)
    n1 = _rows(norm1_w)
    n2 = _rows(norm2_w)
    upw = ffn_up.astype(BF16)
    dw = ffn_down.astype(BF16)
    fcb = _rows(ffn_conv_b)

    h, cos_t, sin_t = _prenorm(xf, mods, n1, positions, 0, seq)
    for l in range(depth):
        proj, dt = _inproj(h, w_main, ssd_conv_w, conv_b, l, batch, seq)
        q, k, v = _mla_prep(proj, cos_t, sin_t, qan, kvan, wq, wkv, gq, gk, qw, kw, krw, l)
        o_a = _attention(q, k, v, batch, seq)
        o_c = _ssd(proj, dt, dtb, alog, dful, ssd_nw, expand, tri, l, batch, seq)
        x1, h2 = _merge(xf, proj, o_a, o_c, mods, wb, wo, n2, pool_wb, pool_sc, l, batch, seq)
        xf, h = _ffn(h2, x1, mods, upw, ffn_conv_w, fcb, dw, n1, l, seq)
    return xf.reshape(batch, seq, d)
```

```python
import functools

import jax
import jax.numpy as jnp
import numpy as np
from jax import lax
from jax.experimental import pallas as pl
from jax.experimental.pallas import tpu as pltpu

F32 = jnp.float32
BF16 = jnp.bfloat16

D_MODEL = 1024
MLA_HEADS = 8
QK_NOPE = 64
QK_ROPE = 32
QK_HEAD = QK_NOPE + QK_ROPE
V_HEAD = 64
Q_LORA = 384
KV_LORA = 256
ROPE_THETA = 10000.0
MLA_OUT = MLA_HEADS * V_HEAD
POOL_WINDOWS = (2, 4, 8, 16)
POOL_GROUP = 128
POOL_DIM = len(POOL_WINDOWS) * POOL_GROUP
SSD_HEADS = 16
SSD_HEAD_DIM = 64
SSD_INNER = SSD_HEADS * SSD_HEAD_DIM
SSD_GROUPS = 2
SSD_STATE = 128
SSD_CONV = 4
SSD_CHUNK = 128
FFN_DIM = 2816
FFN_CONV = 3
N_BRANCH = 3
EPS = 1e-6

SUBLANES = 8
HEAD_PAD = 128
NEG_BIG = -1e30
CONV_HALO = SUBLANES

BC_W = 2 * SSD_GROUPS * SSD_STATE
GROUP_W = SSD_INNER // SSD_GROUPS
P_Z = 0
P_XS = P_Z + SSD_INNER
P_GATE = P_XS + SSD_INNER
P_POOL = P_GATE + N_BRANCH * D_MODEL
P_BC = P_POOL + POOL_DIM
P_LAT = P_BC + BC_W
LAT_W = Q_LORA + KV_LORA + HEAD_PAD
PROJ_W = P_LAT + LAT_W
DT_PAD = HEAD_PAD

FFN_CHUNK = 256
FFN_NCHUNK = FFN_DIM // FFN_CHUNK

VMEM_LIMIT = 56 * 1024 * 1024


def _cparams(sem):
    return pltpu.CompilerParams(dimension_semantics=sem, vmem_limit_bytes=VMEM_LIMIT)


def _const_block(a):
    nd = a.ndim
    return pl.BlockSpec(a.shape, lambda *_: (0,) * nd, pipeline_mode=pl.Buffered(1))


def _layer_block(a, layer):
    nd = a.ndim
    return pl.BlockSpec((1,) + a.shape[1:], lambda *_: (layer,) + (0,) * (nd - 1),
                        pipeline_mode=pl.Buffered(1))


def _sigmoid(v):
    return 0.5 * jnp.tanh(0.5 * v) + 0.5


def _silu(v):
    half = 0.5 * v
    return half * jnp.tanh(half) + half


def _norm_mod(x, w, shift, scale):
    y = x * lax.rsqrt(jnp.mean(x * x, axis=-1, keepdims=True) + EPS)
    return (y * w) * (1.0 + scale) + shift


def _shift_rows(u, prev, k):
    full = pltpu.roll(u, k, 0)
    row = lax.broadcasted_iota(jnp.int32, prev.shape, 0)
    first = jnp.where(row < k, pltpu.roll(prev, k, 0), full[:SUBLANES])
    return jnp.concatenate([first, full[SUBLANES:]], axis=0)


def _causal_conv(u, prev, w, b, taps):
    out = w[taps - 1:taps, :] * u + b
    for k in range(taps - 1):
        out = out + w[k:k + 1, :] * _shift_rows(u, prev, taps - 1 - k)
    return out


def _split(x):
    hi = x.astype(BF16)
    return hi, (x - hi.astype(F32)).astype(BF16)


def _split_dot(x, m):
    hi, lo = _split(x)
    return jnp.dot(hi, m, preferred_element_type=F32) + jnp.dot(lo, m, preferred_element_type=F32)


def _split_dot_left(m, x):
    hi, lo = _split(x)
    return jnp.dot(m, hi, preferred_element_type=F32) + jnp.dot(m, lo, preferred_element_type=F32)


def _mods_kernel(c_ref, w_ref, b_ref, o_ref):
    ca = _silu(c_ref[...]).astype(BF16)
    w = w_ref[0].astype(BF16)
    o_ref[0] = jnp.dot(ca, w, preferred_element_type=F32) + b_ref[0]


def _mods(c, ada_w, ada_b):
    depth, d, n = ada_w.shape
    b = c.shape[0]
    tn = 1536
    return pl.pallas_call(
        _mods_kernel,
        grid=(depth, n // tn),
        in_specs=[pl.BlockSpec((b, d), lambda l, j: (0, 0)),
                  pl.BlockSpec((1, d, tn), lambda l, j: (l, 0, j)),
                  pl.BlockSpec((1, 1, tn), lambda l, j: (l, 0, j))],
        out_specs=pl.BlockSpec((1, b, tn), lambda l, j: (l, 0, j)),
        out_shape=jax.ShapeDtypeStruct((depth, b, n), F32),
        compiler_params=_cparams(("arbitrary", "arbitrary")),
        name="adaln_mods",
    )(c, ada_w, ada_b.reshape(depth, 1, n))


def _rope_rows(positions):
    half = QK_ROPE // 2
    inv_freq = ROPE_THETA ** (-jnp.arange(0, QK_ROPE, 2, dtype=F32) / QK_ROPE)
    zeros = jnp.zeros((QK_NOPE,), F32)
    tail = jnp.zeros((HEAD_PAD - QK_HEAD,), F32)
    freq = jnp.concatenate([zeros, inv_freq, inv_freq, tail]).reshape(1, HEAD_PAD)
    sign = np.concatenate([np.zeros(QK_NOPE), -np.ones(half), np.ones(half),
                           np.zeros(HEAD_PAD - QK_HEAD)]).astype(np.float32).reshape(1, HEAD_PAD)
    return positions.astype(F32).reshape(positions.size, 1), freq, jnp.asarray(sign)


def _prenorm_kernel(x_ref, mod_ref, w_ref, pos_ref, freq_ref, sign_ref, h_ref, c_ref, s_ref):
    m = mod_ref[0, 0]
    h_ref[...] = _norm_mod(x_ref[...], w_ref[0], m[0:1], m[1:2]).astype(BF16)
    ang = pos_ref[...] * freq_ref[...]
    c_ref[...] = jnp.cos(ang)
    s_ref[...] = jnp.sin(ang) * sign_ref[...]


def _prenorm(xf, mods, norm_w, positions, layer, seq):
    t, d = xf.shape
    tm = 1024
    per_seq = seq // tm
    pos, freq, sign = _rope_rows(positions)
    row = pl.BlockSpec((1, HEAD_PAD), lambda i: (0, 0))
    tab = pl.BlockSpec((tm, HEAD_PAD), lambda i: (i, 0))
    return pl.pallas_call(
        _prenorm_kernel,
        grid=(t // tm,),
        in_specs=[pl.BlockSpec((tm, d), lambda i: (i, 0)),
                  pl.BlockSpec((1, 1, 6, d), lambda i: (layer, i // per_seq, 0, 0)),
                  _layer_block(norm_w, layer),
                  pl.BlockSpec((tm, 1), lambda i: (i, 0)), row, row],
        out_specs=[pl.BlockSpec((tm, d), lambda i: (i, 0)), tab, tab],
        out_shape=[jax.ShapeDtypeStruct((t, d), BF16)] + [jax.ShapeDtypeStruct((t, HEAD_PAD), F32)] * 2,
        compiler_params=_cparams(("arbitrary",)),
        name="prenorm_rope",
    )(xf, mods, norm_w, pos, freq, sign)


INPROJ_COLS = 512


def _inproj_kernel(h_ref, w_ref, cw_ref, cb_ref, proj_ref, dt_ref, car_ref):
    tm = h_ref.shape[0]
    wc = INPROJ_COLS

    @pl.when(pl.program_id(1) == 0)
    def _():
        car_ref[...] = jnp.zeros(car_ref.shape, F32)

    h = h_ref[...]

    def mm(c0, width):
        return lax.dot_general(h, w_ref[0, c0:c0 + width, :], (((1,), (1,)), ((), ())),
                               preferred_element_type=F32)

    for rel in range(0, SSD_INNER, wc):
        proj_ref[:, P_Z + rel:P_Z + rel + wc] = _silu(mm(P_Z + rel, wc)).astype(BF16)
    for rel in range(0, SSD_INNER + BC_W, wc):
        c0 = P_XS + rel if rel < SSD_INNER else P_BC + rel - SSD_INNER
        cs = slice(rel, rel + wc)
        u = mm(c0, wc)
        prev = car_ref[:, cs]
        car_ref[:, cs] = u[tm - CONV_HALO:, :]
        conv = _causal_conv(u, prev, cw_ref[0, :, cs], cb_ref[0, :, cs], SSD_CONV)
        proj_ref[:, c0:c0 + wc] = conv.astype(BF16)
    for c0 in range(P_GATE, P_BC, wc):
        proj_ref[:, c0:c0 + wc] = mm(c0, wc).astype(BF16)
    lat = mm(P_LAT, LAT_W)
    proj_ref[:, P_LAT:] = lat.astype(BF16)
    dt_ref[...] = lat[:, LAT_W - DT_PAD:]


def _inproj(h, w_main, conv_w, conv_b, layer, batch, seq):
    t, d = h.shape
    tm = 1024
    per_seq = seq // tm
    rows = lambda n: pl.BlockSpec((tm, n), lambda b, j: (b * per_seq + j, 0))
    return pl.pallas_call(
        _inproj_kernel,
        grid=(batch, per_seq),
        in_specs=[rows(d), _layer_block(w_main, layer), _layer_block(conv_w, layer),
                  _layer_block(conv_b, layer)],
        out_specs=[rows(PROJ_W), rows(DT_PAD)],
        out_shape=[jax.ShapeDtypeStruct((t, PROJ_W), BF16),
                   jax.ShapeDtypeStruct((t, DT_PAD), F32)],
        scratch_shapes=[pltpu.VMEM((CONV_HALO, SSD_INNER + BC_W), F32)],
        compiler_params=_cparams(("arbitrary", "arbitrary")),
        name="in_projection",
    )(h, w_main, conv_w, conv_b)


def _rope_apply(y, cos, sin_signed, lane):
    half = QK_ROPE // 2
    partner = jnp.where(lane < QK_NOPE + half,
                        pltpu.roll(y, HEAD_PAD - half, 1),
                        pltpu.roll(y, half, 1))
    return y * cos + partner * sin_signed


def _mla_prep_kernel(lat_ref, cos_ref, sin_ref, qan_ref, kvan_ref, wq_ref, wkv_ref,
                     gq_ref, gk_ref, qw_ref, kw_ref, krw_ref, q_ref, k_ref, v_ref):
    kw = MLA_HEADS * HEAD_PAD
    pair = 2 * HEAD_PAD
    qw = qw_ref[0]
    kwt = kw_ref[0]
    vlane = lax.broadcasted_iota(jnp.int32, (1, kw), 1)
    lane = lax.broadcasted_iota(jnp.int32, (MLA_SUB, HEAD_PAD), 1)

    for r0 in range(0, lat_ref.shape[0], MLA_SUB):
        rs = slice(r0, r0 + MLA_SUB)
        lat = lat_ref[rs, :].astype(F32)
        q_lat = lat[:, :Q_LORA]
        c_kv = lat[:, Q_LORA:Q_LORA + KV_LORA]
        cos = cos_ref[rs, :]
        sin = sin_ref[rs, :]
        k_rope = jnp.where((lane >= QK_NOPE) & (lane < QK_HEAD), lat[:, Q_LORA + KV_LORA:], 0.0)

        qn = q_lat * lax.rsqrt(jnp.mean(q_lat * q_lat, axis=-1, keepdims=True) + EPS) * qan_ref[0]
        cn = c_kv * lax.rsqrt(jnp.mean(c_kv * c_kv, axis=-1, keepdims=True) + EPS) * kvan_ref[0]
        q = jnp.dot(qn.astype(BF16), wq_ref[0], preferred_element_type=F32)
        kv = jnp.dot(cn.astype(BF16), wkv_ref[0], preferred_element_type=F32)
        v_ref[rs, :] = jnp.where(vlane % HEAD_PAD < V_HEAD, kv[:, kw:], 1.0).astype(BF16)

        kr_ms = jnp.sum(k_rope * k_rope, axis=-1, keepdims=True) * (1.0 / QK_ROPE)
        kr = _rope_apply(k_rope * lax.rsqrt(kr_ms + EPS) * krw_ref[0], cos, sin, lane)

        for c0 in range(0, kw, pair):
            xq = q[:, c0:c0 + pair]
            ms = jnp.dot((xq * xq).astype(BF16), gq_ref[...], preferred_element_type=F32)
            yq = xq * lax.rsqrt(ms + EPS) * qw[:, c0:c0 + pair]
            xk = kv[:, c0:c0 + pair]
            ms = jnp.dot((xk * xk).astype(BF16), gk_ref[...], preferred_element_type=F32)
            yk = xk * lax.rsqrt(ms + EPS) * kwt[:, c0:c0 + pair]
            for h0 in range(0, pair, HEAD_PAD):
                q_ref[rs, c0 + h0:c0 + h0 + HEAD_PAD] = _rope_apply(
                    yq[:, h0:h0 + HEAD_PAD], cos, sin, lane).astype(BF16)
                k_ref[rs, c0 + h0:c0 + h0 + HEAD_PAD] = (yk[:, h0:h0 + HEAD_PAD] + kr).astype(BF16)


MLA_SUB = 256


def _mla_prep(proj, cos_t, sin_t, qan, kvan, wq, wkv, gq, gk, qw, kw, krw, layer):
    t = proj.shape[0]
    tm = 1024
    hw = MLA_HEADS * HEAD_PAD
    lb = lambda a: _layer_block(a, layer)
    return pl.pallas_call(
        _mla_prep_kernel,
        grid=(t // tm,),
        in_specs=[pl.BlockSpec((tm, LAT_W), lambda i: (i, P_LAT // LAT_W)),
                  pl.BlockSpec((tm, HEAD_PAD), lambda i: (i, 0)),
                  pl.BlockSpec((tm, HEAD_PAD), lambda i: (i, 0)),
                  lb(qan), lb(kvan), lb(wq), lb(wkv),
                  _const_block(gq), _const_block(gk),
                  lb(qw), lb(kw), lb(krw)],
        out_specs=[pl.BlockSpec((tm, hw), lambda i: (i, 0))] * 3,
        out_shape=[jax.ShapeDtypeStruct((t, hw), BF16)] * 3,
        compiler_params=_cparams(("arbitrary",)),
        name="mla_prep",
    )(proj, cos_t, sin_t, qan, kvan, wq, wkv, gq, gk, qw, kw, krw)


ATT_TILE = 512


def _attn_kernel(q_ref, k_ref, v_ref, o_ref):
    seq = q_ref.shape[0]
    tq = ATT_TILE
    hq = tq // 2
    lane = lax.broadcasted_iota(jnp.int32, (tq, HEAD_PAD), 1)
    mask_a = (lax.broadcasted_iota(jnp.int32, (tq, hq), 1)
              <= lax.broadcasted_iota(jnp.int32, (tq, hq), 0))
    mask_b = (lax.broadcasted_iota(jnp.int32, (hq, hq), 1)
              <= lax.broadcasted_iota(jnp.int32, (hq, hq), 0))
    nt = (((1,), (1,)), ((), ()))
    heads = (slice(0, HEAD_PAD), slice(HEAD_PAD, 2 * HEAD_PAD))

    def block(state, q, k0, width, hs, mask):
        m, acc = state
        s = lax.dot_general(q, k_ref[k0:k0 + width, hs], nt, preferred_element_type=F32)
        if mask is not None:
            s = jnp.where(mask, s, NEG_BIG)
        m_new = jnp.maximum(m, jnp.max(s, axis=-1, keepdims=True))
        p = jnp.exp((s - m_new).astype(BF16))
        acc = jnp.exp(m - m_new) * acc + jnp.dot(p, v_ref[k0:k0 + width, hs],
                                                 preferred_element_type=F32)
        return m_new, acc

    for qi in range(seq // tq):
        q0 = qi * tq
        qs = [q_ref[q0:q0 + tq, hs] for hs in heads]
        init = (jnp.full((tq, 1), NEG_BIG, F32), jnp.zeros((tq, HEAD_PAD), F32))
        states = [init, init]
        for j in range(qi):
            states = [block(st, q, j * tq, tq, hs, None) for st, q, hs in zip(states, qs, heads)]
        states = [block(st, q, q0, hq, hs, mask_a) for st, q, hs in zip(states, qs, heads)]
        outs = []
        for (m, acc), q, hs in zip(states, qs, heads):
            m_lo, acc_lo = block((m[hq:], acc[hq:]), q[hq:], q0 + hq, hq, hs, mask_b)
            acc = jnp.concatenate([acc[:hq], acc_lo], axis=0)
            denom = jnp.where(lane < V_HEAD, pltpu.roll(acc, V_HEAD, 1), 1.0)
            outs.append(acc / denom)
        o_ref[q0:q0 + tq, :] = jnp.where(lane < V_HEAD, outs[0],
                                         pltpu.roll(outs[1], V_HEAD, 1)).astype(BF16)


def _attention(q, k, v, batch, seq):
    t = q.shape[0]
    pairs = MLA_HEADS // 2
    blk = pl.BlockSpec((seq, 2 * HEAD_PAD), lambda b, p: (b, p))
    return pl.pallas_call(
        _attn_kernel,
        grid=(batch, pairs),
        in_specs=[blk, blk, blk],
        out_specs=pl.BlockSpec((seq, 2 * V_HEAD), lambda b, p: (b, p)),
        out_shape=jax.ShapeDtypeStruct((t, MLA_OUT), BF16),
        compiler_params=_cparams(("arbitrary", "arbitrary")),
        name="mla_attention",
    )(q, k, v)


POOL_LEVELS = 4


def _pool_rows(u_ref, pw_ref, ps_ref, car_ref, j):
    ts = u_ref.shape[0]

    @pl.when(j == 0)
    def _():
        car_ref[...] = jnp.zeros(car_ref.shape, F32)

    t_pos = j * ts + lax.broadcasted_iota(jnp.int32, (ts, 1), 0)
    outs = []
    for g, w in enumerate(POOL_WINDOWS):
        gs = slice(g * POOL_GROUP, (g + 1) * POOL_GROUP)
        ug = u_ref[:, gs].astype(F32)
        acc = ug
        span = 1
        level = 0
        while span < w:
            prev = car_ref[level, :, gs]
            car_ref[level, :, gs] = acc[ts - SUBLANES:, :]
            acc = acc + _shift_rows(acc, prev, span)
            span *= 2
            level += 1
        count = jnp.minimum(t_pos + 1, w).astype(F32)
        pooled = acc / count - ug
        mixed = jnp.dot(pooled.astype(BF16), pw_ref[0, g], preferred_element_type=F32)
        outs.append((mixed * ps_ref[0, :, gs]).astype(BF16))
    return jnp.concatenate(outs, axis=1)


SSD_STEP_CHUNKS = 8


def _ssd_kernel(zs_ref, xs_ref, bc_ref, dt_ref, dtb_ref, alog_ref, dful_ref, nw_ref,
                exp_ref, tri_ref, o_ref, state_ref, gated_ref):
    lc = SSD_CHUNK

    @pl.when(pl.program_id(1) == 0)
    def _():
        state_ref[...] = jnp.zeros(state_ref.shape, F32)

    rows = lax.broadcasted_iota(jnp.int32, (lc, lc), 0)
    cols = lax.broadcasted_iota(jnp.int32, (lc, lc), 1)
    causal = cols <= rows
    lane = lax.broadcasted_iota(jnp.int32, (lc, 2 * SSD_HEAD_DIM), 1)
    nt = (((1,), (1,)), ((), ()))
    tn = (((0,), (0,)), ((), ()))
    heads_per_group = SSD_HEADS // SSD_GROUPS
    neg_a = -jnp.exp(alog_ref[0])
    dtb = dtb_ref[0]
    dful = dful_ref[0]
    nw = nw_ref[0]

    for r0 in range(0, xs_ref.shape[0], lc):
        rs = slice(r0, r0 + lc)
        xs = _silu(xs_ref[rs, :].astype(F32))
        bc = _silu(bc_ref[rs, :].astype(F32)).astype(BF16)
        bm = bc[:, :SSD_GROUPS * SSD_STATE]
        cm = bc[:, SSD_GROUPS * SSD_STATE:]

        v = dt_ref[rs, :] + dtb
        dt = jnp.maximum(v, 0.0) + jnp.log(1.0 + jnp.exp(-jnp.abs(v)))
        a_cs = _split_dot_left(tri_ref[...], dt * neg_a)
        a_last = a_cs[lc - 1:lc, :]
        ea = jnp.exp(a_cs)
        stacked = jnp.concatenate([dt, ea, jnp.exp(a_last - a_cs)], axis=0).astype(BF16)
        full = jnp.dot(stacked, exp_ref[...], preferred_element_type=F32)
        dt_full = full[0:lc]
        ea_full = full[lc:2 * lc]
        ds_full = full[2 * lc:3 * lc]
        chunk_decay = _split_dot(ea[lc - SUBLANES:], exp_ref[...])[SUBLANES - 1:SUBLANES, :]
        a_cs_t = a_cs.T

        xdt = xs * dt_full
        xdt_b = xdt.astype(BF16)
        xds_b = (xdt * ds_full).astype(BF16)

        for g in range(SSD_GROUPS):
            gs = slice(g * GROUP_W, (g + 1) * GROUP_W)
            bg = bm[:, g * SSD_STATE:(g + 1) * SSD_STATE]
            cg = cm[:, g * SSD_STATE:(g + 1) * SSD_STATE]
            cb = lax.dot_general(cg, bg, nt, preferred_element_type=F32)
            state = state_ref[g]
            y_off = jnp.dot(cg, state.astype(BF16), preferred_element_type=F32) * ea_full[:, gs]
            s_new = lax.dot_general(bg, xds_b[:, gs], tn, preferred_element_type=F32)
            state_ref[g] = state * chunk_decay[:, gs] + s_new
            for pp in range(heads_per_group // 2):
                h0 = g * heads_per_group + 2 * pp
                ps = slice(h0 * SSD_HEAD_DIM, (h0 + 2) * SSD_HEAD_DIM)
                ys = []
                for hd in (h0, h0 + 1):
                    seg = a_cs[:, hd:hd + 1] - a_cs_t[hd:hd + 1, :]
                    decay = jnp.exp(jnp.where(causal, seg, NEG_BIG))
                    ys.append(jnp.dot((cb * decay).astype(BF16), xdt_b[:, ps],
                                      preferred_element_type=F32))
                y_pair = jnp.where(lane < SSD_HEAD_DIM, ys[0], ys[1])
                ls = slice(2 * pp * SSD_HEAD_DIM, (2 * pp + 2) * SSD_HEAD_DIM)
                y = y_pair + y_off[:, ls] + xs[:, ps] * dful[:, ps]
                gated_ref[rs, ps] = y * zs_ref[rs, ps].astype(F32)

        for g in range(SSD_GROUPS):
            gs = slice(g * GROUP_W, (g + 1) * GROUP_W)
            gt = gated_ref[rs, gs]
            ms = jnp.mean(gt * gt, axis=-1, keepdims=True)
            o_ref[rs, gs] = (gt * lax.rsqrt(ms + EPS) * nw[:, gs]).astype(o_ref.dtype)


def _ssd(proj, dt, dtb, alog, dful, nw, expand, tri, layer, batch, seq):
    t = proj.shape[0]
    lc = SSD_STEP_CHUNKS * SSD_CHUNK
    nc = seq // lc
    lb = lambda a: _layer_block(a, layer)
    return pl.pallas_call(
        _ssd_kernel,
        grid=(batch, nc),
        in_specs=[pl.BlockSpec((lc, SSD_INNER), lambda b, j: (b * nc + j, P_Z // SSD_INNER)),
                  pl.BlockSpec((lc, SSD_INNER), lambda b, j: (b * nc + j, P_XS // SSD_INNER)),
                  pl.BlockSpec((lc, BC_W), lambda b, j: (b * nc + j, P_BC // BC_W)),
                  pl.BlockSpec((lc, DT_PAD), lambda b, j: (b * nc + j, 0)),
                  lb(dtb), lb(alog), lb(dful), lb(nw),
                  _const_block(expand), _const_block(tri)],
        out_specs=pl.BlockSpec((lc, SSD_INNER), lambda b, j: (b * nc + j, 0)),
        out_shape=jax.ShapeDtypeStruct((t, SSD_INNER), BF16),
        scratch_shapes=[pltpu.VMEM((SSD_GROUPS, SSD_STATE, GROUP_W), F32),
                        pltpu.VMEM((lc, SSD_INNER), F32)],
        compiler_params=_cparams(("arbitrary", "arbitrary")),
        name="ssd_branch",
    )(proj, proj, proj, dt, dtb, alog, dful, nw, expand, tri)


MERGE_CAST_ROWS = 256


def _merge_kernel(x_ref, ga_ref, gb_ref, gc_ref, oa_ref, u_ref, oc_ref, mod_ref,
                  wb_ref, wo_ref, nw_ref, pw_ref, ps_ref, x_out, h_out, car_ref, wbs_ref, wos_ref):
    @pl.when((pl.program_id(0) == 0) & (pl.program_id(1) == 0))
    def _():
        for r0 in range(0, wbs_ref.shape[0], MERGE_CAST_ROWS):
            wbs_ref[r0:r0 + MERGE_CAST_ROWS, :] = wb_ref[0, r0:r0 + MERGE_CAST_ROWS, :].astype(BF16)
        for r0 in range(0, wos_ref.shape[0], MERGE_CAST_ROWS):
            wos_ref[r0:r0 + MERGE_CAST_ROWS, :] = wo_ref[0, r0:r0 + MERGE_CAST_ROWS, :].astype(BF16)

    m = mod_ref[0, 0]
    o_b = _pool_rows(u_ref, pw_ref, ps_ref, car_ref, pl.program_id(1))
    ya = jnp.dot(oa_ref[...], wbs_ref[0:MLA_OUT, :], preferred_element_type=F32)
    yb = jnp.dot(o_b, wbs_ref[MLA_OUT:MLA_OUT + POOL_DIM, :], preferred_element_type=F32)
    yc = jnp.dot(oc_ref[...], wbs_ref[MLA_OUT + POOL_DIM:, :], preferred_element_type=F32)
    merged = (_sigmoid(ga_ref[...].astype(F32)) * ya
              + _sigmoid(gb_ref[...].astype(F32)) * yb
              + _sigmoid(gc_ref[...].astype(F32)) * yc)
    y = jnp.dot(merged.astype(BF16), wos_ref[...], preferred_element_type=F32)
    x1 = x_ref[...] + m[2:3] * y
    x_out[...] = x1
    h_out[...] = _norm_mod(x1, nw_ref[0], m[3:4], m[4:5]).astype(BF16)


def _merge(xf, proj, o_a, o_c, mods, wb, wo, norm_w, pool_w, pool_scale, layer, batch, seq):
    t, d = xf.shape
    tm = 512
    per_seq = seq // tm
    col = lambda n, blk: pl.BlockSpec((tm, n), lambda b, j: (b * per_seq + j, blk))
    tile = lambda n: col(n, 0)
    return pl.pallas_call(
        _merge_kernel,
        grid=(batch, per_seq),
        in_specs=[tile(d), col(d, P_GATE // d), col(d, P_GATE // d + 1), col(d, P_GATE // d + 2),
                  tile(MLA_OUT), col(POOL_DIM, P_POOL // POOL_DIM), tile(SSD_INNER),
                  pl.BlockSpec((1, 1, 6, d), lambda b, j: (layer, b, 0, 0)),
                  _layer_block(wb, layer), _layer_block(wo, layer), _layer_block(norm_w, layer),
                  _layer_block(pool_w, layer), _layer_block(pool_scale, layer)],
        out_specs=[tile(d), tile(d)],
        out_shape=[jax.ShapeDtypeStruct((t, d), F32), jax.ShapeDtypeStruct((t, d), BF16)],
        scratch_shapes=[pltpu.VMEM((POOL_LEVELS, SUBLANES, POOL_DIM), F32),
                        pltpu.VMEM(wb.shape[1:], BF16), pltpu.VMEM(wo.shape[1:], BF16)],
        compiler_params=_cparams(("arbitrary", "arbitrary")),
        name="merge_out",
    )(xf, proj, proj, proj, o_a, proj, o_c, mods, wb, wo, norm_w, pool_w, pool_scale)


def _ffn_kernel(with_next, h_ref, x_ref, mod_ref, *rest):
    if with_next:
        modn_ref, nwn_ref, upw_ref, cw_ref, cb_ref, dw_ref, x_out, h_out, carry_ref, act_ref = rest
    else:
        upw_ref, cw_ref, cb_ref, dw_ref, x_out, carry_ref, act_ref = rest
    ts = h_ref.shape[0]

    @pl.when(pl.program_id(1) == 0)
    def _():
        carry_ref[...] = jnp.zeros(carry_ref.shape, F32)

    h = h_ref[...]

    def conv_half(cs):
        u = jnp.dot(h, upw_ref[0, :, cs], preferred_element_type=F32)
        prev = carry_ref[:, cs]
        carry_ref[:, cs] = u[ts - CONV_HALO:, :]
        return _causal_conv(u, prev, cw_ref[0, :, cs], cb_ref[0, :, cs], FFN_CONV)

    for c in range(FFN_NCHUNK):
        gate = conv_half(slice(c * FFN_CHUNK, (c + 1) * FFN_CHUNK))
        val = conv_half(slice(FFN_DIM + c * FFN_CHUNK, FFN_DIM + (c + 1) * FFN_CHUNK))
        act_ref[c] = (_silu(gate) * val).astype(BF16)
    y = jnp.dot(act_ref[0], dw_ref[0, 0:FFN_CHUNK, :], preferred_element_type=F32)
    for c in range(1, FFN_NCHUNK):
        y = y + jnp.dot(act_ref[c], dw_ref[0, c * FFN_CHUNK:(c + 1) * FFN_CHUNK, :],
                        preferred_element_type=F32)
    m = mod_ref[0, 0]
    x2 = x_ref[...] + m[5:6] * y
    x_out[...] = x2
    if with_next:
        mn = modn_ref[0, 0]
        h_out[...] = _norm_mod(x2, nwn_ref[0], mn[0:1], mn[1:2]).astype(BF16)


def _ffn(h2, x1, mods, upw, cw, cb, dw, norm1_w, layer, seq):
    t, d = x1.shape
    ts = 512
    per_seq = seq // ts
    batch = t // seq
    with_next = layer + 1 < mods.shape[0]
    tile = pl.BlockSpec((ts, d), lambda b, j: (b * per_seq + j, 0))
    modspec = lambda l: pl.BlockSpec((1, 1, 6, d), lambda b, j: (l, b, 0, 0))
    in_specs = [tile, tile, modspec(layer)]
    args = [h2, x1, mods]
    out_specs = [tile]
    out_shape = [jax.ShapeDtypeStruct((t, d), F32)]
    if with_next:
        in_specs += [modspec(layer + 1), _layer_block(norm1_w, layer + 1)]
        args += [mods, norm1_w]
        out_specs.append(tile)
        out_shape.append(jax.ShapeDtypeStruct((t, d), BF16))
    in_specs += [_layer_block(a, layer) for a in (upw, cw, cb, dw)]
    args += [upw, cw, cb, dw]
    outs = pl.pallas_call(
        functools.partial(_ffn_kernel, with_next),
        grid=(batch, per_seq),
        in_specs=in_specs,
        out_specs=out_specs,
        out_shape=out_shape,
        scratch_shapes=[pltpu.VMEM((CONV_HALO, 2 * FFN_DIM), F32),
                        pltpu.VMEM((FFN_NCHUNK, ts, FFN_CHUNK), BF16)],
        compiler_params=_cparams(("arbitrary", "arbitrary")),
        name="conv_mlp",
    )(*args)
    return (outs[0], outs[1]) if with_next else (outs[0], None)


def _relayout_kernel(pieces, zero_rows, w_ref, o_ref):
    for src0, width, dst0 in pieces:
        o_ref[0, dst0:dst0 + width, :] = w_ref[0, src0:src0 + width, :].astype(BF16)
    for dst0, width in zero_rows:
        o_ref[0, dst0:dst0 + width, :] = jnp.zeros((width, o_ref.shape[2]), BF16)


def _relayout_w_in(w_t, pieces, zero_rows):
    depth, n_in, d = w_t.shape
    tk = 256
    return pl.pallas_call(
        functools.partial(_relayout_kernel, pieces, zero_rows),
        grid=(depth, d // tk),
        in_specs=[pl.BlockSpec((1, n_in, tk), lambda l, i: (l, 0, i))],
        out_specs=pl.BlockSpec((1, PROJ_W, tk), lambda l, i: (l, 0, i)),
        out_shape=jax.ShapeDtypeStruct((depth, PROJ_W, d), BF16),
        compiler_params=_cparams(("arbitrary", "arbitrary")),
        name="w_in_relayout",
    )(w_t)


def _head_blocks(w, per_head, lo, hi):
    depth, kdim, _ = w.shape
    w = w.reshape(depth, kdim, MLA_HEADS, per_head)[..., lo:hi]
    w = jnp.pad(w, ((0, 0), (0, 0), (0, 0), (0, HEAD_PAD - (hi - lo))))
    return w.reshape(depth, kdim, MLA_HEADS * HEAD_PAD)


def _group_mean_matrix(sizes):
    blk = np.zeros((HEAD_PAD, HEAD_PAD), np.float32)
    start = 0
    for n in sizes:
        blk[start:start + n, start:start + n] = 1.0 / n
        start += n
    return jnp.asarray(np.kron(np.eye(2, dtype=np.float32), blk), BF16)


def _rows(v, pad_to=None):
    if pad_to is not None:
        v = jnp.pad(v, ((0, 0), (0, pad_to - v.shape[1])))
    return v[:, None, :]


def kernel(x, c, positions, ada_w, ada_b, norm1_w, w_in, q_a_norm, w_q_b, kv_a_norm, w_kv_b, q_norm, k_norm, pool_w, pool_scale, ssd_conv_w, ssd_conv_b, ssd_dt_bias, ssd_a_log, ssd_d, ssd_norm_w, w_branch, w_out, norm2_w, ffn_up, ffn_conv_w, ffn_conv_b, ffn_down):
    batch, seq, d = x.shape
    depth = ada_w.shape[0]
    t = batch * seq
    xf = x.reshape(t, d)

    mods = _mods(c, ada_w, ada_b).reshape(depth, batch, 6, d)

    gq = _group_mean_matrix((QK_NOPE, QK_ROPE))
    gk = _group_mean_matrix((QK_NOPE,))
    head_of_col = np.arange(SSD_INNER) // SSD_HEAD_DIM
    expand = jnp.asarray(np.arange(DT_PAD)[:, None] == head_of_col[None, :], BF16)
    tri = jnp.asarray(np.tril(np.ones((SSD_CHUNK, SSD_CHUNK), np.float32)), BF16)

    o = 0
    offs = {}
    for name, size in (("q_lat", Q_LORA), ("c_kv", KV_LORA), ("k_rope", QK_ROPE), ("pool", POOL_DIM),
                       ("z", SSD_INNER), ("xs", SSD_INNER), ("bc", BC_W),
                       ("dt", SSD_HEADS), ("gate", N_BRANCH * D_MODEL)):
        offs[name] = (o, o + size)
        o += size
    pieces = []
    for name, dst0 in (("z", P_Z), ("xs", P_XS), ("gate", P_GATE), ("pool", P_POOL), ("bc", P_BC),
                       ("q_lat", P_LAT), ("c_kv", P_LAT + Q_LORA)):
        pieces.append((offs[name][0], offs[name][1] - offs[name][0], dst0))
    tail0 = PROJ_W - HEAD_PAD
    pieces.append((offs["dt"][0], SSD_HEADS, tail0))
    pieces.append((offs["k_rope"][0], QK_ROPE, tail0 + QK_NOPE))
    zero_rows = ((tail0 + SSD_HEADS, QK_NOPE - SSD_HEADS), (tail0 + QK_HEAD, HEAD_PAD - QK_HEAD))
    w_main = _relayout_w_in(jnp.swapaxes(w_in, 1, 2), tuple(pieces), zero_rows)

    wq = _head_blocks(w_q_b, QK_HEAD, 0, QK_HEAD).astype(BF16)
    wkv = jnp.concatenate([_head_blocks(w_kv_b, QK_NOPE + V_HEAD, 0, QK_NOPE),
                           _head_blocks(w_kv_b, QK_NOPE + V_HEAD, QK_NOPE, QK_NOPE + V_HEAD)],
                          axis=2).astype(BF16)
    hw = MLA_HEADS * HEAD_PAD
    qw = jnp.tile(_rows(q_norm, HEAD_PAD), (1, 1, MLA_HEADS)) * (QK_HEAD ** -0.5)
    kw = jnp.tile(_rows(k_norm[:, :QK_NOPE], HEAD_PAD), (1, 1, MLA_HEADS))
    krw = _rows(jnp.pad(k_norm[:, QK_NOPE:], ((0, 0), (QK_NOPE, 0))), HEAD_PAD)
    qan = _rows(q_a_norm)
    kvan = _rows(kv_a_norm)

    pool_wb = pool_w.astype(BF16)
    pool_sc = _rows(pool_scale)
    conv_b = _rows(ssd_conv_b)
    dtb = _rows(ssd_dt_bias, DT_PAD)
    alog = _rows(ssd_a_log, DT_PAD)
    dful = _rows(jnp.repeat(ssd_d, SSD_HEAD_DIM, axis=1))
    ssd_nw = _rows(ssd_norm_w)
    wb = w_branch
    wo = w_out
    n1 = _rows(norm1_w)
    n2 = _rows(norm2_w)
    upw = ffn_up.astype(BF16)
    dw = ffn_down.astype(BF16)
    fcb = _rows(ffn_conv_b)

    h, cos_t, sin_t = _prenorm(xf, mods, n1, positions, 0, seq)
    for l in range(depth):
        proj, dt = _inproj(h, w_main, ssd_conv_w, conv_b, l, batch, seq)
        q, k, v = _mla_prep(proj, cos_t, sin_t, qan, kvan, wq, wkv, gq, gk, qw, kw, krw, l)
        o_a = _attention(q, k, v, batch, seq)
        o_c = _ssd(proj, dt, dtb, alog, dful, ssd_nw, expand, tri, l, batch, seq)
        x1, h2 = _merge(xf, proj, o_a, o_c, mods, wb, wo, n2, pool_wb, pool_sc, l, batch, seq)
        xf, h = _ffn(h2, x1, mods, upw, ffn_conv_w, fcb, dw, n1, l, seq)
    return xf.reshape(batch, seq, d)
```
